```python
import math, functools
import jax, jax.numpy as jnp
from jax import lax
import numpy as np

D_MODEL = 1024
BATCH = 32
SEQ = 256
DEPTH = 4
DEC_BATCH = 8
DEC_SEQ = 1024
PAST_LEN = 512

GRID_W = 64
HEAD_DIM = 64
MIX_WIDTH = D_MODEL
GROUP_WIDTH = MIX_WIDTH // 4
GROUP_HEADS = GROUP_WIDTH // HEAD_DIM
GLA_HEADS = GROUP_HEADS
GLA_LOWRANK = 16
GLA_TAU = 16.0
GLA_CHUNK = 64
SWA_HEADS = GROUP_HEADS
SWA_KV_HEADS = 2
SWA_WINDOW = 128
SWA_BLOCK = 128
NA_HEADS = GROUP_HEADS
NA_KH = 8
NA_KW = 16
GDN_HEADS = GROUP_HEADS
GDN_CONV = 3
GDN_CHUNK = 64
D_FF = 2816
N_EXPERTS = 8
TOP_K = 2
D_FF_EXPERT = 1024
N_DENSE = (DEPTH + 1) // 2
N_MOE = DEPTH // 2
ROPE_BASE = 10000.0
CTX_BLOCK = 128
EPS = 1e-6
NEG_INF = -1e30
SWA_KV_WIDTH = SWA_KV_HEADS * HEAD_DIM
IN_SPLITS = [GROUP_WIDTH] * 4 + [GLA_LOWRANK] * 2 + [GROUP_WIDTH, SWA_KV_WIDTH, SWA_KV_WIDTH] + [GROUP_WIDTH] * 3 + [GROUP_WIDTH] * 4 + [GDN_HEADS] * 4
IN_COLS = sum(IN_SPLITS)

kernel_name = 'hybrid_parallel_heads_diffusion_step'


def _rmsnorm(x, g):
    xf = x.astype(jnp.float32)
    y = xf * lax.rsqrt(jnp.mean(xf * xf, axis=-1, keepdims=True) + EPS)
    return (y * g.astype(jnp.float32)).astype(x.dtype)


def _l2norm(x):
    xf = x.astype(jnp.float32)
    return (xf * lax.rsqrt(jnp.sum(xf * xf, axis=-1, keepdims=True) + EPS)).astype(x.dtype)


def _split(x, sizes):
    cuts = [int(s) for s in np.cumsum(sizes)[:-1]]
    return jnp.split(x, cuts, axis=-1)


def _heads(t, n):
    b, s, _ = t.shape
    return t.reshape(b, s, n, -1).transpose(0, 2, 1, 3)


def _merge(t):
    b, h, s, d = t.shape
    return t.transpose(0, 2, 1, 3).reshape(b, s, h * d)


def _flip(t):
    return jnp.flip(t, axis=2)


def _rope_2d(x):
    n = x.shape[2]
    t = jnp.arange(n)
    rows = (t // GRID_W).astype(jnp.float32)
    cols = (t % GRID_W).astype(jnp.float32)
    half = HEAD_DIM // 2
    nf = half // 2
    inv = 1.0 / (ROPE_BASE ** (jnp.arange(nf, dtype=jnp.float32) / nf))

    def rot(xp, pos):
        ang = pos[:, None] * inv[None, :]
        cos, sin = jnp.cos(ang), jnp.sin(ang)
        x1, x2 = xp[..., :nf], xp[..., nf:]
        return jnp.concatenate([x1 * cos - x2 * sin, x1 * sin + x2 * cos], axis=-1)

    xf = x.astype(jnp.float32)
    out = jnp.concatenate([rot(xf[..., :half], rows), rot(xf[..., half:], cols)], axis=-1)
    return out.astype(x.dtype)


def _dwconv(x, w):
    k, c = w.shape
    return lax.conv_general_dilated(x, w[:, None, :].astype(x.dtype), window_strides=(1,),
                                    padding=[(k // 2, k // 2)],
                                    dimension_numbers=('NWC', 'WIO', 'NWC'),
                                    feature_group_count=c)


def _gla_chunked(q, k, v, log_a, s0):
    b, h, n, dk = q.shape
    dv = v.shape[-1]
    L = GLA_CHUNK
    nc = n // L
    f32 = jnp.float32
    qf = q.astype(f32).reshape(b, h, nc, L, dk) * (dk ** -0.5)
    kf = k.astype(f32).reshape(b, h, nc, L, dk)
    vf = v.astype(f32).reshape(b, h, nc, L, dv)
    cum = jnp.cumsum(log_a.astype(f32).reshape(b, h, nc, L, dk), axis=3)
    last = cum[:, :, :, -1:, :]
    q_dec = qf * jnp.exp(cum)
    k_inv = kf * jnp.exp(-cum)
    k_end = kf * jnp.exp(last - cum)
    causal = jnp.tril(jnp.ones((L, L), bool))
    att = jnp.where(causal, jnp.einsum('bhcid,bhcjd->bhcij', q_dec, k_inv), 0.0)
    o_intra = jnp.einsum('bhcij,bhcjv->bhciv', att, vf)

    def step(s, xs):
        qd, ke, dl, vc, oi = xs
        o = oi + jnp.einsum('bhid,bhdv->bhiv', qd, s)
        s = dl[..., None] * s + jnp.einsum('bhjd,bhjv->bhdv', ke, vc)
        return s, o

    xs = tuple(jnp.moveaxis(t, 2, 0) for t in (q_dec, k_end, jnp.exp(last[:, :, :, 0, :]), vf, o_intra))
    s_fin, o = lax.scan(step, s0.astype(f32), xs)
    o = jnp.moveaxis(o, 0, 2).reshape(b, h, n, dv)
    return o.astype(v.dtype), s_fin.astype(s0.dtype)


def _gdn_chunked(q, k, v, g, beta, s0):
    b, h, n, dk = q.shape
    dv = v.shape[-1]
    L = GDN_CHUNK
    nc = n // L
    f32 = jnp.float32
    qf = q.astype(f32).reshape(b, h, nc, L, dk)
    kf = k.astype(f32).reshape(b, h, nc, L, dk)
    vf = v.astype(f32).reshape(b, h, nc, L, dv)
    bf = beta.astype(f32).reshape(b, h, nc, L)
    gc = jnp.cumsum(g.astype(f32).reshape(b, h, nc, L), axis=-1)
    incl = jnp.tril(jnp.ones((L, L), bool))
    strict = jnp.tril(jnp.ones((L, L), bool), -1)
    diff = gc[..., :, None] - gc[..., None, :]
    dec = jnp.where(incl, jnp.exp(jnp.where(incl, diff, 0.0)), 0.0)
    kk = jnp.einsum('bhcid,bhcjd->bhcij', kf, kf)
    a_mat = jnp.where(strict, bf[..., :, None] * dec * kk, 0.0)
    eye = jnp.eye(L, dtype=f32)
    t_inv = lax.linalg.triangular_solve(eye + a_mat, jnp.broadcast_to(eye, a_mat.shape),
                                        left_side=True, lower=True, unit_diagonal=True)
    w_v = jnp.einsum('bhcij,bhcjv->bhciv', t_inv, bf[..., None] * vf)
    w_k = jnp.einsum('bhcij,bhcjd->bhcid', t_inv, (bf * jnp.exp(gc))[..., None] * kf)
    qk = jnp.einsum('bhcid,bhcjd->bhcij', qf, kf) * dec
    q_dec = qf * jnp.exp(gc)[..., None]
    k_end = kf * jnp.exp(gc[..., -1:] - gc)[..., None]
    g_last = jnp.exp(gc[..., -1])

    def step(s, xs):
        qd, qkc, wv, wk, ke, gl = xs
        u = wv - jnp.einsum('bhid,bhdv->bhiv', wk, s)
        o = jnp.einsum('bhid,bhdv->bhiv', qd, s) + jnp.einsum('bhij,bhjv->bhiv', qkc, u)
        s = gl[..., None, None] * s + jnp.einsum('bhjd,bhjv->bhdv', ke, u)
        return s, o

    xs = tuple(jnp.moveaxis(t, 2, 0) for t in (q_dec, qk, w_v, w_k, k_end, g_last))
    s_fin, o = lax.scan(step, s0.astype(f32), xs)
    o = jnp.moveaxis(o, 0, 2).reshape(b, h, n, dv)
    return o.astype(v.dtype), s_fin.astype(s0.dtype)


def _ctx_attention(q, k, v, sink):
    b, h, s, d = q.shape
    kvh = k.shape[1]
    g = h // kvh
    nqb = s // CTX_BLOCK
    scale = d ** -0.5
    qb = jnp.moveaxis(q.reshape(b, kvh, g, nqb, CTX_BLOCK, d), 3, 0)

    def block(qi):
        logits = jnp.einsum('bkgqd,bksd->bkgqs', qi, k).astype(jnp.float32) * scale
        if sink is not None:
            sk = jnp.broadcast_to(sink.astype(jnp.float32).reshape(1, kvh, g, 1, 1), logits.shape[:-1] + (1,))
            logits = jnp.concatenate([logits, sk], axis=-1)
        p = jax.nn.softmax(logits, axis=-1)[..., :s].astype(v.dtype)
        return jnp.einsum('bkgqs,bksd->bkgqd', p, v)

    out = lax.map(block, qb)
    return jnp.moveaxis(out, 0, 3).reshape(b, h, s, d)


def _swa_latent(q, k, v, ck, cv, sink):
    b, h, n, d = q.shape
    kvh = k.shape[1]
    g = h // kvh
    nb = n // SWA_BLOCK
    span = SWA_BLOCK + 2 * SWA_WINDOW
    scale = d ** -0.5
    qg = q.reshape(b, kvh, g, n, d)
    pad = ((0, 0), (0, 0), (SWA_WINDOW, SWA_WINDOW), (0, 0))
    kp = jnp.pad(k, pad)
    vp = jnp.pad(v, pad)
    sk = jnp.broadcast_to(sink.astype(jnp.float32).reshape(1, kvh, g, 1, 1), (b, kvh, g, SWA_BLOCK, 1))

    def block(i):
        start = i * SWA_BLOCK
        qi = lax.dynamic_slice_in_dim(qg, start, SWA_BLOCK, axis=3)
        ki = lax.dynamic_slice_in_dim(kp, start, span, axis=2)
        vi = lax.dynamic_slice_in_dim(vp, start, span, axis=2)
        qpos = start + jnp.arange(SWA_BLOCK)
        kpos = start - SWA_WINDOW + jnp.arange(span)
        mask = (jnp.abs(qpos[:, None] - kpos[None, :]) <= SWA_WINDOW) & (kpos[None, :] >= 0) & (kpos[None, :] < n)
        l_loc = jnp.where(mask, jnp.einsum('bkgqd,bksd->bkgqs', qi, ki).astype(jnp.float32) * scale, NEG_INF)
        l_ctx = jnp.einsum('bkgqd,bkpd->bkgqp', qi, ck).astype(jnp.float32) * scale
        p = jax.nn.softmax(jnp.concatenate([l_loc, l_ctx, sk], axis=-1), axis=-1).astype(v.dtype)
        return (jnp.einsum('bkgqs,bksd->bkgqd', p[..., :span], vi)
                + jnp.einsum('bkgqp,bkpd->bkgqd', p[..., span:-1], cv))

    out = lax.map(block, jnp.arange(nb))
    return out.transpose(1, 2, 3, 0, 4, 5).reshape(b, h, n, d)


def _na_latent(q, k, v, ck, cv, rpb):
    b, h, n, d = q.shape
    rows = n // GRID_W
    kh = min(NA_KH, rows)
    scale = d ** -0.5
    qg = q.reshape(b, h, rows, GRID_W, d)
    kg = k.reshape(b, h, rows, GRID_W, d)
    vg = v.reshape(b, h, rows, GRID_W, d)
    col = jnp.arange(GRID_W)
    cs = jnp.clip(col - NA_KW // 2, 0, GRID_W - NA_KW)
    col_mask = (col[None, :] >= cs[:, None]) & (col[None, :] < cs[:, None] + NA_KW)
    dc = jnp.clip(col[None, :] - col[:, None], -(NA_KW - 1), NA_KW - 1) + NA_KW - 1
    rpb_c = rpb.astype(jnp.float32)[:, :, dc]
    nloc = kh * GRID_W

    def row(r):
        rs = jnp.clip(r - kh // 2, 0, rows - kh)
        qr = lax.dynamic_index_in_dim(qg, r, axis=2, keepdims=False)
        kr = lax.dynamic_slice_in_dim(kg, rs, kh, axis=2)
        vr = lax.dynamic_slice_in_dim(vg, rs, kh, axis=2)
        dr = rs + jnp.arange(kh) - r + NA_KH - 1
        bias = jnp.take(rpb_c, dr, axis=1).transpose(0, 2, 1, 3)
        l_loc = jnp.einsum('bhqd,bhrkd->bhqrk', qr, kr).astype(jnp.float32) * scale + bias[None]
        l_loc = jnp.where(col_mask[:, None, :], l_loc, NEG_INF).reshape(b, h, GRID_W, nloc)
        l_ctx = jnp.einsum('bhqd,bhpd->bhqp', qr, ck).astype(jnp.float32) * scale
        p = jax.nn.softmax(jnp.concatenate([l_loc, l_ctx], axis=-1), axis=-1).astype(v.dtype)
        p_loc = p[..., :nloc].reshape(b, h, GRID_W, kh, GRID_W)
        return (jnp.einsum('bhqrk,bhrkd->bhqd', p_loc, vr)
                + jnp.einsum('bhqp,bhpd->bhqd', p[..., nloc:], cv))

    out = lax.map(row, jnp.arange(rows))
    return out.transpose(1, 2, 0, 3, 4).reshape(b, h, n, d)


def _decay(a, alog, dtb):
    g = -jnp.exp(alog.astype(jnp.float32)) * jax.nn.softplus(a.astype(jnp.float32) + dtb.astype(jnp.float32))
    return g.transpose(0, 2, 1)


def _token_mixers(h, p, cache):
    b, n, _ = h.shape
    latent = cache is not None
    (a_q, a_k, a_v, a_r, a_lf, a_lb,
     s_q, s_k, s_v,
     n_q, n_k, n_v,
     d_q, d_k, d_v, d_r, d_af, d_ab, d_bf, d_bb) = _split(h @ p['w_in'], IN_SPLITS)

    q, k, v = _heads(a_q, GLA_HEADS), _heads(a_k, GLA_HEADS), _heads(a_v, GLA_HEADS)
    la_f = _heads(jax.nn.log_sigmoid(a_lf @ p['gla_wup'][0] + p['gla_bup'][0]) / GLA_TAU, GLA_HEADS)
    la_b = _heads(jax.nn.log_sigmoid(a_lb @ p['gla_wup'][1] + p['gla_bup'][1]) / GLA_TAU, GLA_HEADS)
    s0 = cache['gla'] if latent else jnp.zeros((b, 2, GLA_HEADS, HEAD_DIM, HEAD_DIM), h.dtype)
    o_f, sa_f = _gla_chunked(q, k, v, la_f, s0[:, 0])
    o_b, sa_b = _gla_chunked(_flip(q), _flip(k), _flip(v), _flip(la_b), s0[:, 1])
    out_a = _rmsnorm(o_f + _flip(o_b), p['gla_onorm']) * jax.nn.silu(_heads(a_r, GLA_HEADS))

    q = _rmsnorm(_heads(s_q, SWA_HEADS), p['swa_qnorm'])
    swa_k = _rmsnorm(_heads(s_k, SWA_KV_HEADS), p['swa_knorm'])
    swa_v = _heads(s_v, SWA_KV_HEADS)
    if latent:
        out_b = _swa_latent(_rope_2d(q), _rope_2d(swa_k), swa_v, cache['swa_k'], cache['swa_v'], p['swa_sink'])
    else:
        out_b = _ctx_attention(q, swa_k, swa_v, p['swa_sink'])

    q = _rmsnorm(_heads(n_q, NA_HEADS), p['na_qnorm'])
    na_k = _rmsnorm(_heads(n_k, NA_HEADS), p['na_knorm'])
    na_v = _heads(n_v, NA_HEADS)
    if latent:
        out_c = _na_latent(q, na_k, na_v, cache['na_k'], cache['na_v'], p['na_rpb'])
    else:
        out_c = _ctx_attention(q, na_k, na_v, None)

    qkv = jax.nn.silu(_dwconv(jnp.concatenate([d_q, d_k, d_v], axis=-1), p['gdn_conv']))
    q, k, v = jnp.split(qkv, 3, axis=-1)
    q = _l2norm(_heads(q, GDN_HEADS)) * (HEAD_DIM ** -0.5)
    k = _l2norm(_heads(k, GDN_HEADS))
    v = _heads(v, GDN_HEADS)
    g_f = _decay(d_af, p['gdn_alog'][0], p['gdn_dtbias'][0])
    g_b = _decay(d_ab, p['gdn_alog'][1], p['gdn_dtbias'][1])
    beta_f = jax.nn.sigmoid(d_bf.astype(jnp.float32)).transpose(0, 2, 1)
    beta_b = jax.nn.sigmoid(d_bb.astype(jnp.float32)).transpose(0, 2, 1)
    s0 = cache['gdn'] if latent else jnp.zeros((b, 2, GDN_HEADS, HEAD_DIM, HEAD_DIM), h.dtype)
    o_f, sd_f = _gdn_chunked(q, k, v, g_f, beta_f, s0[:, 0])
    o_b, sd_b = _gdn_chunked(_flip(q), _flip(k), _flip(v), _flip(g_b), _flip(beta_b), s0[:, 1])
    out_d = _rmsnorm(o_f + _flip(o_b), p['gdn_onorm']) * jax.nn.silu(_heads(d_r, GDN_HEADS))

    y = jnp.concatenate([_merge(out_a), _merge(out_b), _merge(out_c), _merge(out_d)], axis=-1) @ p['w_out']
    if latent:
        return y, None
    return y, (swa_k, swa_v, na_k, na_v, jnp.stack([sa_f, sa_b], axis=1), jnp.stack([sd_f, sd_b], axis=1))


def _swiglu(h, w1, w3, w2):
    return (jax.nn.silu(h @ w1) * (h @ w3)) @ w2


def _moe(h, router, w1, w3, w2):
    logits = (h @ router).astype(jnp.float32)
    top_v, top_i = lax.top_k(logits, TOP_K)
    top_w = jax.nn.softmax(top_v, axis=-1)
    gates = jnp.sum(jax.nn.one_hot(top_i, N_EXPERTS, dtype=jnp.float32) * top_w[..., None], axis=-2)
    out = jnp.zeros_like(h)
    for e in range(N_EXPERTS):
        out = out + gates[..., e:e + 1].astype(h.dtype) * _swiglu(h, w1[e], w3[e], w2[e])
    return out


def _modulation(cvec, w, b):
    return (jax.nn.silu(cvec) @ w + b)[:, None, :]


def _layer(x, mod, p, cache, ffn):
    sh1, sc1, g1, sh2, sc2, g2 = jnp.split(mod, 6, axis=-1)
    y, ctx_state = _token_mixers(_rmsnorm(x, p['norm1']) * (1 + sc1) + sh1, p, cache)
    x = x + g1 * y
    x = x + g2 * ffn(_rmsnorm(x, p['norm2']) * (1 + sc2) + sh2)
    return x, ctx_state


def setup_inputs(seed: int = 0) -> dict:
    key = jax.random.key(seed)
    keys = iter(jax.random.split(key, 48))
    f32 = jnp.float32

    def nrm(shape, scale=1.0):
        return jax.random.normal(next(keys), shape, f32) * scale

    def gain(shape):
        return 1.0 + nrm(shape, 0.02)

    lo, hi = math.log(1e-3), math.log(1e-1)
    dt = jnp.exp(jax.random.uniform(next(keys), (DEPTH, 2, GDN_HEADS), f32) * (hi - lo) + lo)
    return {
        'x_prompt': nrm((BATCH, SEQ, D_MODEL)),
        'x_sample': nrm((DEC_BATCH, DEC_SEQ, D_MODEL)),
        'cache_swa_k': nrm((DEC_BATCH, DEPTH, SWA_KV_HEADS, PAST_LEN, HEAD_DIM)),
        'cache_swa_v': nrm((DEC_BATCH, DEPTH, SWA_KV_HEADS, PAST_LEN, HEAD_DIM)),
        'cache_na_k': nrm((DEC_BATCH, DEPTH, NA_HEADS, PAST_LEN, HEAD_DIM)),
        'cache_na_v': nrm((DEC_BATCH, DEPTH, NA_HEADS, PAST_LEN, HEAD_DIM)),
        'state_gla': nrm((DEC_BATCH, DEPTH, 2, GLA_HEADS, HEAD_DIM, HEAD_DIM), 0.3),
        'state_gdn': nrm((DEC_BATCH, DEPTH, 2, GDN_HEADS, HEAD_DIM, HEAD_DIM), 0.1),
        'c': nrm((DEC_BATCH, D_MODEL)),
        'c_ctx': nrm((D_MODEL,)),
        'w_mod': nrm((DEPTH, D_MODEL, 6 * D_MODEL), 0.5 * D_MODEL ** -0.5),
        'b_mod': nrm((DEPTH, 6 * D_MODEL), 0.02),
        'norm1_g': gain((DEPTH, D_MODEL)),
        'norm2_g': gain((DEPTH, D_MODEL)),
        'w_in': nrm((DEPTH, D_MODEL, IN_COLS), D_MODEL ** -0.5),
        'w_out': nrm((DEPTH, MIX_WIDTH, D_MODEL), MIX_WIDTH ** -0.5),
        'gla_wup': nrm((DEPTH, 2, GLA_LOWRANK, GROUP_WIDTH), GLA_LOWRANK ** -0.5),
        'gla_bup': nrm((DEPTH, 2, GROUP_WIDTH), 0.1),
        'gla_onorm': gain((DEPTH, HEAD_DIM)),
        'swa_qnorm': gain((DEPTH, HEAD_DIM)),
        'swa_knorm': gain((DEPTH, HEAD_DIM)),
        'swa_sink': nrm((DEPTH, SWA_HEADS), 0.5),
        'na_qnorm': gain((DEPTH, HEAD_DIM)),
        'na_knorm': gain((DEPTH, HEAD_DIM)),
        'na_rpb': nrm((DEPTH, NA_HEADS, 2 * NA_KH - 1, 2 * NA_KW - 1), 0.2),
        'gdn_conv': nrm((DEPTH, GDN_CONV, 3 * GROUP_WIDTH), GDN_CONV ** -0.5),
        'gdn_alog': jnp.log(jax.random.uniform(next(keys), (DEPTH, 2, GDN_HEADS), f32, 1.0, 16.0)),
        'gdn_dtbias': dt + jnp.log(-jnp.expm1(-dt)),
        'gdn_onorm': gain((DEPTH, HEAD_DIM)),
        'ffn_w1': nrm((N_DENSE, D_MODEL, D_FF), D_MODEL ** -0.5),
        'ffn_w3': nrm((N_DENSE, D_MODEL, D_FF), D_MODEL ** -0.5),
        'ffn_w2': nrm((N_DENSE, D_FF, D_MODEL), D_FF ** -0.5),
        'moe_router': nrm((N_MOE, D_MODEL, N_EXPERTS), D_MODEL ** -0.5),
        'moe_w1': nrm((N_MOE, N_EXPERTS, D_MODEL, D_FF_EXPERT), D_MODEL ** -0.5),
        'moe_w3': nrm((N_MOE, N_EXPERTS, D_MODEL, D_FF_EXPERT), D_MODEL ** -0.5),
        'moe_w2': nrm((N_MOE, N_EXPERTS, D_FF_EXPERT, D_MODEL), D_FF_EXPERT ** -0.5),
    }


def reference(x_prompt, x_sample, cache_swa_k, cache_swa_v, cache_na_k, cache_na_v, state_gla, state_gdn,
              c, c_ctx, w_mod, b_mod, norm1_g, norm2_g, w_in, w_out, gla_wup, gla_bup, gla_onorm,
              swa_qnorm, swa_knorm, swa_sink, na_qnorm, na_knorm, na_rpb, gdn_conv, gdn_alog, gdn_dtbias,
              gdn_onorm, ffn_w1, ffn_w3, ffn_w2, moe_router, moe_w1, moe_w3, moe_w2):
    xp, xs = x_prompt, x_sample
    l_swa_k, l_swa_v, l_na_k, l_na_v, l_gla, l_gdn = [], [], [], [], [], []
    for l in range(DEPTH):
        p = {'norm1': norm1_g[l], 'norm2': norm2_g[l], 'w_in': w_in[l], 'w_out': w_out[l],
             'gla_wup': gla_wup[l], 'gla_bup': gla_bup[l], 'gla_onorm': gla_onorm[l],
             'swa_qnorm': swa_qnorm[l], 'swa_knorm': swa_knorm[l], 'swa_sink': swa_sink[l],
             'na_qnorm': na_qnorm[l], 'na_knorm': na_knorm[l], 'na_rpb': na_rpb[l],
             'gdn_conv': gdn_conv[l], 'gdn_alog': gdn_alog[l], 'gdn_dtbias': gdn_dtbias[l],
             'gdn_onorm': gdn_onorm[l]}
        if l % 2 == 0:
            ffn = functools.partial(_swiglu, w1=ffn_w1[l // 2], w3=ffn_w3[l // 2], w2=ffn_w2[l // 2])
        else:
            ffn = functools.partial(_moe, router=moe_router[l // 2], w1=moe_w1[l // 2],
                                    w3=moe_w3[l // 2], w2=moe_w2[l // 2])
        cache = {'swa_k': cache_swa_k[:, l], 'swa_v': cache_swa_v[:, l],
                 'na_k': cache_na_k[:, l], 'na_v': cache_na_v[:, l],
                 'gla': state_gla[:, l], 'gdn': state_gdn[:, l]}
        xp, st = _layer(xp, _modulation(c_ctx[None, :], w_mod[l], b_mod[l]), p, None, ffn)
        xs, _ = _layer(xs, _modulation(c, w_mod[l], b_mod[l]), p, cache, ffn)
        for lst, t in zip((l_swa_k, l_swa_v, l_na_k, l_na_v, l_gla, l_gdn), st):
            lst.append(t)
    new_swa_k = jnp.stack(l_swa_k, axis=1)
    new_swa_v = jnp.stack(l_swa_v, axis=1)
    new_na_k = jnp.stack(l_na_k, axis=1)
    new_na_v = jnp.stack(l_na_v, axis=1)
    new_gla = jnp.stack(l_gla, axis=1)
    new_gdn = jnp.stack(l_gdn, axis=1)
    return (xp, xs, new_swa_k, new_swa_v, new_na_k, new_na_v, new_gla, new_gdn)
```

```python
import functools
import math

import numpy as np
import jax
import jax.numpy as jnp
from jax import lax
from jax.experimental import pallas as pl
from jax.experimental.pallas import tpu as pltpu

D_MODEL = 1024
BATCH = 32
SEQ = 256
DEPTH = 4
DEC_BATCH = 8
DEC_SEQ = 1024
PAST_LEN = 512

GRID_W = 64
HEAD_DIM = 64
GROUP_WIDTH = D_MODEL // 4
GROUP_HEADS = GROUP_WIDTH // HEAD_DIM
GLA_HEADS = GROUP_HEADS
GLA_LOWRANK = 16
GLA_TAU = 16.0
GLA_CHUNK = 64
SWA_HEADS = GROUP_HEADS
SWA_KV_HEADS = 2
SWA_WINDOW = 128
SWA_BLOCK = 128
NA_HEADS = GROUP_HEADS
NA_KH = 8
NA_KW = 16
GDN_HEADS = GROUP_HEADS
GDN_CONV = 3
GDN_CHUNK = 64
D_FF = 2816
N_EXPERTS = 8
TOP_K = 2
D_FF_EXPERT = 1024
ROPE_BASE = 10000.0
CTX_BLOCK = 128
EPS = 1e-6
NEG_INF = -1e30
SWA_KV_WIDTH = SWA_KV_HEADS * HEAD_DIM
IN_SPLITS = ([GROUP_WIDTH] * 4 + [GLA_LOWRANK] * 2 + [GROUP_WIDTH, SWA_KV_WIDTH, SWA_KV_WIDTH]
             + [GROUP_WIDTH] * 3 + [GROUP_WIDTH] * 4 + [GDN_HEADS] * 4)
IN_COLS = sum(IN_SPLITS)

LANES = 128
VMEM_LIMIT_BYTES = 56 * 1024 * 1024

N_PROMPT_TOK = BATCH * SEQ
N_SAMPLE_TOK = DEC_BATCH * DEC_SEQ
N_TOK = N_PROMPT_TOK + N_SAMPLE_TOK
N_MOD_ROWS = 16
MOD_W = 6 * D_MODEL
TM = 512
IN_COLS_PAD = -(-IN_COLS // LANES) * LANES


def _mod_row(i):
    n_prompt_tiles = N_PROMPT_TOK // TM
    return jnp.where(i < n_prompt_tiles, 0, 1 + (i - n_prompt_tiles) // (DEC_SEQ // TM))


def _cparams(*sem):
    return pltpu.CompilerParams(dimension_semantics=sem, vmem_limit_bytes=VMEM_LIMIT_BYTES)


def _mod_kernel(c_ref, w_ref, b_ref, o_ref):
    c = c_ref[...]
    s = c * jax.nn.sigmoid(c)
    o_ref[...] = jnp.dot(s, w_ref[...], preferred_element_type=jnp.float32,
                         precision=lax.Precision.HIGHEST) + b_ref[...]


def _modulation_all(cvec, w_mod, b_mod):
    tn = 1536
    return pl.pallas_call(
        _mod_kernel,
        grid=(DEPTH, MOD_W // tn),
        in_specs=[pl.BlockSpec((N_MOD_ROWS, D_MODEL), lambda l, j: (0, 0)),
                  pl.BlockSpec((None, D_MODEL, tn), lambda l, j: (l, 0, j)),
                  pl.BlockSpec((None, 1, tn), lambda l, j: (l, 0, j))],
        out_specs=pl.BlockSpec((None, N_MOD_ROWS, tn), lambda l, j: (l, 0, j)),
        out_shape=jax.ShapeDtypeStruct((DEPTH, N_MOD_ROWS, MOD_W), jnp.float32),
        compiler_params=_cparams("arbitrary", "arbitrary"),
        name="modulation",
    )(cvec, w_mod, b_mod.reshape(DEPTH, 1, MOD_W))


def _modulated_norm(x, g, shift, scale):
    y = x * lax.rsqrt(jnp.mean(x * x, axis=-1, keepdims=True) + EPS) * g
    return y * (1.0 + scale) + shift


def _in_proj_kernel(x_ref, mod_ref, g_ref, w_ref, o_ref):
    h = _modulated_norm(x_ref[...], g_ref[...], mod_ref[:, 0:D_MODEL], mod_ref[:, D_MODEL:2 * D_MODEL])
    o_ref[...] = jnp.dot(h.astype(jnp.bfloat16), w_ref[...], preferred_element_type=jnp.float32)


def _in_proj(x, mod, g, w):
    return pl.pallas_call(
        _in_proj_kernel,
        grid=(N_TOK // TM,),
        in_specs=[pl.BlockSpec((TM, D_MODEL), lambda i: (i, 0)),
                  pl.BlockSpec((None, 1, MOD_W), lambda i: (_mod_row(i), 0, 0)),
                  pl.BlockSpec((1, D_MODEL), lambda i: (0, 0)),
                  pl.BlockSpec((D_MODEL, IN_COLS_PAD), lambda i: (0, 0), pipeline_mode=pl.Buffered(1))],
        out_specs=pl.BlockSpec((TM, IN_COLS_PAD), lambda i: (i, 0)),
        out_shape=jax.ShapeDtypeStruct((N_TOK, IN_COLS_PAD), jnp.float32),
        compiler_params=_cparams("arbitrary"),
        name="in_proj",
    )(x, mod, g, w)


def _out_proj_kernel(mix_ref, x_ref, mod_ref, w_ref, o_ref):
    y = jnp.dot(mix_ref[...].astype(jnp.bfloat16), w_ref[...], preferred_element_type=jnp.float32)
    o_ref[...] = x_ref[...] + mod_ref[:, 2 * D_MODEL:3 * D_MODEL] * y


def _out_proj(mix, x, mod, w):
    return pl.pallas_call(
        _out_proj_kernel,
        grid=(N_TOK // TM,),
        in_specs=[pl.BlockSpec((TM, D_MODEL), lambda i: (i, 0)),
                  pl.BlockSpec((TM, D_MODEL), lambda i: (i, 0)),
                  pl.BlockSpec((None, 1, MOD_W), lambda i: (_mod_row(i), 0, 0)),
                  pl.BlockSpec((D_MODEL, D_MODEL), lambda i: (0, 0), pipeline_mode=pl.Buffered(1))],
        out_specs=pl.BlockSpec((TM, D_MODEL), lambda i: (i, 0)),
        out_shape=jax.ShapeDtypeStruct((N_TOK, D_MODEL), jnp.float32),
        compiler_params=_cparams("arbitrary"),
        name="out_proj",
    )(mix, x, mod, w)


FF_CHUNK = D_FF // 2


def _ffn_kernel(x_ref, mod_ref, g_ref, w1_ref, w3_ref, w2_ref, o_ref):
    x = x_ref[...]
    h = _modulated_norm(x, g_ref[...], mod_ref[:, 3 * D_MODEL:4 * D_MODEL], mod_ref[:, 4 * D_MODEL:5 * D_MODEL])
    hb = h.astype(jnp.bfloat16)
    y = jnp.zeros((TM, D_MODEL), jnp.float32)
    for c0 in range(0, D_FF, FF_CHUNK):
        a = jnp.dot(hb, w1_ref[:, c0:c0 + FF_CHUNK], preferred_element_type=jnp.float32)
        b = jnp.dot(hb, w3_ref[:, c0:c0 + FF_CHUNK], preferred_element_type=jnp.float32)
        s = (a * jax.nn.sigmoid(a) * b).astype(jnp.bfloat16)
        y = y + jnp.dot(s, w2_ref[c0:c0 + FF_CHUNK, :], preferred_element_type=jnp.float32)
    o_ref[...] = x + mod_ref[:, 5 * D_MODEL:6 * D_MODEL] * y


def _ffn(x, mod, g, w1, w3, w2):
    resident = dict(pipeline_mode=pl.Buffered(1))
    return pl.pallas_call(
        _ffn_kernel,
        grid=(N_TOK // TM,),
        in_specs=[pl.BlockSpec((TM, D_MODEL), lambda i: (i, 0)),
                  pl.BlockSpec((None, 1, MOD_W), lambda i: (_mod_row(i), 0, 0)),
                  pl.BlockSpec((1, D_MODEL), lambda i: (0, 0)),
                  pl.BlockSpec((D_MODEL, D_FF), lambda i: (0, 0), **resident),
                  pl.BlockSpec((D_MODEL, D_FF), lambda i: (0, 0), **resident),
                  pl.BlockSpec((D_FF, D_MODEL), lambda i: (0, 0), **resident)],
        out_specs=pl.BlockSpec((TM, D_MODEL), lambda i: (i, 0)),
        out_shape=jax.ShapeDtypeStruct((N_TOK, D_MODEL), jnp.float32),
        compiler_params=_cparams("arbitrary"),
        name="ffn_dense",
    )(x, mod, g, w1, w3, w2)


def _moe_kernel(x_ref, mod_ref, g_ref, r_ref, w1_ref, w3_ref, w2_ref, o_ref, hb_ref, gate_ref):
    e = pl.program_id(1)

    @pl.when(e == 0)
    def _():
        h = _modulated_norm(x_ref[...], g_ref[...], mod_ref[:, 3 * D_MODEL:4 * D_MODEL],
                            mod_ref[:, 4 * D_MODEL:5 * D_MODEL])
        hb_ref[...] = h.astype(jnp.bfloat16)
        logits = jnp.dot(h, r_ref[...], preferred_element_type=jnp.float32,
                         precision=lax.Precision.HIGHEST)
        lane = lax.broadcasted_iota(jnp.int32, logits.shape, 1)
        v1 = jnp.max(logits, axis=-1, keepdims=True)
        i1 = jnp.min(jnp.where(logits == v1, lane, N_EXPERTS), axis=-1, keepdims=True)
        rest = jnp.where(lane == i1, -jnp.inf, logits)
        v2 = jnp.max(rest, axis=-1, keepdims=True)
        i2 = jnp.min(jnp.where(rest == v2, lane, N_EXPERTS), axis=-1, keepdims=True)
        p2 = jnp.exp(v2 - v1)
        den = 1.0 + p2
        gate_ref[...] = jnp.where(lane == i1, 1.0 / den, 0.0) + jnp.where(lane == i2, p2 / den, 0.0)
        o_ref[...] = jnp.zeros_like(o_ref)

    hb = hb_ref[...]
    a = jnp.dot(hb, w1_ref[...], preferred_element_type=jnp.float32)
    b = jnp.dot(hb, w3_ref[...], preferred_element_type=jnp.float32)
    s = (a * jax.nn.sigmoid(a) * b).astype(jnp.bfloat16)
    y = jnp.dot(s, w2_ref[...], preferred_element_type=jnp.float32)
    gates = gate_ref[...]
    lane = lax.broadcasted_iota(jnp.int32, gates.shape, 1)
    ge = jnp.sum(jnp.where(lane == e, gates, 0.0), axis=-1, keepdims=True)
    o_ref[...] += ge * y

    @pl.when(e == N_EXPERTS - 1)
    def _():
        o_ref[...] = x_ref[...] + mod_ref[:, 5 * D_MODEL:6 * D_MODEL] * o_ref[...]


def _moe(x, mod, g, router, w1, w3, w2):
    return pl.pallas_call(
        _moe_kernel,
        grid=(N_TOK // TM, N_EXPERTS),
        in_specs=[pl.BlockSpec((TM, D_MODEL), lambda i, e: (i, 0)),
                  pl.BlockSpec((None, 1, MOD_W), lambda i, e: (_mod_row(i), 0, 0)),
                  pl.BlockSpec((1, D_MODEL), lambda i, e: (0, 0)),
                  pl.BlockSpec((D_MODEL, N_EXPERTS), lambda i, e: (0, 0)),
                  pl.BlockSpec((None, D_MODEL, D_FF_EXPERT), lambda i, e: (e, 0, 0)),
                  pl.BlockSpec((None, D_MODEL, D_FF_EXPERT), lambda i, e: (e, 0, 0)),
                  pl.BlockSpec((None, D_FF_EXPERT, D_MODEL), lambda i, e: (e, 0, 0))],
        out_specs=pl.BlockSpec((TM, D_MODEL), lambda i, e: (i, 0)),
        out_shape=jax.ShapeDtypeStruct((N_TOK, D_MODEL), jnp.float32),
        scratch_shapes=[pltpu.VMEM((TM, D_MODEL), jnp.bfloat16),
                        pltpu.VMEM((TM, N_EXPERTS), jnp.float32)],
        compiler_params=_cparams("arbitrary", "arbitrary"),
        name="ffn_moe",
    )(x, mod, g, router, w1, w3, w2)


def _rmsnorm(x, g):
    y = x * lax.rsqrt(jnp.mean(x * x, axis=-1, keepdims=True) + EPS)
    return y * g


def _l2norm(x):
    return x * lax.rsqrt(jnp.sum(x * x, axis=-1, keepdims=True) + EPS)


def _heads(t, n):
    b, s, _ = t.shape
    return t.reshape(b, s, n, -1).transpose(0, 2, 1, 3)


def _merge(t):
    b, h, s, d = t.shape
    return t.transpose(0, 2, 1, 3).reshape(b, s, h * d)


def _flip(t):
    return jnp.flip(t, axis=2)


def _rope_2d(x):
    n = x.shape[2]
    t = jnp.arange(n)
    rows = (t // GRID_W).astype(jnp.float32)
    cols = (t % GRID_W).astype(jnp.float32)
    half = HEAD_DIM // 2
    nf = half // 2
    inv = 1.0 / (ROPE_BASE ** (jnp.arange(nf, dtype=jnp.float32) / nf))

    def rot(xp, pos):
        ang = pos[:, None] * inv[None, :]
        cos, sin = jnp.cos(ang), jnp.sin(ang)
        x1, x2 = xp[..., :nf], xp[..., nf:]
        return jnp.concatenate([x1 * cos - x2 * sin, x1 * sin + x2 * cos], axis=-1)

    return jnp.concatenate([rot(x[..., :half], rows), rot(x[..., half:], cols)], axis=-1)


def _dwconv(x, w):
    k, c = w.shape
    return lax.conv_general_dilated(x, w[:, None, :], window_strides=(1,), padding=[(k // 2, k // 2)],
                                    dimension_numbers=('NWC', 'WIO', 'NWC'), feature_group_count=c)


def _gla_chunked(q, k, v, log_a, s0):
    b, h, n, dk = q.shape
    dv = v.shape[-1]
    L = GLA_CHUNK
    nc = n // L
    qf = q.reshape(b, h, nc, L, dk) * (dk ** -0.5)
    kf = k.reshape(b, h, nc, L, dk)
    vf = v.reshape(b, h, nc, L, dv)
    cum = jnp.cumsum(log_a.reshape(b, h, nc, L, dk), axis=3)
    last = cum[:, :, :, -1:, :]
    q_dec = qf * jnp.exp(cum)
    k_inv = kf * jnp.exp(-cum)
    k_end = kf * jnp.exp(last - cum)
    causal = jnp.tril(jnp.ones((L, L), bool))
    att = jnp.where(causal, jnp.einsum('bhcid,bhcjd->bhcij', q_dec, k_inv), 0.0)
    o_intra = jnp.einsum('bhcij,bhcjv->bhciv', att, vf)

    def step(s, xs):
        qd, ke, dl, vc, oi = xs
        o = oi + jnp.einsum('bhid,bhdv->bhiv', qd, s)
        s = dl[..., None] * s + jnp.einsum('bhjd,bhjv->bhdv', ke, vc)
        return s, o

    xs = tuple(jnp.moveaxis(t, 2, 0) for t in (q_dec, k_end, jnp.exp(last[:, :, :, 0, :]), vf, o_intra))
    s_fin, o = lax.scan(step, s0, xs)
    return jnp.moveaxis(o, 0, 2).reshape(b, h, n, dv), s_fin


def _gdn_chunked(q, k, v, g, beta, s0):
    b, h, n, dk = q.shape
    dv = v.shape[-1]
    L = GDN_CHUNK
    nc = n // L
    f32 = jnp.float32
    qf = q.reshape(b, h, nc, L, dk)
    kf = k.reshape(b, h, nc, L, dk)
    vf = v.reshape(b, h, nc, L, dv)
    bf = beta.reshape(b, h, nc, L)
    gc = jnp.cumsum(g.reshape(b, h, nc, L), axis=-1)
    incl = jnp.tril(jnp.ones((L, L), bool))
    strict = jnp.tril(jnp.ones((L, L), bool), -1)
    diff = gc[..., :, None] - gc[..., None, :]
    dec = jnp.where(incl, jnp.exp(jnp.where(incl, diff, 0.0)), 0.0)
    kk = jnp.einsum('bhcid,bhcjd->bhcij', kf, kf)
    a_mat = jnp.where(strict, bf[..., :, None] * dec * kk, 0.0)
    eye = jnp.eye(L, dtype=f32)
    t_inv = lax.linalg.triangular_solve(eye + a_mat, jnp.broadcast_to(eye, a_mat.shape),
                                        left_side=True, lower=True, unit_diagonal=True)
    w_v = jnp.einsum('bhcij,bhcjv->bhciv', t_inv, bf[..., None] * vf)
    w_k = jnp.einsum('bhcij,bhcjd->bhcid', t_inv, (bf * jnp.exp(gc))[..., None] * kf)
    qk = jnp.einsum('bhcid,bhcjd->bhcij', qf, kf) * dec
    q_dec = qf * jnp.exp(gc)[..., None]
    k_end = kf * jnp.exp(gc[..., -1:] - gc)[..., None]
    g_last = jnp.exp(gc[..., -1])

    def step(s, xs):
        qd, qkc, wv, wk, ke, gl = xs
        u = wv - jnp.einsum('bhid,bhdv->bhiv', wk, s)
        o = jnp.einsum('bhid,bhdv->bhiv', qd, s) + jnp.einsum('bhij,bhjv->bhiv', qkc, u)
        s = gl[..., None, None] * s + jnp.einsum('bhjd,bhjv->bhdv', ke, u)
        return s, o

    xs = tuple(jnp.moveaxis(t, 2, 0) for t in (q_dec, qk, w_v, w_k, k_end, g_last))
    s_fin, o = lax.scan(step, s0, xs)
    return jnp.moveaxis(o, 0, 2).reshape(b, h, n, dv), s_fin


def _ctx_attention(q, k, v, sink):
    b, h, s, d = q.shape
    kvh = k.shape[1]
    g = h // kvh
    nqb = s // CTX_BLOCK
    scale = d ** -0.5
    qb = jnp.moveaxis(q.reshape(b, kvh, g, nqb, CTX_BLOCK, d), 3, 0)

    def block(qi):
        logits = jnp.einsum('bkgqd,bksd->bkgqs', qi, k) * scale
        if sink is not None:
            sk = jnp.broadcast_to(sink.reshape(1, kvh, g, 1, 1), logits.shape[:-1] + (1,))
            logits = jnp.concatenate([logits, sk], axis=-1)
        p = jax.nn.softmax(logits, axis=-1)[..., :s]
        return jnp.einsum('bkgqs,bksd->bkgqd', p, v)

    out = lax.map(block, qb)
    return jnp.moveaxis(out, 0, 3).reshape(b, h, s, d)


def _swa_latent(q, k, v, ck, cv, sink):
    b, h, n, d = q.shape
    kvh = k.shape[1]
    g = h // kvh
    nb = n // SWA_BLOCK
    span = SWA_BLOCK + 2 * SWA_WINDOW
    scale = d ** -0.5
    qg = q.reshape(b, kvh, g, n, d)
    pad = ((0, 0), (0, 0), (SWA_WINDOW, SWA_WINDOW), (0, 0))
    kp = jnp.pad(k, pad)
    vp = jnp.pad(v, pad)
    sk = jnp.broadcast_to(sink.reshape(1, kvh, g, 1, 1), (b, kvh, g, SWA_BLOCK, 1))

    def block(i):
        start = i * SWA_BLOCK
        qi = lax.dynamic_slice_in_dim(qg, start, SWA_BLOCK, axis=3)
        ki = lax.dynamic_slice_in_dim(kp, start, span, axis=2)
        vi = lax.dynamic_slice_in_dim(vp, start, span, axis=2)
        qpos = start + jnp.arange(SWA_BLOCK)
        kpos = start - SWA_WINDOW + jnp.arange(span)
        mask = (jnp.abs(qpos[:, None] - kpos[None, :]) <= SWA_WINDOW) & (kpos[None, :] >= 0) & (kpos[None, :] < n)
        l_loc = jnp.where(mask, jnp.einsum('bkgqd,bksd->bkgqs', qi, ki) * scale, NEG_INF)
        l_ctx = jnp.einsum('bkgqd,bkpd->bkgqp', qi, ck) * scale
        p = jax.nn.softmax(jnp.concatenate([l_loc, l_ctx, sk], axis=-1), axis=-1)
        return (jnp.einsum('bkgqs,bksd->bkgqd', p[..., :span], vi)
                + jnp.einsum('bkgqp,bkpd->bkgqd', p[..., span:-1], cv))

    out = lax.map(block, jnp.arange(nb))
    return out.transpose(1, 2, 3, 0, 4, 5).reshape(b, h, n, d)


def _na_latent(q, k, v, ck, cv, rpb):
    b, h, n, d = q.shape
    rows = n // GRID_W
    kh = min(NA_KH, rows)
    scale = d ** -0.5
    qg = q.reshape(b, h, rows, GRID_W, d)
    kg = k.reshape(b, h, rows, GRID_W, d)
    vg = v.reshape(b, h, rows, GRID_W, d)
    col = jnp.arange(GRID_W)
    cs = jnp.clip(col - NA_KW // 2, 0, GRID_W - NA_KW)
    col_mask = (col[None, :] >= cs[:, None]) & (col[None, :] < cs[:, None] + NA_KW)
    dc = jnp.clip(col[None, :] - col[:, None], -(NA_KW - 1), NA_KW - 1) + NA_KW - 1
    rpb_c = rpb[:, :, dc]
    nloc = kh * GRID_W

    def row(r):
        rs = jnp.clip(r - kh // 2, 0, rows - kh)
        qr = lax.dynamic_index_in_dim(qg, r, axis=2, keepdims=False)
        kr = lax.dynamic_slice_in_dim(kg, rs, kh, axis=2)
        vr = lax.dynamic_slice_in_dim(vg, rs, kh, axis=2)
        dr = rs + jnp.arange(kh) - r + NA_KH - 1
        bias = jnp.take(rpb_c, dr, axis=1).transpose(0, 2, 1, 3)
        l_loc = jnp.einsum('bhqd,bhrkd->bhqrk', qr, kr) * scale + bias[None]
        l_loc = jnp.where(col_mask[:, None, :], l_loc, NEG_INF).reshape(b, h, GRID_W, nloc)
        l_ctx = jnp.einsum('bhqd,bhpd->bhqp', qr, ck) * scale
        p = jax.nn.softmax(jnp.concatenate([l_loc, l_ctx], axis=-1), axis=-1)
        p_loc = p[..., :nloc].reshape(b, h, GRID_W, kh, GRID_W)
        return (jnp.einsum('bhqrk,bhrkd->bhqd', p_loc, vr)
                + jnp.einsum('bhqp,bhpd->bhqd', p[..., nloc:], cv))

    out = lax.map(row, jnp.arange(rows))
    return out.transpose(1, 2, 0, 3, 4).reshape(b, h, n, d)


def _decay(a, alog, dtb):
    g = -jnp.exp(alog) * jax.nn.softplus(a + dtb)
    return g.transpose(0, 2, 1)


def _token_mixers(hp, p, cache):
    b, n, _ = hp.shape
    latent = cache is not None
    cuts = [int(s) for s in np.cumsum(IN_SPLITS)[:-1]]
    (a_q, a_k, a_v, a_r, a_lf, a_lb, s_q, s_k, s_v, n_q, n_k, n_v,
     d_q, d_k, d_v, d_r, d_af, d_ab, d_bf, d_bb) = jnp.split(hp, cuts, axis=-1)

    q, k, v = _heads(a_q, GLA_HEADS), _heads(a_k, GLA_HEADS), _heads(a_v, GLA_HEADS)
    la_f = _heads(jax.nn.log_sigmoid(a_lf @ p['gla_wup'][0] + p['gla_bup'][0]) / GLA_TAU, GLA_HEADS)
    la_b = _heads(jax.nn.log_sigmoid(a_lb @ p['gla_wup'][1] + p['gla_bup'][1]) / GLA_TAU, GLA_HEADS)
    s0 = cache['gla'] if latent else jnp.zeros((b, 2, GLA_HEADS, HEAD_DIM, HEAD_DIM), hp.dtype)
    o_f, sa_f = _gla_chunked(q, k, v, la_f, s0[:, 0])
    o_b, sa_b = _gla_chunked(_flip(q), _flip(k), _flip(v), _flip(la_b), s0[:, 1])
    out_a = _rmsnorm(o_f + _flip(o_b), p['gla_onorm']) * jax.nn.silu(_heads(a_r, GLA_HEADS))

    q = _rmsnorm(_heads(s_q, SWA_HEADS), p['swa_qnorm'])
    swa_k = _rmsnorm(_heads(s_k, SWA_KV_HEADS), p['swa_knorm'])
    swa_v = _heads(s_v, SWA_KV_HEADS)
    if latent:
        out_b = _swa_latent(_rope_2d(q), _rope_2d(swa_k), swa_v, cache['swa_k'], cache['swa_v'], p['swa_sink'])
    else:
        out_b = _ctx_attention(q, swa_k, swa_v, p['swa_sink'])

    q = _rmsnorm(_heads(n_q, NA_HEADS), p['na_qnorm'])
    na_k = _rmsnorm(_heads(n_k, NA_HEADS), p['na_knorm'])
    na_v = _heads(n_v, NA_HEADS)
    if latent:
        out_c = _na_latent(q, na_k, na_v, cache['na_k'], cache['na_v'], p['na_rpb'])
    else:
        out_c = _ctx_attention(q, na_k, na_v, None)

    qkv = jax.nn.silu(_dwconv(jnp.concatenate([d_q, d_k, d_v], axis=-1), p['gdn_conv']))
    q, k, v = jnp.split(qkv, 3, axis=-1)
    q = _l2norm(_heads(q, GDN_HEADS)) * (HEAD_DIM ** -0.5)
    k = _l2norm(_heads(k, GDN_HEADS))
    v = _heads(v, GDN_HEADS)
    g_f = _decay(d_af, p['gdn_alog'][0], p['gdn_dtbias'][0])
    g_b = _decay(d_ab, p['gdn_alog'][1], p['gdn_dtbias'][1])
    beta_f = jax.nn.sigmoid(d_bf).transpose(0, 2, 1)
    beta_b = jax.nn.sigmoid(d_bb).transpose(0, 2, 1)
    s0 = cache['gdn'] if latent else jnp.zeros((b, 2, GDN_HEADS, HEAD_DIM, HEAD_DIM), hp.dtype)
    o_f, sd_f = _gdn_chunked(q, k, v, g_f, beta_f, s0[:, 0])
    o_b, sd_b = _gdn_chunked(_flip(q), _flip(k), _flip(v), _flip(g_b), _flip(beta_b), s0[:, 1])
    out_d = _rmsnorm(o_f + _flip(o_b), p['gdn_onorm']) * jax.nn.silu(_heads(d_r, GDN_HEADS))

    y = jnp.concatenate([_merge(out_a), _merge(out_b), _merge(out_c), _merge(out_d)], axis=-1)
    if latent:
        return y, None
    return y, (swa_k, swa_v, na_k, na_v, jnp.stack([sa_f, sa_b], axis=1), jnp.stack([sd_f, sd_b], axis=1))


def kernel(x_prompt, x_sample, cache_swa_k, cache_swa_v, cache_na_k, cache_na_v, state_gla, state_gdn, c, c_ctx, w_mod, b_mod, norm1_g, norm2_g, w_in, w_out, gla_wup, gla_bup, gla_onorm, swa_qnorm, swa_knorm, swa_sink, na_qnorm, na_knorm, na_rpb, gdn_conv, gdn_alog, gdn_dtbias, gdn_onorm, ffn_w1, ffn_w3, ffn_w2, moe_router, moe_w1, moe_w3, moe_w2):
    bf16 = jnp.bfloat16
    x = jnp.concatenate([x_prompt.reshape(N_PROMPT_TOK, D_MODEL), x_sample.reshape(N_SAMPLE_TOK, D_MODEL)], axis=0)
    cvec = jnp.concatenate([c_ctx[None, :], c, jnp.zeros((N_MOD_ROWS - 1 - DEC_BATCH, D_MODEL), jnp.float32)], axis=0)
    mod_all = _modulation_all(cvec, w_mod, b_mod).reshape(DEPTH, N_MOD_ROWS, 1, MOD_W)
    w_in_b = jnp.pad(w_in, ((0, 0), (0, 0), (0, IN_COLS_PAD - IN_COLS))).astype(bf16)
    w_out_b = w_out.astype(bf16)

    ctx_states = []
    for l in range(DEPTH):
        p = {'gla_wup': gla_wup[l], 'gla_bup': gla_bup[l], 'gla_onorm': gla_onorm[l],
             'swa_qnorm': swa_qnorm[l], 'swa_knorm': swa_knorm[l], 'swa_sink': swa_sink[l],
             'na_qnorm': na_qnorm[l], 'na_knorm': na_knorm[l], 'na_rpb': na_rpb[l],
             'gdn_conv': gdn_conv[l], 'gdn_alog': gdn_alog[l], 'gdn_dtbias': gdn_dtbias[l],
             'gdn_onorm': gdn_onorm[l]}
        cache = {'swa_k': cache_swa_k[:, l], 'swa_v': cache_swa_v[:, l],
                 'na_k': cache_na_k[:, l], 'na_v': cache_na_v[:, l],
                 'gla': state_gla[:, l], 'gdn': state_gdn[:, l]}
        mod = mod_all[l]
        hp = _in_proj(x, mod, norm1_g[l][None, :], w_in_b[l])[:, :IN_COLS]
        mix_p, st = _token_mixers(hp[:N_PROMPT_TOK].reshape(BATCH, SEQ, IN_COLS), p, None)
        mix_s, _ = _token_mixers(hp[N_PROMPT_TOK:].reshape(DEC_BATCH, DEC_SEQ, IN_COLS), p, cache)
        mix = jnp.concatenate([mix_p.reshape(N_PROMPT_TOK, D_MODEL), mix_s.reshape(N_SAMPLE_TOK, D_MODEL)], axis=0)
        x = _out_proj(mix, x, mod, w_out_b[l])
        if l % 2 == 0:
            x = _ffn(x, mod, norm2_g[l][None, :], ffn_w1[l // 2].astype(bf16), ffn_w3[l // 2].astype(bf16),
                     ffn_w2[l // 2].astype(bf16))
        else:
            x = _moe(x, mod, norm2_g[l][None, :], moe_router[l // 2], moe_w1[l // 2].astype(bf16),
                     moe_w3[l // 2].astype(bf16), moe_w2[l // 2].astype(bf16))
        ctx_states.append(st)

    outs = [jnp.stack([st[j] for st in ctx_states], axis=1) for j in range(6)]
    y_prompt = x[:N_PROMPT_TOK].reshape(BATCH, SEQ, D_MODEL)
    y_sample = x[N_PROMPT_TOK:].reshape(DEC_BATCH, DEC_SEQ, D_MODEL)
    return (y_prompt, y_sample, *outs)
```

```python
import functools
import math

import numpy as np
import jax
import jax.numpy as jnp
from jax import lax
from jax.experimental import pallas as pl
from jax.experimental.pallas import tpu as pltpu

D_MODEL = 1024
BATCH = 32
SEQ = 256
DEPTH = 4
DEC_BATCH = 8
DEC_SEQ = 1024
PAST_LEN = 512

GRID_W = 64
HEAD_DIM = 64
GROUP_WIDTH = D_MODEL // 4
GROUP_HEADS = GROUP_WIDTH // HEAD_DIM
GLA_HEADS = GROUP_HEADS
GLA_LOWRANK = 16
GLA_TAU = 16.0
GLA_CHUNK = 64
SWA_HEADS = GROUP_HEADS
SWA_KV_HEADS = 2
SWA_WINDOW = 128
SWA_BLOCK = 128
NA_HEADS = GROUP_HEADS
NA_KH = 8
NA_KW = 16
GDN_HEADS = GROUP_HEADS
GDN_CONV = 3
GDN_CHUNK = 64
D_FF = 2816
N_EXPERTS = 8
TOP_K = 2
D_FF_EXPERT = 1024
ROPE_BASE = 10000.0
CTX_BLOCK = 128
EPS = 1e-6
NEG_INF = -1e30
SWA_KV_WIDTH = SWA_KV_HEADS * HEAD_DIM
IN_SPLITS = ([GROUP_WIDTH] * 4 + [GLA_LOWRANK] * 2 + [GROUP_WIDTH, SWA_KV_WIDTH, SWA_KV_WIDTH]
             + [GROUP_WIDTH] * 3 + [GROUP_WIDTH] * 4 + [GDN_HEADS] * 4)
IN_COLS = sum(IN_SPLITS)

LANES = 128
VMEM_LIMIT_BYTES = 56 * 1024 * 1024

N_PROMPT_TOK = BATCH * SEQ
N_SAMPLE_TOK = DEC_BATCH * DEC_SEQ
N_TOK = N_PROMPT_TOK + N_SAMPLE_TOK
N_MOD_ROWS = 16
MOD_W = 6 * D_MODEL
TM = 512
IN_COLS_PAD = -(-IN_COLS // LANES) * LANES

COL_A = 0
COL_D = 4 * GROUP_WIDTH
COL_C = 8 * GROUP_WIDTH
COL_B = 11 * GROUP_WIDTH
COL_SMALL = COL_B + GROUP_WIDTH + 2 * SWA_KV_WIDTH
SMALL_COL_BLOCK = COL_SMALL // LANES
SMALL_GDN = 2 * GLA_LOWRANK


def _in_col_perm():
    off = np.concatenate([[0], np.cumsum(IN_SPLITS)])
    seg = lambda a, b: np.arange(off[a], off[b])
    return np.concatenate([seg(0, 4), seg(12, 16), seg(9, 12), seg(6, 9), seg(4, 6), seg(16, 20)])


_IN_COL_PERM = _in_col_perm()


def _small_expand_matrix():
    e = np.zeros((LANES, 4 * GROUP_WIDTH), np.float32)
    for s in range(4):
        for h in range(GROUP_HEADS):
            e[SMALL_GDN + s * GROUP_HEADS + h, s * GROUP_WIDTH + h * HEAD_DIM:s * GROUP_WIDTH + (h + 1) * HEAD_DIM] = 1.0
    return e


def _mod_row(i):
    n_prompt_tiles = N_PROMPT_TOK // TM
    return jnp.where(i < n_prompt_tiles, 0, 1 + (i - n_prompt_tiles) // (DEC_SEQ // TM))


def _cparams(*sem):
    return pltpu.CompilerParams(dimension_semantics=sem, vmem_limit_bytes=VMEM_LIMIT_BYTES)


def _mod_kernel(c_ref, w_ref, b_ref, o_ref):
    c = c_ref[...]
    s = c * jax.nn.sigmoid(c)
    o_ref[...] = jnp.dot(s, w_ref[...], preferred_element_type=jnp.float32,
                         precision=lax.Precision.HIGHEST) + b_ref[...]


def _modulation_all(cvec, w_mod, b_mod):
    tn = 1536
    return pl.pallas_call(
        _mod_kernel,
        grid=(DEPTH, MOD_W // tn),
        in_specs=[pl.BlockSpec((N_MOD_ROWS, D_MODEL), lambda l, j: (0, 0)),
                  pl.BlockSpec((None, D_MODEL, tn), lambda l, j: (l, 0, j)),
                  pl.BlockSpec((None, 1, tn), lambda l, j: (l, 0, j))],
        out_specs=pl.BlockSpec((None, N_MOD_ROWS, tn), lambda l, j: (l, 0, j)),
        out_shape=jax.ShapeDtypeStruct((DEPTH, N_MOD_ROWS, MOD_W), jnp.float32),
        compiler_params=_cparams("arbitrary", "arbitrary"),
        name="modulation",
    )(cvec, w_mod, b_mod.reshape(DEPTH, 1, MOD_W))


def _modulated_norm(x, g, shift, scale):
    y = x * lax.rsqrt(jnp.mean(x * x, axis=-1, keepdims=True) + EPS) * g
    return y * (1.0 + scale) + shift


def _in_proj_kernel(x_ref, mod_ref, g_ref, w_ref, o_ref):
    h = _modulated_norm(x_ref[...], g_ref[...], mod_ref[:, 0:D_MODEL], mod_ref[:, D_MODEL:2 * D_MODEL])
    o_ref[...] = jnp.dot(h.astype(jnp.bfloat16), w_ref[...], preferred_element_type=jnp.float32)


def _in_proj(x, mod, g, w):
    return pl.pallas_call(
        _in_proj_kernel,
        grid=(N_TOK // TM,),
        in_specs=[pl.BlockSpec((TM, D_MODEL), lambda i: (i, 0)),
                  pl.BlockSpec((None, 1, MOD_W), lambda i: (_mod_row(i), 0, 0)),
                  pl.BlockSpec((1, D_MODEL), lambda i: (0, 0)),
                  pl.BlockSpec((D_MODEL, IN_COLS_PAD), lambda i: (0, 0), pipeline_mode=pl.Buffered(1))],
        out_specs=pl.BlockSpec((TM, IN_COLS_PAD), lambda i: (i, 0)),
        out_shape=jax.ShapeDtypeStruct((N_TOK, IN_COLS_PAD), jnp.float32),
        compiler_params=_cparams("arbitrary"),
        name="in_proj",
    )(x, mod, g, w)


def _out_proj_kernel(mix_ref, x_ref, mod_ref, w_ref, o_ref):
    y = jnp.dot(mix_ref[...].astype(jnp.bfloat16), w_ref[...], preferred_element_type=jnp.float32)
    o_ref[...] = x_ref[...] + mod_ref[:, 2 * D_MODEL:3 * D_MODEL] * y


def _out_proj(mix, x, mod, w):
    return pl.pallas_call(
        _out_proj_kernel,
        grid=(N_TOK // TM,),
        in_specs=[pl.BlockSpec((TM, D_MODEL), lambda i: (i, 0)),
                  pl.BlockSpec((TM, D_MODEL), lambda i: (i, 0)),
                  pl.BlockSpec((None, 1, MOD_W), lambda i: (_mod_row(i), 0, 0)),
                  pl.BlockSpec((D_MODEL, D_MODEL), lambda i: (0, 0), pipeline_mode=pl.Buffered(1))],
        out_specs=pl.BlockSpec((TM, D_MODEL), lambda i: (i, 0)),
        out_shape=jax.ShapeDtypeStruct((N_TOK, D_MODEL), jnp.float32),
        compiler_params=_cparams("arbitrary"),
        name="out_proj",
    )(mix, x, mod, w)


FF_CHUNK = D_FF // 2


def _ffn_kernel(x_ref, mod_ref, g_ref, w1_ref, w3_ref, w2_ref, o_ref):
    x = x_ref[...]
    h = _modulated_norm(x, g_ref[...], mod_ref[:, 3 * D_MODEL:4 * D_MODEL], mod_ref[:, 4 * D_MODEL:5 * D_MODEL])
    hb = h.astype(jnp.bfloat16)
    y = jnp.zeros((TM, D_MODEL), jnp.float32)
    for c0 in range(0, D_FF, FF_CHUNK):
        a = jnp.dot(hb, w1_ref[:, c0:c0 + FF_CHUNK], preferred_element_type=jnp.float32)
        b = jnp.dot(hb, w3_ref[:, c0:c0 + FF_CHUNK], preferred_element_type=jnp.float32)
        s = (a * jax.nn.sigmoid(a) * b).astype(jnp.bfloat16)
        y = y + jnp.dot(s, w2_ref[c0:c0 + FF_CHUNK, :], preferred_element_type=jnp.float32)
    o_ref[...] = x + mod_ref[:, 5 * D_MODEL:6 * D_MODEL] * y


def _ffn(x, mod, g, w1, w3, w2):
    resident = dict(pipeline_mode=pl.Buffered(1))
    return pl.pallas_call(
        _ffn_kernel,
        grid=(N_TOK // TM,),
        in_specs=[pl.BlockSpec((TM, D_MODEL), lambda i: (i, 0)),
                  pl.BlockSpec((None, 1, MOD_W), lambda i: (_mod_row(i), 0, 0)),
                  pl.BlockSpec((1, D_MODEL), lambda i: (0, 0)),
                  pl.BlockSpec((D_MODEL, D_FF), lambda i: (0, 0), **resident),
                  pl.BlockSpec((D_MODEL, D_FF), lambda i: (0, 0), **resident),
                  pl.BlockSpec((D_FF, D_MODEL), lambda i: (0, 0), **resident)],
        out_specs=pl.BlockSpec((TM, D_MODEL), lambda i: (i, 0)),
        out_shape=jax.ShapeDtypeStruct((N_TOK, D_MODEL), jnp.float32),
        compiler_params=_cparams("arbitrary"),
        name="ffn_dense",
    )(x, mod, g, w1, w3, w2)


def _moe_kernel(x_ref, mod_ref, g_ref, r_ref, w1_ref, w3_ref, w2_ref, o_ref, hb_ref, gate_ref):
    e = pl.program_id(1)

    @pl.when(e == 0)
    def _():
        h = _modulated_norm(x_ref[...], g_ref[...], mod_ref[:, 3 * D_MODEL:4 * D_MODEL],
                            mod_ref[:, 4 * D_MODEL:5 * D_MODEL])
        hb_ref[...] = h.astype(jnp.bfloat16)
        logits = jnp.dot(h, r_ref[...], preferred_element_type=jnp.float32,
                         precision=lax.Precision.HIGHEST)
        lane = lax.broadcasted_iota(jnp.int32, logits.shape, 1)
        v1 = jnp.max(logits, axis=-1, keepdims=True)
        i1 = jnp.min(jnp.where(logits == v1, lane, N_EXPERTS), axis=-1, keepdims=True)
        rest = jnp.where(lane == i1, -jnp.inf, logits)
        v2 = jnp.max(rest, axis=-1, keepdims=True)
        i2 = jnp.min(jnp.where(rest == v2, lane, N_EXPERTS), axis=-1, keepdims=True)
        p2 = jnp.exp(v2 - v1)
        den = 1.0 + p2
        gate_ref[...] = jnp.where(lane == i1, 1.0 / den, 0.0) + jnp.where(lane == i2, p2 / den, 0.0)
        o_ref[...] = jnp.zeros_like(o_ref)

    hb = hb_ref[...]
    a = jnp.dot(hb, w1_ref[...], preferred_element_type=jnp.float32)
    b = jnp.dot(hb, w3_ref[...], preferred_element_type=jnp.float32)
    s = (a * jax.nn.sigmoid(a) * b).astype(jnp.bfloat16)
    y = jnp.dot(s, w2_ref[...], preferred_element_type=jnp.float32)
    gates = gate_ref[...]
    lane = lax.broadcasted_iota(jnp.int32, gates.shape, 1)
    ge = jnp.sum(jnp.where(lane == e, gates, 0.0), axis=-1, keepdims=True)
    o_ref[...] += ge * y

    @pl.when(e == N_EXPERTS - 1)
    def _():
        o_ref[...] = x_ref[...] + mod_ref[:, 5 * D_MODEL:6 * D_MODEL] * o_ref[...]


def _moe(x, mod, g, router, w1, w3, w2):
    return pl.pallas_call(
        _moe_kernel,
        grid=(N_TOK // TM, N_EXPERTS),
        in_specs=[pl.BlockSpec((TM, D_MODEL), lambda i, e: (i, 0)),
                  pl.BlockSpec((None, 1, MOD_W), lambda i, e: (_mod_row(i), 0, 0)),
                  pl.BlockSpec((1, D_MODEL), lambda i, e: (0, 0)),
                  pl.BlockSpec((D_MODEL, N_EXPERTS), lambda i, e: (0, 0)),
                  pl.BlockSpec((None, D_MODEL, D_FF_EXPERT), lambda i, e: (e, 0, 0)),
                  pl.BlockSpec((None, D_MODEL, D_FF_EXPERT), lambda i, e: (e, 0, 0)),
                  pl.BlockSpec((None, D_FF_EXPERT, D_MODEL), lambda i, e: (e, 0, 0))],
        out_specs=pl.BlockSpec((TM, D_MODEL), lambda i, e: (i, 0)),
        out_shape=jax.ShapeDtypeStruct((N_TOK, D_MODEL), jnp.float32),
        scratch_shapes=[pltpu.VMEM((TM, D_MODEL), jnp.bfloat16),
                        pltpu.VMEM((TM, N_EXPERTS), jnp.float32)],
        compiler_params=_cparams("arbitrary", "arbitrary"),
        name="ffn_moe",
    )(x, mod, g, router, w1, w3, w2)


CH = GLA_CHUNK
SLAB = GROUP_WIDTH
N_SLAB_HEADS = SLAB // HEAD_DIM
_HI = lax.Precision.HIGHEST


def _linear_consts():
    i = np.arange(CH)[:, None]
    j = (np.arange(SLAB) % CH)[None, :]
    slab = np.stack([i >= j, i > j, i <= j, i < j, i == j]).astype(np.float32)
    r = np.arange(SLAB) // CH
    bd = (r[:, None] == r[None, :]).astype(np.float32)
    t = np.arange(CH)
    tri = np.stack([t[:, None] >= t[None, :], t[:, None] <= t[None, :]]).astype(np.float32)
    return jnp.asarray(slab), jnp.asarray(bd), jnp.asarray(tri, dtype=jnp.bfloat16)


def _bf(x):
    return x.astype(jnp.bfloat16)


def _split2(x):
    hi = _bf(x)
    return hi, _bf(x - hi.astype(jnp.float32))


def _split3_lanes(x):
    hi = _bf(x)
    r1 = x - hi.astype(jnp.float32)
    mid = _bf(r1)
    lo = _bf(r1 - mid.astype(jnp.float32))
    return jnp.concatenate([hi, mid, lo], axis=1)


def _mm(a, b):
    return jnp.dot(a, b, preferred_element_type=jnp.float32)


def _mm_nt(a, b):
    return lax.dot_general(a, b, (((1,), (1,)), ((), ())), preferred_element_type=jnp.float32)


def _mm_tn(a, b):
    return lax.dot_general(a, b, (((0,), (0,)), ((), ())), preferred_element_type=jnp.float32)


def _blockdiag(y_b, bd_b):
    return jnp.concatenate([y_b] * N_SLAB_HEADS, axis=0) * bd_b


def _exact_rows_mm(lhs_b, x):
    c = _mm(lhs_b, _split3_lanes(x))
    return c[:, 0:SLAB] + c[:, SLAB:2 * SLAB] + c[:, 2 * SLAB:3 * SLAB]


def _head_mm3(lhs, y, bd_b):
    m = lhs.shape[0]
    lh, ll = _split2(lhs)
    yh, yl = _split2(y)
    a = _mm(jnp.concatenate([lh, ll], axis=0), _blockdiag(yh, bd_b))
    return a[0:m] + a[m:2 * m] + _mm(lh, _blockdiag(yl, bd_b))


def _unit_triangular_inverse(nmat, eye, bd_b):
    t = eye + nmat
    p = _head_mm3(nmat, nmat, bd_b)
    n_doublings = int(math.log2(CH)) - 1
    for it in range(n_doublings):
        if it < n_doublings - 1:
            res = _head_mm3(jnp.concatenate([t, p], axis=0), p, bd_b)
            t = t + res[0:CH]
            p = res[CH:2 * CH]
        else:
            t = t + _head_mm3(t, p, bd_b)
    return t


def _group_sum(x, bd_f):
    return jnp.dot(x, bd_f, preferred_element_type=jnp.float32, precision=_HI)


def _softplus(z):
    return jnp.maximum(z, 0.0) + jnp.log1p(jnp.exp(-jnp.abs(z)))


def _silu(z):
    return z * jax.nn.sigmoid(z)


def _gdn_chunk(qn, kn, v, beta, g, st_ref, d, slab_ref, bd_b, tri_ref):
    incl = slab_ref[2 * d]
    strict = slab_ref[2 * d + 1]
    eye = slab_ref[4]
    c = _exact_rows_mm(tri_ref[d], g)
    r_row = _exact_rows_mm(jnp.ones((8, CH), jnp.bfloat16), c * eye)[0:1]
    dec = jnp.exp(jnp.where(incl > 0.5, c - r_row, 0.0)) * incl
    kn_b = _bf(kn)
    kbd = _blockdiag(kn_b, bd_b)
    kk = _mm_nt(kn_b, kbd)
    qk = _mm_nt(_bf(qn), kbd) * dec
    t_inv = _unit_triangular_inverse(-(strict * beta * dec * kk), eye, bd_b)
    g_last = c[0:1] if d == 1 else c[CH - 1:CH]
    e_c = jnp.exp(c)
    t_b = _bf(t_inv)
    w_v = _mm(t_b, _blockdiag(_bf(beta * v), bd_b))
    w_k = _mm(t_b, _blockdiag(_bf(beta * e_c * kn), bd_b))
    q_dec = qn * e_c
    k_end = kn * jnp.exp(g_last - c)
    s = st_ref[...]
    s_b = _bf(s)
    u = w_v - _mm(_bf(w_k), s_b)
    u_b = _bf(u)
    o = _mm(_bf(q_dec), s_b) + _mm(_bf(qk), _blockdiag(u_b, bd_b))
    st_ref[...] = s * jnp.exp(g_last) + bd_b.astype(jnp.float32) * _mm_tn(_bf(k_end), u_b)
    return o


def _gdn_kernel(n, has_s0, *refs):
    refs = list(refs)
    x_ref, small_ref = refs[0:2]
    k = 2
    if has_s0:
        s0_ref = refs[k]
        k += 1
    conv_ref, dec_ref, gain_ref, e_ref, slab_ref, bd_ref, tri_ref = refs[k:k + 7]
    k += 7
    if has_s0:
        k += 1
    o_ref, sfin_ref, qn_s, kn_s, v_s, g_s, b_s, of_s, ob_s, st_s = refs[k:]
    nc = n // CH
    bd_f = bd_ref[...]
    bd_b = _bf(bd_f)

    x = x_ref[:, 0:3 * SLAB]
    row = lax.broadcasted_iota(jnp.int32, (n, 1), 0)
    prev = jnp.where(row == 0, 0.0, pltpu.roll(x, 1, axis=0))
    nxt = jnp.where(row == n - 1, 0.0, pltpu.roll(x, n - 1, axis=0))
    y = _silu(prev * conv_ref[0:1, :] + x * conv_ref[1:2, :] + nxt * conv_ref[2:3, :])
    q = y[:, 0:SLAB]
    kk = y[:, SLAB:2 * SLAB]
    qn_s[...] = q * lax.rsqrt(_group_sum(q * q, bd_f) + EPS) * (HEAD_DIM ** -0.5)
    kn_s[...] = kk * lax.rsqrt(_group_sum(kk * kk, bd_f) + EPS)
    v_s[...] = y[:, 2 * SLAB:3 * SLAB]
    bc = jnp.dot(small_ref[...], e_ref[...], preferred_element_type=jnp.float32, precision=_HI)
    for d in range(2):
        z = bc[:, d * SLAB:(d + 1) * SLAB] + dec_ref[2 + d:3 + d, :]
        g_s[d] = -jnp.exp(dec_ref[d:d + 1, :]) * _softplus(z)
        b_s[d] = jax.nn.sigmoid(bc[:, (2 + d) * SLAB:(3 + d) * SLAB])
    if has_s0:
        st_s[...] = s0_ref[...]
    else:
        st_s[...] = jnp.zeros_like(st_s)

    def body(c, carry):
        for d in range(2):
            cc = c if d == 0 else nc - 1 - c
            rows = pl.ds(pl.multiple_of(cc * CH, CH), CH)
            o = _gdn_chunk(qn_s[rows, :], kn_s[rows, :], v_s[rows, :], b_s[d, rows, :], g_s[d, rows, :],
                           st_s.at[d], d, slab_ref, bd_b, tri_ref)
            if d == 0:
                of_s[rows, :] = o
            else:
                ob_s[rows, :] = o
        return carry

    lax.fori_loop(0, nc, body, 0)
    sfin_ref[...] = st_s[...]
    o = of_s[...] + ob_s[...]
    o = o * lax.rsqrt(_group_sum(o * o, bd_f) * (1.0 / HEAD_DIM) + EPS) * gain_ref[...]
    o_ref[...] = o * _silu(x_ref[:, 3 * SLAB:4 * SLAB])


def _gla_chunk(q, k, v, la, st_ref, d, slab_ref, bd_b, tri_ref):
    incl = slab_ref[2 * d]
    cum = _exact_rows_mm(tri_ref[d], la)
    last = cum[0:1] if d == 1 else cum[CH - 1:CH]
    q_dec = _bf(q * (HEAD_DIM ** -0.5) * jnp.exp(cum))
    k_inv = _bf(k * jnp.exp(-cum))
    k_end = _bf(k * jnp.exp(last - cum))
    v_b = _bf(v)
    att = _mm_nt(q_dec, _blockdiag(k_inv, bd_b)) * incl
    s_t = st_ref[...]
    o = _mm(_bf(att), _blockdiag(v_b, bd_b)) + _mm_nt(q_dec, _bf(s_t))
    st_ref[...] = s_t * jnp.exp(last) + bd_b.astype(jnp.float32) * _mm_tn(v_b, k_end)
    return o


def _gla_kernel(n, has_s0, *refs):
    refs = list(refs)
    x_ref, small_ref = refs[0:2]
    k = 2
    if has_s0:
        s0_ref = refs[k]
        k += 1
    wup_ref, bup_ref, gain_ref, slab_ref, bd_ref, tri_ref = refs[k:k + 6]
    k += 6
    if has_s0:
        k += 1
    o_ref, sfin_ref, la_s, of_s, ob_s, st_s = refs[k:]
    nc = n // CH
    bd_f = bd_ref[...]
    bd_b = _bf(bd_f)
    small = small_ref[...]
    for d in range(2):
        z = jnp.dot(small, wup_ref[d], preferred_element_type=jnp.float32, precision=_HI) + bup_ref[d:d + 1, :]
        la_s[d] = (jnp.minimum(z, 0.0) - jnp.log1p(jnp.exp(-jnp.abs(z)))) * (1.0 / GLA_TAU)
    if has_s0:
        st_s[...] = s0_ref[...]
    else:
        st_s[...] = jnp.zeros_like(st_s)

    def body(c, carry):
        for d in range(2):
            cc = c if d == 0 else nc - 1 - c
            rows = pl.ds(pl.multiple_of(cc * CH, CH), CH)
            o = _gla_chunk(x_ref[rows, 0:SLAB], x_ref[rows, SLAB:2 * SLAB], x_ref[rows, 2 * SLAB:3 * SLAB],
                           la_s[d, rows, :], st_s.at[d], d, slab_ref, bd_b, tri_ref)
            if d == 0:
                of_s[rows, :] = o
            else:
                ob_s[rows, :] = o
        return carry

    lax.fori_loop(0, nc, body, 0)
    sfin_ref[...] = st_s[...]
    o = of_s[...] + ob_s[...]
    o = o * lax.rsqrt(_group_sum(o * o, bd_f) * (1.0 / HEAD_DIM) + EPS) * gain_ref[...]
    o_ref[...] = o * _silu(x_ref[:, 3 * SLAB:4 * SLAB])


def _linear_mixer(kind, hp, col_block, s0_bd, params, consts, prev_out):
    latent = prev_out is not None
    n = DEC_SEQ if latent else SEQ
    n_seq = DEC_BATCH if latent else BATCH
    row0 = N_PROMPT_TOK // n if latent else 0
    full = lambda a: pl.BlockSpec(a.shape, lambda b: (0,) * a.ndim)
    in_specs = [pl.BlockSpec((n, 4 * SLAB), lambda b: (row0 + b, col_block)),
                pl.BlockSpec((n, LANES), lambda b: (row0 + b, SMALL_COL_BLOCK))]
    args = [hp, hp]
    if latent:
        in_specs.append(pl.BlockSpec((None, 2, SLAB, SLAB), lambda b: (b, 0, 0, 0)))
        args.append(s0_bd)
    for a in tuple(params) + tuple(consts):
        in_specs.append(full(a))
        args.append(a)
    aliases = {}
    if latent:
        aliases = {len(args): 0}
        in_specs.append(pl.BlockSpec(memory_space=pl.ANY))
        args.append(prev_out)
    seq_buf = pltpu.VMEM((n, SLAB), jnp.float32)
    dir_buf = pltpu.VMEM((2, n, SLAB), jnp.float32)
    state_buf = pltpu.VMEM((2, SLAB, SLAB), jnp.float32)
    if kind == 'gdn':
        body = functools.partial(_gdn_kernel, n, latent)
        scratch = [seq_buf, seq_buf, seq_buf, dir_buf, dir_buf, seq_buf, seq_buf, state_buf]
    else:
        body = functools.partial(_gla_kernel, n, latent)
        scratch = [dir_buf, seq_buf, seq_buf, state_buf]
    return pl.pallas_call(
        body,
        grid=(n_seq,),
        in_specs=in_specs,
        out_specs=[pl.BlockSpec((n, SLAB), lambda b: (row0 + b, 0)),
                   pl.BlockSpec((None, 2, SLAB, SLAB), lambda b: (b, 0, 0, 0))],
        out_shape=[jax.ShapeDtypeStruct((N_TOK, SLAB), jnp.float32),
                   jax.ShapeDtypeStruct((n_seq, 2, SLAB, SLAB), jnp.float32)],
        scratch_shapes=scratch,
        input_output_aliases=aliases,
        compiler_params=_cparams("arbitrary"),
        name=kind + ("_latent" if latent else "_ctx"),
    )(*args)


def _to_blockdiag(s, transpose):
    if transpose:
        s = jnp.swapaxes(s, -1, -2)
    eye = jnp.eye(N_SLAB_HEADS, dtype=s.dtype)
    return jnp.einsum('bdhij,hg->bdhigj', s, eye).reshape(s.shape[0], 2, SLAB, SLAB)


def _from_blockdiag(sbd, transpose):
    t = sbd.reshape(sbd.shape[0], 2, N_SLAB_HEADS, HEAD_DIM, N_SLAB_HEADS, HEAD_DIM)
    s = jnp.stack([t[:, :, h, :, h, :] for h in range(N_SLAB_HEADS)], axis=2)
    return jnp.swapaxes(s, -1, -2) if transpose else s


def _rmsnorm(x, g):
    y = x * lax.rsqrt(jnp.mean(x * x, axis=-1, keepdims=True) + EPS)
    return y * g


def _l2norm(x):
    return x * lax.rsqrt(jnp.sum(x * x, axis=-1, keepdims=True) + EPS)


def _heads(t, n):
    b, s, _ = t.shape
    return t.reshape(b, s, n, -1).transpose(0, 2, 1, 3)


def _merge(t):
    b, h, s, d = t.shape
    return t.transpose(0, 2, 1, 3).reshape(b, s, h * d)


def _flip(t):
    return jnp.flip(t, axis=2)


def _rope_2d(x):
    n = x.shape[2]
    t = jnp.arange(n)
    rows = (t // GRID_W).astype(jnp.float32)
    cols = (t % GRID_W).astype(jnp.float32)
    half = HEAD_DIM // 2
    nf = half // 2
    inv = 1.0 / (ROPE_BASE ** (jnp.arange(nf, dtype=jnp.float32) / nf))

    def rot(xp, pos):
        ang = pos[:, None] * inv[None, :]
        cos, sin = jnp.cos(ang), jnp.sin(ang)
        x1, x2 = xp[..., :nf], xp[..., nf:]
        return jnp.concatenate([x1 * cos - x2 * sin, x1 * sin + x2 * cos], axis=-1)

    return jnp.concatenate([rot(x[..., :half], rows), rot(x[..., half:], cols)], axis=-1)


def _dwconv(x, w):
    k, c = w.shape
    return lax.conv_general_dilated(x, w[:, None, :], window_strides=(1,), padding=[(k // 2, k // 2)],
                                    dimension_numbers=('NWC', 'WIO', 'NWC'), feature_group_count=c)


def _gla_chunked(q, k, v, log_a, s0):
    b, h, n, dk = q.shape
    dv = v.shape[-1]
    L = GLA_CHUNK
    nc = n // L
    qf = q.reshape(b, h, nc, L, dk) * (dk ** -0.5)
    kf = k.reshape(b, h, nc, L, dk)
    vf = v.reshape(b, h, nc, L, dv)
    cum = jnp.cumsum(log_a.reshape(b, h, nc, L, dk), axis=3)
    last = cum[:, :, :, -1:, :]
    q_dec = qf * jnp.exp(cum)
    k_inv = kf * jnp.exp(-cum)
    k_end = kf * jnp.exp(last - cum)
    causal = jnp.tril(jnp.ones((L, L), bool))
    att = jnp.where(causal, jnp.einsum('bhcid,bhcjd->bhcij', q_dec, k_inv), 0.0)
    o_intra = jnp.einsum('bhcij,bhcjv->bhciv', att, vf)

    def step(s, xs):
        qd, ke, dl, vc, oi = xs
        o = oi + jnp.einsum('bhid,bhdv->bhiv', qd, s)
        s = dl[..., None] * s + jnp.einsum('bhjd,bhjv->bhdv', ke, vc)
        return s, o

    xs = tuple(jnp.moveaxis(t, 2, 0) for t in (q_dec, k_end, jnp.exp(last[:, :, :, 0, :]), vf, o_intra))
    s_fin, o = lax.scan(step, s0, xs)
    return jnp.moveaxis(o, 0, 2).reshape(b, h, n, dv), s_fin


def _gdn_chunked(q, k, v, g, beta, s0):
    b, h, n, dk = q.shape
    dv = v.shape[-1]
    L = GDN_CHUNK
    nc = n // L
    f32 = jnp.float32
    qf = q.reshape(b, h, nc, L, dk)
    kf = k.reshape(b, h, nc, L, dk)
    vf = v.reshape(b, h, nc, L, dv)
    bf = beta.reshape(b, h, nc, L)
    gc = jnp.cumsum(g.reshape(b, h, nc, L), axis=-1)
    incl = jnp.tril(jnp.ones((L, L), bool))
    strict = jnp.tril(jnp.ones((L, L), bool), -1)
    diff = gc[..., :, None] - gc[..., None, :]
    dec = jnp.where(incl, jnp.exp(jnp.where(incl, diff, 0.0)), 0.0)
    kk = jnp.einsum('bhcid,bhcjd->bhcij', kf, kf)
    a_mat = jnp.where(strict, bf[..., :, None] * dec * kk, 0.0)
    eye = jnp.eye(L, dtype=f32)
    t_inv = lax.linalg.triangular_solve(eye + a_mat, jnp.broadcast_to(eye, a_mat.shape),
                                        left_side=True, lower=True, unit_diagonal=True)
    w_v = jnp.einsum('bhcij,bhcjv->bhciv', t_inv, bf[..., None] * vf)
    w_k = jnp.einsum('bhcij,bhcjd->bhcid', t_inv, (bf * jnp.exp(gc))[..., None] * kf)
    qk = jnp.einsum('bhcid,bhcjd->bhcij', qf, kf) * dec
    q_dec = qf * jnp.exp(gc)[..., None]
    k_end = kf * jnp.exp(gc[..., -1:] - gc)[..., None]
    g_last = jnp.exp(gc[..., -1])

    def step(s, xs):
        qd, qkc, wv, wk, ke, gl = xs
        u = wv - jnp.einsum('bhid,bhdv->bhiv', wk, s)
        o = jnp.einsum('bhid,bhdv->bhiv', qd, s) + jnp.einsum('bhij,bhjv->bhiv', qkc, u)
        s = gl[..., None, None] * s + jnp.einsum('bhjd,bhjv->bhdv', ke, u)
        return s, o

    xs = tuple(jnp.moveaxis(t, 2, 0) for t in (q_dec, qk, w_v, w_k, k_end, g_last))
    s_fin, o = lax.scan(step, s0, xs)
    return jnp.moveaxis(o, 0, 2).reshape(b, h, n, dv), s_fin


def _ctx_attention(q, k, v, sink):
    b, h, s, d = q.shape
    kvh = k.shape[1]
    g = h // kvh
    nqb = s // CTX_BLOCK
    scale = d ** -0.5
    qb = jnp.moveaxis(q.reshape(b, kvh, g, nqb, CTX_BLOCK, d), 3, 0)

    def block(qi):
        logits = jnp.einsum('bkgqd,bksd->bkgqs', qi, k) * scale
        if sink is not None:
            sk = jnp.broadcast_to(sink.reshape(1, kvh, g, 1, 1), logits.shape[:-1] + (1,))
            logits = jnp.concatenate([logits, sk], axis=-1)
        p = jax.nn.softmax(logits, axis=-1)[..., :s]
        return jnp.einsum('bkgqs,bksd->bkgqd', p, v)

    out = lax.map(block, qb)
    return jnp.moveaxis(out, 0, 3).reshape(b, h, s, d)


def _swa_latent(q, k, v, ck, cv, sink):
    b, h, n, d = q.shape
    kvh = k.shape[1]
    g = h // kvh
    nb = n // SWA_BLOCK
    span = SWA_BLOCK + 2 * SWA_WINDOW
    scale = d ** -0.5
    qg = q.reshape(b, kvh, g, n, d)
    pad = ((0, 0), (0, 0), (SWA_WINDOW, SWA_WINDOW), (0, 0))
    kp = jnp.pad(k, pad)
    vp = jnp.pad(v, pad)
    sk = jnp.broadcast_to(sink.reshape(1, kvh, g, 1, 1), (b, kvh, g, SWA_BLOCK, 1))

    def block(i):
        start = i * SWA_BLOCK
        qi = lax.dynamic_slice_in_dim(qg, start, SWA_BLOCK, axis=3)
        ki = lax.dynamic_slice_in_dim(kp, start, span, axis=2)
        vi = lax.dynamic_slice_in_dim(vp, start, span, axis=2)
        qpos = start + jnp.arange(SWA_BLOCK)
        kpos = start - SWA_WINDOW + jnp.arange(span)
        mask = (jnp.abs(qpos[:, None] - kpos[None, :]) <= SWA_WINDOW) & (kpos[None, :] >= 0) & (kpos[None, :] < n)
        l_loc = jnp.where(mask, jnp.einsum('bkgqd,bksd->bkgqs', qi, ki) * scale, NEG_INF)
        l_ctx = jnp.einsum('bkgqd,bkpd->bkgqp', qi, ck) * scale
        p = jax.nn.softmax(jnp.concatenate([l_loc, l_ctx, sk], axis=-1), axis=-1)
        return (jnp.einsum('bkgqs,bksd->bkgqd', p[..., :span], vi)
                + jnp.einsum('bkgqp,bkpd->bkgqd', p[..., span:-1], cv))

    out = lax.map(block, jnp.arange(nb))
    return out.transpose(1, 2, 3, 0, 4, 5).reshape(b, h, n, d)


def _na_latent(q, k, v, ck, cv, rpb):
    b, h, n, d = q.shape
    rows = n // GRID_W
    kh = min(NA_KH, rows)
    scale = d ** -0.5
    qg = q.reshape(b, h, rows, GRID_W, d)
    kg = k.reshape(b, h, rows, GRID_W, d)
    vg = v.reshape(b, h, rows, GRID_W, d)
    col = jnp.arange(GRID_W)
    cs = jnp.clip(col - NA_KW // 2, 0, GRID_W - NA_KW)
    col_mask = (col[None, :] >= cs[:, None]) & (col[None, :] < cs[:, None] + NA_KW)
    dc = jnp.clip(col[None, :] - col[:, None], -(NA_KW - 1), NA_KW - 1) + NA_KW - 1
    rpb_c = rpb[:, :, dc]
    nloc = kh * GRID_W

    def row(r):
        rs = jnp.clip(r - kh // 2, 0, rows - kh)
        qr = lax.dynamic_index_in_dim(qg, r, axis=2, keepdims=False)
        kr = lax.dynamic_slice_in_dim(kg, rs, kh, axis=2)
        vr = lax.dynamic_slice_in_dim(vg, rs, kh, axis=2)
        dr = rs + jnp.arange(kh) - r + NA_KH - 1
        bias = jnp.take(rpb_c, dr, axis=1).transpose(0, 2, 1, 3)
        l_loc = jnp.einsum('bhqd,bhrkd->bhqrk', qr, kr) * scale + bias[None]
        l_loc = jnp.where(col_mask[:, None, :], l_loc, NEG_INF).reshape(b, h, GRID_W, nloc)
        l_ctx = jnp.einsum('bhqd,bhpd->bhqp', qr, ck) * scale
        p = jax.nn.softmax(jnp.concatenate([l_loc, l_ctx], axis=-1), axis=-1)
        p_loc = p[..., :nloc].reshape(b, h, GRID_W, kh, GRID_W)
        return (jnp.einsum('bhqrk,bhrkd->bhqd', p_loc, vr)
                + jnp.einsum('bhqp,bhpd->bhqd', p[..., nloc:], cv))

    out = lax.map(row, jnp.arange(rows))
    return out.transpose(1, 2, 0, 3, 4).reshape(b, h, n, d)


def _decay(a, alog, dtb):
    g = -jnp.exp(alog) * jax.nn.softplus(a + dtb)
    return g.transpose(0, 2, 1)


def _attention_mixers(hp, p, cache):
    latent = cache is not None
    s_q = hp[..., COL_B:COL_B + GROUP_WIDTH]
    s_k = hp[..., COL_B + GROUP_WIDTH:COL_B + GROUP_WIDTH + SWA_KV_WIDTH]
    s_v = hp[..., COL_B + GROUP_WIDTH + SWA_KV_WIDTH:COL_B + GROUP_WIDTH + 2 * SWA_KV_WIDTH]
    n_q, n_k, n_v = (hp[..., COL_C + i * GROUP_WIDTH:COL_C + (i + 1) * GROUP_WIDTH] for i in range(3))

    q = _rmsnorm(_heads(s_q, SWA_HEADS), p['swa_qnorm'])
    swa_k = _rmsnorm(_heads(s_k, SWA_KV_HEADS), p['swa_knorm'])
    swa_v = _heads(s_v, SWA_KV_HEADS)
    if latent:
        out_b = _swa_latent(_rope_2d(q), _rope_2d(swa_k), swa_v, cache['swa_k'], cache['swa_v'], p['swa_sink'])
    else:
        out_b = _ctx_attention(q, swa_k, swa_v, p['swa_sink'])

    q = _rmsnorm(_heads(n_q, NA_HEADS), p['na_qnorm'])
    na_k = _rmsnorm(_heads(n_k, NA_HEADS), p['na_knorm'])
    na_v = _heads(n_v, NA_HEADS)
    if latent:
        out_c = _na_latent(q, na_k, na_v, cache['na_k'], cache['na_v'], p['na_rpb'])
    else:
        out_c = _ctx_attention(q, na_k, na_v, None)

    y = jnp.concatenate([_merge(out_b), _merge(out_c)], axis=-1)
    if latent:
        return y, None
    return y, (swa_k, swa_v, na_k, na_v)


def kernel(x_prompt, x_sample, cache_swa_k, cache_swa_v, cache_na_k, cache_na_v, state_gla, state_gdn, c, c_ctx, w_mod, b_mod, norm1_g, norm2_g, w_in, w_out, gla_wup, gla_bup, gla_onorm, swa_qnorm, swa_knorm, swa_sink, na_qnorm, na_knorm, na_rpb, gdn_conv, gdn_alog, gdn_dtbias, gdn_onorm, ffn_w1, ffn_w3, ffn_w2, moe_router, moe_w1, moe_w3, moe_w2):
    bf16 = jnp.bfloat16
    x = jnp.concatenate([x_prompt.reshape(N_PROMPT_TOK, D_MODEL), x_sample.reshape(N_SAMPLE_TOK, D_MODEL)], axis=0)
    cvec = jnp.concatenate([c_ctx[None, :], c, jnp.zeros((N_MOD_ROWS - 1 - DEC_BATCH, D_MODEL), jnp.float32)], axis=0)
    mod_all = _modulation_all(cvec, w_mod, b_mod).reshape(DEPTH, N_MOD_ROWS, 1, MOD_W)
    w_in_b = jnp.pad(jnp.take(w_in, jnp.asarray(_IN_COL_PERM), axis=2),
                     ((0, 0), (0, 0), (0, IN_COLS_PAD - IN_COLS))).astype(bf16)
    w_out_b = w_out.astype(bf16)
    consts = _linear_consts()
    tile_heads = lambda g: jnp.tile(g, N_SLAB_HEADS)[None, :]
    per_head = lambda a: jnp.repeat(a, HEAD_DIM, axis=-1)
    small_expand = jnp.asarray(_small_expand_matrix())

    ctx_states = []
    for l in range(DEPTH):
        p = {'swa_qnorm': swa_qnorm[l], 'swa_knorm': swa_knorm[l], 'swa_sink': swa_sink[l],
             'na_qnorm': na_qnorm[l], 'na_knorm': na_knorm[l], 'na_rpb': na_rpb[l]}
        cache = {'swa_k': cache_swa_k[:, l], 'swa_v': cache_swa_v[:, l],
                 'na_k': cache_na_k[:, l], 'na_v': cache_na_v[:, l]}
        mod = mod_all[l]
        hp = _in_proj(x, mod, norm1_g[l][None, :], w_in_b[l])

        wup = jnp.zeros((2, LANES, SLAB), jnp.float32)
        wup = wup.at[0, 0:GLA_LOWRANK].set(gla_wup[l, 0]).at[1, GLA_LOWRANK:2 * GLA_LOWRANK].set(gla_wup[l, 1])
        gla_params = (wup, gla_bup[l], tile_heads(gla_onorm[l]))
        out_a, st_a = _linear_mixer('gla', hp, COL_A // (4 * SLAB), None, gla_params, consts, None)
        out_a, _ = _linear_mixer('gla', hp, COL_A // (4 * SLAB), _to_blockdiag(state_gla[:, l], True),
                                 gla_params, consts, out_a)
        gdn_params = (gdn_conv[l], jnp.concatenate([per_head(gdn_alog[l]), per_head(gdn_dtbias[l])], axis=0),
                      tile_heads(gdn_onorm[l]), small_expand)
        out_d, st_d = _linear_mixer('gdn', hp, COL_D // (4 * SLAB), None, gdn_params, consts, None)
        out_d, _ = _linear_mixer('gdn', hp, COL_D // (4 * SLAB), _to_blockdiag(state_gdn[:, l], False),
                                 gdn_params, consts, out_d)

        att_p, st = _attention_mixers(hp[:N_PROMPT_TOK].reshape(BATCH, SEQ, IN_COLS_PAD), p, None)
        att_s, _ = _attention_mixers(hp[N_PROMPT_TOK:].reshape(DEC_BATCH, DEC_SEQ, IN_COLS_PAD), p, cache)
        att = jnp.concatenate([att_p.reshape(N_PROMPT_TOK, 2 * GROUP_WIDTH),
                               att_s.reshape(N_SAMPLE_TOK, 2 * GROUP_WIDTH)], axis=0)
        st = st + (_from_blockdiag(st_a, True), _from_blockdiag(st_d, False))
        mix = jnp.concatenate([out_a, att, out_d], axis=-1)
        x = _out_proj(mix, x, mod, w_out_b[l])
        if l % 2 == 0:
            x = _ffn(x, mod, norm2_g[l][None, :], ffn_w1[l // 2].astype(bf16), ffn_w3[l // 2].astype(bf16),
                     ffn_w2[l // 2].astype(bf16))
        else:
            x = _moe(x, mod, norm2_g[l][None, :], moe_router[l // 2], moe_w1[l // 2].astype(bf16),
                     moe_w3[l // 2].astype(bf16), moe_w2[l // 2].astype(bf16))
        ctx_states.append(st)

    outs = [jnp.stack([st[j] for st in ctx_states], axis=1) for j in range(6)]
    y_prompt = x[:N_PROMPT_TOK].reshape(BATCH, SEQ, D_MODEL)
    y_sample = x[N_PROMPT_TOK:].reshape(DEC_BATCH, DEC_SEQ, D_MODEL)
    return (y_prompt, y_sample, *outs)
```

```python
import functools
import math

import numpy as np
import jax
import jax.numpy as jnp
from jax import lax
from jax.experimental import pallas as pl
from jax.experimental.pallas import tpu as pltpu

D_MODEL = 1024
BATCH = 32
SEQ = 256
DEPTH = 4
DEC_BATCH = 8
DEC_SEQ = 1024
PAST_LEN = 512

GRID_W = 64
HEAD_DIM = 64
GROUP_WIDTH = D_MODEL // 4
GROUP_HEADS = GROUP_WIDTH // HEAD_DIM
GLA_HEADS = GROUP_HEADS
GLA_LOWRANK = 16
GLA_TAU = 16.0
GLA_CHUNK = 64
SWA_HEADS = GROUP_HEADS
SWA_KV_HEADS = 2
SWA_WINDOW = 128
SWA_BLOCK = 128
NA_HEADS = GROUP_HEADS
NA_KH = 8
NA_KW = 16
GDN_HEADS = GROUP_HEADS
GDN_CONV = 3
GDN_CHUNK = 64
D_FF = 2816
N_EXPERTS = 8
TOP_K = 2
D_FF_EXPERT = 1024
ROPE_BASE = 10000.0
CTX_BLOCK = 128
EPS = 1e-6
NEG_INF = -1e30
SWA_KV_WIDTH = SWA_KV_HEADS * HEAD_DIM
IN_SPLITS = ([GROUP_WIDTH] * 4 + [GLA_LOWRANK] * 2 + [GROUP_WIDTH, SWA_KV_WIDTH, SWA_KV_WIDTH]
             + [GROUP_WIDTH] * 3 + [GROUP_WIDTH] * 4 + [GDN_HEADS] * 4)
IN_COLS = sum(IN_SPLITS)

LANES = 128
VMEM_LIMIT_BYTES = 56 * 1024 * 1024

N_PROMPT_TOK = BATCH * SEQ
N_SAMPLE_TOK = DEC_BATCH * DEC_SEQ
N_TOK = N_PROMPT_TOK + N_SAMPLE_TOK
N_MOD_ROWS = 16
MOD_W = 6 * D_MODEL
TM = 512
IN_COLS_PAD = -(-IN_COLS // LANES) * LANES

COL_A = 0
COL_D = 4 * GROUP_WIDTH
COL_C = 8 * GROUP_WIDTH
COL_B = 11 * GROUP_WIDTH
COL_SMALL = COL_B + GROUP_WIDTH + 2 * SWA_KV_WIDTH
SMALL_COL_BLOCK = COL_SMALL // LANES
SMALL_GDN = 2 * GLA_LOWRANK


SWA_Q_HEAD_ORDER = (0, 2, 1, 3)
_SWA_Q_LANES = np.concatenate([h * HEAD_DIM + np.arange(HEAD_DIM) for h in SWA_Q_HEAD_ORDER])


def _in_col_perm():
    off = np.concatenate([[0], np.cumsum(IN_SPLITS)])
    seg = lambda a, b: np.arange(off[a], off[b])
    return np.concatenate([seg(0, 4), seg(12, 16), seg(9, 12), off[6] + _SWA_Q_LANES, seg(7, 9), seg(4, 6),
                           seg(16, 20)])


_IN_COL_PERM = _in_col_perm()
_OUT_ROW_PERM = np.concatenate([np.arange(GROUP_WIDTH), GROUP_WIDTH + _SWA_Q_LANES,
                                np.arange(2 * GROUP_WIDTH, D_MODEL)])


def _small_expand_matrix():
    e = np.zeros((LANES, 4 * GROUP_WIDTH), np.float32)
    for s in range(4):
        for h in range(GROUP_HEADS):
            e[SMALL_GDN + s * GROUP_HEADS + h, s * GROUP_WIDTH + h * HEAD_DIM:s * GROUP_WIDTH + (h + 1) * HEAD_DIM] = 1.0
    return e


def _mod_row(i):
    n_prompt_tiles = N_PROMPT_TOK // TM
    return jnp.where(i < n_prompt_tiles, 0, 1 + (i - n_prompt_tiles) // (DEC_SEQ // TM))


def _cparams(*sem):
    return pltpu.CompilerParams(dimension_semantics=sem, vmem_limit_bytes=VMEM_LIMIT_BYTES)


def _mod_kernel(c_ref, w_ref, b_ref, o_ref):
    c = c_ref[...]
    s = c * jax.nn.sigmoid(c)
    o_ref[...] = jnp.dot(s, w_ref[...], preferred_element_type=jnp.float32,
                         precision=lax.Precision.HIGHEST) + b_ref[...]


def _modulation_all(cvec, w_mod, b_mod):
    tn = 1536
    return pl.pallas_call(
        _mod_kernel,
        grid=(DEPTH, MOD_W // tn),
        in_specs=[pl.BlockSpec((N_MOD_ROWS, D_MODEL), lambda l, j: (0, 0)),
                  pl.BlockSpec((None, D_MODEL, tn), lambda l, j: (l, 0, j)),
                  pl.BlockSpec((None, 1, tn), lambda l, j: (l, 0, j))],
        out_specs=pl.BlockSpec((None, N_MOD_ROWS, tn), lambda l, j: (l, 0, j)),
        out_shape=jax.ShapeDtypeStruct((DEPTH, N_MOD_ROWS, MOD_W), jnp.float32),
        compiler_params=_cparams("arbitrary", "arbitrary"),
        name="modulation",
    )(cvec, w_mod, b_mod.reshape(DEPTH, 1, MOD_W))


def _modulated_norm(x, g, shift, scale):
    y = x * lax.rsqrt(jnp.mean(x * x, axis=-1, keepdims=True) + EPS) * g
    return y * (1.0 + scale) + shift


def _in_proj_kernel(x_ref, mod_ref, g_ref, w_ref, o_ref):
    h = _modulated_norm(x_ref[...], g_ref[...], mod_ref[:, 0:D_MODEL], mod_ref[:, D_MODEL:2 * D_MODEL])
    o_ref[...] = jnp.dot(h.astype(jnp.bfloat16), w_ref[...], preferred_element_type=jnp.float32)


def _in_proj(x, mod, g, w):
    return pl.pallas_call(
        _in_proj_kernel,
        grid=(N_TOK // TM,),
        in_specs=[pl.BlockSpec((TM, D_MODEL), lambda i: (i, 0)),
                  pl.BlockSpec((None, 1, MOD_W), lambda i: (_mod_row(i), 0, 0)),
                  pl.BlockSpec((1, D_MODEL), lambda i: (0, 0)),
                  pl.BlockSpec((D_MODEL, IN_COLS_PAD), lambda i: (0, 0), pipeline_mode=pl.Buffered(1))],
        out_specs=pl.BlockSpec((TM, IN_COLS_PAD), lambda i: (i, 0)),
        out_shape=jax.ShapeDtypeStruct((N_TOK, IN_COLS_PAD), jnp.float32),
        compiler_params=_cparams("arbitrary"),
        name="in_proj",
    )(x, mod, g, w)


def _out_proj_kernel(a_ref, att_ref, d_ref, x_ref, mod_ref, w_ref, o_ref):
    mix = jnp.concatenate([a_ref[...], att_ref[...], d_ref[...]], axis=1)
    y = jnp.dot(mix.astype(jnp.bfloat16), w_ref[...], preferred_element_type=jnp.float32)
    o_ref[...] = x_ref[...] + mod_ref[:, 2 * D_MODEL:3 * D_MODEL] * y


def _out_proj(out_a, att, out_d, x, mod, w):
    return pl.pallas_call(
        _out_proj_kernel,
        grid=(N_TOK // TM,),
        in_specs=[pl.BlockSpec((TM, GROUP_WIDTH), lambda i: (i, 0)),
                  pl.BlockSpec((TM, 2 * GROUP_WIDTH), lambda i: (i, 0)),
                  pl.BlockSpec((TM, GROUP_WIDTH), lambda i: (i, 0)),
                  pl.BlockSpec((TM, D_MODEL), lambda i: (i, 0)),
                  pl.BlockSpec((None, 1, MOD_W), lambda i: (_mod_row(i), 0, 0)),
                  pl.BlockSpec((D_MODEL, D_MODEL), lambda i: (0, 0), pipeline_mode=pl.Buffered(1))],
        out_specs=pl.BlockSpec((TM, D_MODEL), lambda i: (i, 0)),
        out_shape=jax.ShapeDtypeStruct((N_TOK, D_MODEL), jnp.float32),
        compiler_params=_cparams("arbitrary"),
        name="out_proj",
    )(out_a, att, out_d, x, mod, w)


FF_CHUNK = D_FF // 2


def _ffn_kernel(x_ref, mod_ref, g_ref, w1_ref, w3_ref, w2_ref, o_ref):
    x = x_ref[...]
    h = _modulated_norm(x, g_ref[...], mod_ref[:, 3 * D_MODEL:4 * D_MODEL], mod_ref[:, 4 * D_MODEL:5 * D_MODEL])
    hb = h.astype(jnp.bfloat16)
    y = jnp.zeros((TM, D_MODEL), jnp.float32)
    for c0 in range(0, D_FF, FF_CHUNK):
        a = jnp.dot(hb, w1_ref[:, c0:c0 + FF_CHUNK], preferred_element_type=jnp.float32)
        b = jnp.dot(hb, w3_ref[:, c0:c0 + FF_CHUNK], preferred_element_type=jnp.float32)
        s = (a * jax.nn.sigmoid(a) * b).astype(jnp.bfloat16)
        y = y + jnp.dot(s, w2_ref[c0:c0 + FF_CHUNK, :], preferred_element_type=jnp.float32)
    o_ref[...] = x + mod_ref[:, 5 * D_MODEL:6 * D_MODEL] * y


def _ffn(x, mod, g, w1, w3, w2):
    resident = dict(pipeline_mode=pl.Buffered(1))
    return pl.pallas_call(
        _ffn_kernel,
        grid=(N_TOK // TM,),
        in_specs=[pl.BlockSpec((TM, D_MODEL), lambda i: (i, 0)),
                  pl.BlockSpec((None, 1, MOD_W), lambda i: (_mod_row(i), 0, 0)),
                  pl.BlockSpec((1, D_MODEL), lambda i: (0, 0)),
                  pl.BlockSpec((D_MODEL, D_FF), lambda i: (0, 0), **resident),
                  pl.BlockSpec((D_MODEL, D_FF), lambda i: (0, 0), **resident),
                  pl.BlockSpec((D_FF, D_MODEL), lambda i: (0, 0), **resident)],
        out_specs=pl.BlockSpec((TM, D_MODEL), lambda i: (i, 0)),
        out_shape=jax.ShapeDtypeStruct((N_TOK, D_MODEL), jnp.float32),
        compiler_params=_cparams("arbitrary"),
        name="ffn_dense",
    )(x, mod, g, w1, w3, w2)


def _moe_kernel(x_ref, mod_ref, g_ref, r_ref, w1_ref, w3_ref, w2_ref, o_ref, hb_ref, gate_ref):
    e = pl.program_id(1)

    @pl.when(e == 0)
    def _():
        h = _modulated_norm(x_ref[...], g_ref[...], mod_ref[:, 3 * D_MODEL:4 * D_MODEL],
                            mod_ref[:, 4 * D_MODEL:5 * D_MODEL])
        hb_ref[...] = h.astype(jnp.bfloat16)
        logits = jnp.dot(h, r_ref[...], preferred_element_type=jnp.float32,
                         precision=lax.Precision.HIGHEST)
        lane = lax.broadcasted_iota(jnp.int32, logits.shape, 1)
        v1 = jnp.max(logits, axis=-1, keepdims=True)
        i1 = jnp.min(jnp.where(logits == v1, lane, N_EXPERTS), axis=-1, keepdims=True)
        rest = jnp.where(lane == i1, -jnp.inf, logits)
        v2 = jnp.max(rest, axis=-1, keepdims=True)
        i2 = jnp.min(jnp.where(rest == v2, lane, N_EXPERTS), axis=-1, keepdims=True)
        p2 = jnp.exp(v2 - v1)
        den = 1.0 + p2
        gate_ref[...] = jnp.where(lane == i1, 1.0 / den, 0.0) + jnp.where(lane == i2, p2 / den, 0.0)
        o_ref[...] = jnp.zeros_like(o_ref)

    hb = hb_ref[...]
    a = jnp.dot(hb, w1_ref[...], preferred_element_type=jnp.float32)
    b = jnp.dot(hb, w3_ref[...], preferred_element_type=jnp.float32)
    s = (a * jax.nn.sigmoid(a) * b).astype(jnp.bfloat16)
    y = jnp.dot(s, w2_ref[...], preferred_element_type=jnp.float32)
    gates = gate_ref[...]
    lane = lax.broadcasted_iota(jnp.int32, gates.shape, 1)
    ge = jnp.sum(jnp.where(lane == e, gates, 0.0), axis=-1, keepdims=True)
    o_ref[...] += ge * y

    @pl.when(e == N_EXPERTS - 1)
    def _():
        o_ref[...] = x_ref[...] + mod_ref[:, 5 * D_MODEL:6 * D_MODEL] * o_ref[...]


def _moe(x, mod, g, router, w1, w3, w2):
    return pl.pallas_call(
        _moe_kernel,
        grid=(N_TOK // TM, N_EXPERTS),
        in_specs=[pl.BlockSpec((TM, D_MODEL), lambda i, e: (i, 0)),
                  pl.BlockSpec((None, 1, MOD_W), lambda i, e: (_mod_row(i), 0, 0)),
                  pl.BlockSpec((1, D_MODEL), lambda i, e: (0, 0)),
                  pl.BlockSpec((D_MODEL, N_EXPERTS), lambda i, e: (0, 0)),
                  pl.BlockSpec((None, D_MODEL, D_FF_EXPERT), lambda i, e: (e, 0, 0)),
                  pl.BlockSpec((None, D_MODEL, D_FF_EXPERT), lambda i, e: (e, 0, 0)),
                  pl.BlockSpec((None, D_FF_EXPERT, D_MODEL), lambda i, e: (e, 0, 0))],
        out_specs=pl.BlockSpec((TM, D_MODEL), lambda i, e: (i, 0)),
        out_shape=jax.ShapeDtypeStruct((N_TOK, D_MODEL), jnp.float32),
        scratch_shapes=[pltpu.VMEM((TM, D_MODEL), jnp.bfloat16),
                        pltpu.VMEM((TM, N_EXPERTS), jnp.float32)],
        compiler_params=_cparams("arbitrary", "arbitrary"),
        name="ffn_moe",
    )(x, mod, g, router, w1, w3, w2)


CH = GLA_CHUNK
SLAB = GROUP_WIDTH
N_SLAB_HEADS = SLAB // HEAD_DIM
_HI = lax.Precision.HIGHEST


def _linear_consts():
    i = np.arange(CH)[:, None]
    j = (np.arange(SLAB) % CH)[None, :]
    slab = np.stack([i >= j, i > j, i <= j, i < j, i == j]).astype(np.float32)
    r = np.arange(SLAB) // CH
    bd = (r[:, None] == r[None, :]).astype(np.float32)
    t = np.arange(CH)
    tri = np.stack([t[:, None] >= t[None, :], t[:, None] <= t[None, :]]).astype(np.float32)
    return jnp.asarray(slab), jnp.asarray(bd), jnp.asarray(tri, dtype=jnp.bfloat16)


def _bf(x):
    return x.astype(jnp.bfloat16)


def _split2(x):
    hi = _bf(x)
    return hi, _bf(x - hi.astype(jnp.float32))


def _split3_lanes(x):
    hi = _bf(x)
    r1 = x - hi.astype(jnp.float32)
    mid = _bf(r1)
    lo = _bf(r1 - mid.astype(jnp.float32))
    return jnp.concatenate([hi, mid, lo], axis=1)


def _mm(a, b):
    return jnp.dot(a, b, preferred_element_type=jnp.float32)


def _mm_nt(a, b):
    return lax.dot_general(a, b, (((1,), (1,)), ((), ())), preferred_element_type=jnp.float32)


def _mm_tn(a, b):
    return lax.dot_general(a, b, (((0,), (0,)), ((), ())), preferred_element_type=jnp.float32)


def _blockdiag(y_b, bd_b):
    return jnp.concatenate([y_b] * N_SLAB_HEADS, axis=0) * bd_b


def _exact_rows_mm(lhs_b, x):
    c = _mm(lhs_b, _split3_lanes(x))
    return c[:, 0:SLAB] + c[:, SLAB:2 * SLAB] + c[:, 2 * SLAB:3 * SLAB]


def _head_mm3(lhs, y, bd_b):
    m = lhs.shape[0]
    lh, ll = _split2(lhs)
    yh, yl = _split2(y)
    a = _mm(jnp.concatenate([lh, ll], axis=0), _blockdiag(yh, bd_b))
    return a[0:m] + a[m:2 * m] + _mm(lh, _blockdiag(yl, bd_b))


def _unit_triangular_inverse(nmat, eye, bd_b):
    t = eye + nmat
    p = _head_mm3(nmat, nmat, bd_b)
    n_doublings = int(math.log2(CH)) - 1
    for it in range(n_doublings):
        if it < n_doublings - 1:
            res = _head_mm3(jnp.concatenate([t, p], axis=0), p, bd_b)
            t = t + res[0:CH]
            p = res[CH:2 * CH]
        else:
            t = t + _head_mm3(t, p, bd_b)
    return t


def _group_sum(x, bd_f):
    return jnp.dot(x, bd_f, preferred_element_type=jnp.float32, precision=_HI)


def _softplus(z):
    return jnp.maximum(z, 0.0) + jnp.log1p(jnp.exp(-jnp.abs(z)))


def _silu(z):
    return z * jax.nn.sigmoid(z)


def _gdn_chunk(qn, kn, v, beta, g, st_ref, d, slab_ref, bd_b, tri_ref):
    incl = slab_ref[2 * d]
    strict = slab_ref[2 * d + 1]
    eye = slab_ref[4]
    c = _exact_rows_mm(tri_ref[d], g)
    r_row = _exact_rows_mm(jnp.ones((8, CH), jnp.bfloat16), c * eye)[0:1]
    dec = jnp.exp(jnp.where(incl > 0.5, c - r_row, 0.0)) * incl
    kn_b = _bf(kn)
    kbd = _blockdiag(kn_b, bd_b)
    kk = _mm_nt(kn_b, kbd)
    qk = _mm_nt(_bf(qn), kbd) * dec
    t_inv = _unit_triangular_inverse(-(strict * beta * dec * kk), eye, bd_b)
    g_last = c[0:1] if d == 1 else c[CH - 1:CH]
    e_c = jnp.exp(c)
    t_b = _bf(t_inv)
    w_v = _mm(t_b, _blockdiag(_bf(beta * v), bd_b))
    w_k = _mm(t_b, _blockdiag(_bf(beta * e_c * kn), bd_b))
    q_dec = qn * e_c
    k_end = kn * jnp.exp(g_last - c)
    s = st_ref[...]
    s_b = _bf(s)
    u = w_v - _mm(_bf(w_k), s_b)
    u_b = _bf(u)
    o = _mm(_bf(q_dec), s_b) + _mm(_bf(qk), _blockdiag(u_b, bd_b))
    st_ref[...] = s * jnp.exp(g_last) + bd_b.astype(jnp.float32) * _mm_tn(_bf(k_end), u_b)
    return o


def _gdn_kernel(n, has_s0, *refs):
    refs = list(refs)
    x_ref, small_ref = refs[0:2]
    k = 2
    if has_s0:
        s0_ref = refs[k]
        k += 1
    conv_ref, dec_ref, gain_ref, e_ref, slab_ref, bd_ref, tri_ref = refs[k:k + 7]
    k += 7
    if has_s0:
        k += 1
    o_ref, sfin_ref, qn_s, kn_s, v_s, g_s, b_s, of_s, ob_s, st_s = refs[k:]
    nc = n // CH
    bd_f = bd_ref[...]
    bd_b = _bf(bd_f)

    x = x_ref[:, 0:3 * SLAB]
    row = lax.broadcasted_iota(jnp.int32, (n, 1), 0)
    prev = jnp.where(row == 0, 0.0, pltpu.roll(x, 1, axis=0))
    nxt = jnp.where(row == n - 1, 0.0, pltpu.roll(x, n - 1, axis=0))
    y = _silu(prev * conv_ref[0:1, :] + x * conv_ref[1:2, :] + nxt * conv_ref[2:3, :])
    q = y[:, 0:SLAB]
    kk = y[:, SLAB:2 * SLAB]
    qn_s[...] = q * lax.rsqrt(_group_sum(q * q, bd_f) + EPS) * (HEAD_DIM ** -0.5)
    kn_s[...] = kk * lax.rsqrt(_group_sum(kk * kk, bd_f) + EPS)
    v_s[...] = y[:, 2 * SLAB:3 * SLAB]
    bc = jnp.dot(small_ref[...], e_ref[...], preferred_element_type=jnp.float32, precision=_HI)
    for d in range(2):
        z = bc[:, d * SLAB:(d + 1) * SLAB] + dec_ref[2 + d:3 + d, :]
        g_s[d] = -jnp.exp(dec_ref[d:d + 1, :]) * _softplus(z)
        b_s[d] = jax.nn.sigmoid(bc[:, (2 + d) * SLAB:(3 + d) * SLAB])
    if has_s0:
        st_s[...] = s0_ref[...]
    else:
        st_s[...] = jnp.zeros_like(st_s)

    def body(c, carry):
        for d in range(2):
            cc = c if d == 0 else nc - 1 - c
            rows = pl.ds(pl.multiple_of(cc * CH, CH), CH)
            o = _gdn_chunk(qn_s[rows, :], kn_s[rows, :], v_s[rows, :], b_s[d, rows, :], g_s[d, rows, :],
                           st_s.at[d], d, slab_ref, bd_b, tri_ref)
            if d == 0:
                of_s[rows, :] = o
            else:
                ob_s[rows, :] = o
        return carry

    lax.fori_loop(0, nc, body, 0)
    sfin_ref[...] = st_s[...]
    o = of_s[...] + ob_s[...]
    o = o * lax.rsqrt(_group_sum(o * o, bd_f) * (1.0 / HEAD_DIM) + EPS) * gain_ref[...]
    o_ref[...] = o * _silu(x_ref[:, 3 * SLAB:4 * SLAB])


def _gla_chunk(q, k, v, la, st_ref, d, slab_ref, bd_b, tri_ref):
    incl = slab_ref[2 * d]
    cum = _exact_rows_mm(tri_ref[d], la)
    last = cum[0:1] if d == 1 else cum[CH - 1:CH]
    q_dec = _bf(q * (HEAD_DIM ** -0.5) * jnp.exp(cum))
    k_inv = _bf(k * jnp.exp(-cum))
    k_end = _bf(k * jnp.exp(last - cum))
    v_b = _bf(v)
    att = _mm_nt(q_dec, _blockdiag(k_inv, bd_b)) * incl
    s_t = st_ref[...]
    o = _mm(_bf(att), _blockdiag(v_b, bd_b)) + _mm_nt(q_dec, _bf(s_t))
    st_ref[...] = s_t * jnp.exp(last) + bd_b.astype(jnp.float32) * _mm_tn(v_b, k_end)
    return o


def _gla_kernel(n, has_s0, *refs):
    refs = list(refs)
    x_ref, small_ref = refs[0:2]
    k = 2
    if has_s0:
        s0_ref = refs[k]
        k += 1
    wup_ref, bup_ref, gain_ref, slab_ref, bd_ref, tri_ref = refs[k:k + 6]
    k += 6
    if has_s0:
        k += 1
    o_ref, sfin_ref, la_s, of_s, ob_s, st_s = refs[k:]
    nc = n // CH
    bd_f = bd_ref[...]
    bd_b = _bf(bd_f)
    small = small_ref[...]
    for d in range(2):
        z = jnp.dot(small, wup_ref[d], preferred_element_type=jnp.float32, precision=_HI) + bup_ref[d:d + 1, :]
        la_s[d] = (jnp.minimum(z, 0.0) - jnp.log1p(jnp.exp(-jnp.abs(z)))) * (1.0 / GLA_TAU)
    if has_s0:
        st_s[...] = s0_ref[...]
    else:
        st_s[...] = jnp.zeros_like(st_s)

    def body(c, carry):
        for d in range(2):
            cc = c if d == 0 else nc - 1 - c
            rows = pl.ds(pl.multiple_of(cc * CH, CH), CH)
            o = _gla_chunk(x_ref[rows, 0:SLAB], x_ref[rows, SLAB:2 * SLAB], x_ref[rows, 2 * SLAB:3 * SLAB],
                           la_s[d, rows, :], st_s.at[d], d, slab_ref, bd_b, tri_ref)
            if d == 0:
                of_s[rows, :] = o
            else:
                ob_s[rows, :] = o
        return carry

    lax.fori_loop(0, nc, body, 0)
    sfin_ref[...] = st_s[...]
    o = of_s[...] + ob_s[...]
    o = o * lax.rsqrt(_group_sum(o * o, bd_f) * (1.0 / HEAD_DIM) + EPS) * gain_ref[...]
    o_ref[...] = o * _silu(x_ref[:, 3 * SLAB:4 * SLAB])


def _linear_mixer(kind, hp, col_block, s0_bd, params, consts, prev_out):
    latent = prev_out is not None
    n = DEC_SEQ if latent else SEQ
    n_seq = DEC_BATCH if latent else BATCH
    row0 = N_PROMPT_TOK // n if latent else 0
    full = lambda a: pl.BlockSpec(a.shape, lambda b: (0,) * a.ndim)
    in_specs = [pl.BlockSpec((n, 4 * SLAB), lambda b: (row0 + b, col_block)),
                pl.BlockSpec((n, LANES), lambda b: (row0 + b, SMALL_COL_BLOCK))]
    args = [hp, hp]
    if latent:
        in_specs.append(pl.BlockSpec((None, 2, SLAB, SLAB), lambda b: (b, 0, 0, 0)))
        args.append(s0_bd)
    for a in tuple(params) + tuple(consts):
        in_specs.append(full(a))
        args.append(a)
    aliases = {}
    if latent:
        aliases = {len(args): 0}
        in_specs.append(pl.BlockSpec(memory_space=pl.ANY))
        args.append(prev_out)
    seq_buf = pltpu.VMEM((n, SLAB), jnp.float32)
    dir_buf = pltpu.VMEM((2, n, SLAB), jnp.float32)
    state_buf = pltpu.VMEM((2, SLAB, SLAB), jnp.float32)
    if kind == 'gdn':
        body = functools.partial(_gdn_kernel, n, latent)
        scratch = [seq_buf, seq_buf, seq_buf, dir_buf, dir_buf, seq_buf, seq_buf, state_buf]
    else:
        body = functools.partial(_gla_kernel, n, latent)
        scratch = [dir_buf, seq_buf, seq_buf, state_buf]
    return pl.pallas_call(
        body,
        grid=(n_seq,),
        in_specs=in_specs,
        out_specs=[pl.BlockSpec((n, SLAB), lambda b: (row0 + b, 0)),
                   pl.BlockSpec((None, 2, SLAB, SLAB), lambda b: (b, 0, 0, 0))],
        out_shape=[jax.ShapeDtypeStruct((N_TOK, SLAB), jnp.float32),
                   jax.ShapeDtypeStruct((n_seq, 2, SLAB, SLAB), jnp.float32)],
        scratch_shapes=scratch,
        input_output_aliases=aliases,
        compiler_params=_cparams("arbitrary"),
        name=kind + ("_latent" if latent else "_ctx"),
    )(*args)


def _to_blockdiag(s, transpose):
    if transpose:
        s = jnp.swapaxes(s, -1, -2)
    eye = jnp.eye(N_SLAB_HEADS, dtype=s.dtype)
    return jnp.einsum('bdhij,hg->bdhigj', s, eye).reshape(s.shape[0], 2, SLAB, SLAB)


def _from_blockdiag(sbd, transpose):
    t = sbd.reshape(sbd.shape[0], 2, N_SLAB_HEADS, HEAD_DIM, N_SLAB_HEADS, HEAD_DIM)
    s = jnp.stack([t[:, :, h, :, h, :] for h in range(N_SLAB_HEADS)], axis=2)
    return jnp.swapaxes(s, -1, -2) if transpose else s


ATT_SCALE = HEAD_DIM ** -0.5
NA_ROWS = DEC_SEQ // GRID_W
NA_KROWS = min(NA_KH, NA_ROWS)
NA_WIN = NA_KROWS * GRID_W
N_SWA_BLOCKS = DEC_SEQ // SWA_BLOCK


def _head_rmsnorm(x, bd_f, gain):
    return x * lax.rsqrt(_group_sum(x * x, bd_f) * (1.0 / HEAD_DIM) + EPS) * gain


def _lane_group(shape):
    return lax.broadcasted_iota(jnp.int32, shape, 1) // HEAD_DIM


def _stack_groups(x):
    grp = _lane_group(x.shape)
    return jnp.concatenate([jnp.where(grp == g, x, jnp.zeros_like(x)) for g in range(x.shape[1] // HEAD_DIM)], axis=0)


def _stack_swa_queries(q_b):
    return jnp.concatenate([_stack_groups(q_b[:, 0:LANES]), _stack_groups(q_b[:, LANES:2 * LANES])], axis=0)


def _unstack_swa(o, m):
    low = _lane_group((m, LANES)) == 0
    return jnp.concatenate([jnp.where(low, o[0:m], o[m:2 * m]), jnp.where(low, o[2 * m:3 * m], o[3 * m:4 * m])],
                           axis=1)


def _unstack_groups(o, m):
    grp = _lane_group((m, o.shape[1]))
    out = jnp.where(grp == 0, o[0:m], 0.0)
    for g in range(1, o.shape[1] // HEAD_DIM):
        out = out + jnp.where(grp == g, o[g * m:(g + 1) * m], 0.0)
    return out


def _softmax_pv(logits, values, extra=None):
    m = jnp.max(logits[0], axis=-1, keepdims=True)
    for l in logits[1:]:
        m = jnp.maximum(m, jnp.max(l, axis=-1, keepdims=True))
    if extra is not None:
        m = jnp.maximum(m, extra)
    den = jnp.exp(extra - m) if extra is not None else 0.0
    pv = 0.0
    for l, v in zip(logits, values):
        e = jnp.exp(l - m)
        den = den + jnp.sum(e, axis=-1, keepdims=True)
        pv = pv + _mm(_bf(e), v)
    return pv / den


def _sink_column(sink_ref, m):
    return jnp.concatenate([jnp.full((m, 1), sink_ref[g], jnp.float32) for g in range(SWA_HEADS)], axis=0)


def _rope(x, rope_ref):
    reps = x.shape[1] // LANES
    wide = lambda i: jnp.concatenate([rope_ref[i]] * reps, axis=1)
    shift = HEAD_DIM // 4
    return (x * wide(0) + pltpu.roll(x, x.shape[1] - shift, axis=1) * wide(1)
            + pltpu.roll(x, shift, axis=1) * wide(2))


def _ctx_attn_kernel(bq_ref, bk_ref, bv_ref, cq_ref, ck_ref, cv_ref, gain_ref, sink_ref, bd_ref,
                     o_ref, swak_ref, swav_ref, nak_ref, nav_ref):
    n = SEQ
    bd_f = bd_ref[...]
    q = _head_rmsnorm(bq_ref[...], bd_f, gain_ref[0:1, :])
    k = _head_rmsnorm(bk_ref[...], bd_f[0:LANES, 0:LANES], gain_ref[1:2, 0:LANES])
    v = bv_ref[...]
    for h in range(SWA_KV_HEADS):
        swak_ref[h] = k[:, h * HEAD_DIM:(h + 1) * HEAD_DIM]
        swav_ref[h] = v[:, h * HEAD_DIM:(h + 1) * HEAD_DIM]
    logits = _mm_nt(_stack_swa_queries(_bf(q)), _bf(k)) * ATT_SCALE
    out_b = _unstack_swa(_softmax_pv([logits], [_bf(v)], _sink_column(sink_ref, n)), n)

    q = _head_rmsnorm(cq_ref[...], bd_f, gain_ref[2:3, :])
    k = _head_rmsnorm(ck_ref[...], bd_f, gain_ref[3:4, :])
    v = cv_ref[...]
    for h in range(NA_HEADS):
        nak_ref[h] = k[:, h * HEAD_DIM:(h + 1) * HEAD_DIM]
        nav_ref[h] = v[:, h * HEAD_DIM:(h + 1) * HEAD_DIM]
    logits = _mm_nt(_stack_groups(_bf(q)), _bf(k)) * ATT_SCALE
    out_c = _unstack_groups(_softmax_pv([logits], [_bf(v)]), n)
    o_ref[...] = jnp.concatenate([out_b, out_c], axis=1)


def _swa_latent_kernel(q_ref, k_ref, v_ref, ck_ref, cv_ref, rope_ref, gain_ref, sink_ref, bd_ref, prev_ref,
                       o_ref, q_s, k_s, v_s):
    del prev_ref
    bd_f = bd_ref[...]
    q_s[...] = _bf(_rope(_head_rmsnorm(q_ref[...], bd_f, gain_ref[0:1, :]), rope_ref))
    k_s[...] = _bf(_rope(_head_rmsnorm(k_ref[...], bd_f[0:LANES, 0:LANES], gain_ref[1:2, 0:LANES]), rope_ref))
    v_s[...] = _bf(v_ref[...])
    m = SWA_BLOCK
    iq = lax.broadcasted_iota(jnp.int32, (SWA_HEADS * m, m), 0) % m
    jk = lax.broadcasted_iota(jnp.int32, (SWA_HEADS * m, m), 1)
    ok_prev = jk >= iq
    ok_next = jk <= iq
    ok_same = jk >= 0
    sink = _sink_column(sink_ref, m)
    for i in range(N_SWA_BLOCKS):
        lo, hi = max(i - 1, 0), min(i + 1, N_SWA_BLOCKS - 1)
        qs = _stack_swa_queries(q_s[i * m:(i + 1) * m, :])
        allowed = jnp.concatenate([ok_prev if j < i else ok_next if j > i else ok_same for j in range(lo, hi + 1)],
                                  axis=1)
        l_loc = jnp.where(allowed, _mm_nt(qs, k_s[lo * m:(hi + 1) * m, :]) * ATT_SCALE, NEG_INF)
        l_ctx = _mm_nt(qs, ck_ref[...]) * ATT_SCALE
        o = _softmax_pv([l_loc, l_ctx], [v_s[lo * m:(hi + 1) * m, :], cv_ref[...]], sink)
        o_ref[i * m:(i + 1) * m, :] = _unstack_swa(o, m)


def _na_latent_kernel(q_ref, k_ref, v_ref, ck_ref, cv_ref, bias_ref, gain_ref, bd_ref, prev_ref,
                      o_ref, q_s, k_s, v_s):
    del prev_ref
    bd_f = bd_ref[...]
    q_s[...] = _bf(_head_rmsnorm(q_ref[...], bd_f, gain_ref[2:3, :]))
    k_s[...] = _bf(_head_rmsnorm(k_ref[...], bd_f, gain_ref[3:4, :]))
    v_s[...] = _bf(v_ref[...])

    def row(r, carry):
        rs = jnp.clip(r - NA_KROWS // 2, 0, NA_ROWS - NA_KROWS)
        dr0 = rs - r + NA_KH - 1
        qs = _stack_groups(q_s[pl.ds(pl.multiple_of(r * GRID_W, GRID_W), GRID_W), :])
        win = pl.ds(pl.multiple_of(rs * GRID_W, GRID_W), NA_WIN)
        bias = jnp.concatenate([bias_ref[dr0 + 2 * p] for p in range(NA_KROWS // 2)], axis=1)
        l_loc = _mm_nt(qs, k_s[win, :]) * ATT_SCALE + bias
        l_ctx = _mm_nt(qs, ck_ref[...]) * ATT_SCALE
        o = _softmax_pv([l_loc, l_ctx], [v_s[win, :], cv_ref[...]])
        o_ref[pl.ds(pl.multiple_of(r * GRID_W, GRID_W), GRID_W), :] = _unstack_groups(o, GRID_W)
        return carry

    lax.fori_loop(0, NA_ROWS, row, 0)


def _rope_tables():
    t = jnp.arange(DEC_SEQ)
    rows = (t // GRID_W).astype(jnp.float32)
    cols = (t % GRID_W).astype(jnp.float32)
    half = HEAD_DIM // 2
    nf = half // 2
    inv = 1.0 / (ROPE_BASE ** (jnp.arange(nf, dtype=jnp.float32) / nf))
    d = np.arange(LANES) % HEAD_DIM
    pos = jnp.where(jnp.asarray(d < half)[None, :], rows[:, None], cols[:, None])
    ang = pos * inv[jnp.asarray(d % nf)][None, :]
    first = jnp.asarray((d % half) < nf)[None, :]
    cos, sin = jnp.cos(ang), jnp.sin(ang)
    return jnp.stack([cos, jnp.where(first, -sin, 0.0), jnp.where(first, 0.0, sin)])


def _na_bias_table(rpb):
    col = np.arange(GRID_W)
    cs = np.clip(col - NA_KW // 2, 0, GRID_W - NA_KW)
    col_mask = (col[None, :] >= cs[:, None]) & (col[None, :] < cs[:, None] + NA_KW)
    dc = np.clip(col[None, :] - col[:, None], -(NA_KW - 1), NA_KW - 1) + NA_KW - 1
    b = jnp.where(jnp.asarray(col_mask)[None, None], rpb[:, :, dc], NEG_INF)
    pair = jnp.concatenate([b[:, 0:2 * NA_KH - 2], b[:, 1:2 * NA_KH - 1]], axis=-1)
    return pair.transpose(1, 0, 2, 3).reshape(2 * NA_KH - 2, NA_HEADS * GRID_W, 2 * GRID_W)


def _cache_slab(c):
    b, h, p, d = c.shape
    return c.transpose(0, 2, 1, 3).reshape(b, p, h * d).astype(jnp.bfloat16)


def _attention_ctx(hp, gains, sink, bd_c):
    cb = lambda w, off: (lambda b: (b, off // w))
    q_w, kv_w = GROUP_WIDTH, SWA_KV_WIDTH
    full = lambda a: pl.BlockSpec(a.shape, lambda b: (0,) * a.ndim)
    cache_out = lambda h: pl.BlockSpec((None, h, SEQ, HEAD_DIM), lambda b: (b, 0, 0, 0))
    cache_shape = lambda h: jax.ShapeDtypeStruct((BATCH, h, SEQ, HEAD_DIM), jnp.float32)
    return pl.pallas_call(
        _ctx_attn_kernel,
        grid=(BATCH,),
        in_specs=[pl.BlockSpec((SEQ, q_w), cb(q_w, COL_B)),
                  pl.BlockSpec((SEQ, kv_w), cb(kv_w, COL_B + q_w)),
                  pl.BlockSpec((SEQ, kv_w), cb(kv_w, COL_B + q_w + kv_w)),
                  pl.BlockSpec((SEQ, q_w), cb(q_w, COL_C)),
                  pl.BlockSpec((SEQ, q_w), cb(q_w, COL_C + q_w)),
                  pl.BlockSpec((SEQ, q_w), cb(q_w, COL_C + 2 * q_w)),
                  full(gains), pl.BlockSpec(memory_space=pltpu.SMEM), full(bd_c)],
        out_specs=[pl.BlockSpec((SEQ, 2 * q_w), lambda b: (b, 0)),
                   cache_out(SWA_KV_HEADS), cache_out(SWA_KV_HEADS), cache_out(NA_HEADS), cache_out(NA_HEADS)],
        out_shape=[jax.ShapeDtypeStruct((N_TOK, 2 * q_w), jnp.float32),
                   cache_shape(SWA_KV_HEADS), cache_shape(SWA_KV_HEADS), cache_shape(NA_HEADS), cache_shape(NA_HEADS)],
        compiler_params=_cparams("arbitrary"),
        name="attn_ctx",
    )(hp, hp, hp, hp, hp, hp, gains, sink, bd_c)


def _attention_latent(kind, hp, ck, cv, table, gains, sink, bd_c, prev_out):
    n = DEC_SEQ
    row0 = N_PROMPT_TOK // n
    q_w = GROUP_WIDTH
    kv_w = SWA_KV_WIDTH if kind == 'swa' else q_w
    col = COL_B if kind == 'swa' else COL_C
    cb = lambda w, off: (lambda b: (row0 + b, off // w))
    full = lambda a: pl.BlockSpec(a.shape, lambda b: (0,) * a.ndim)
    in_specs = [pl.BlockSpec((n, q_w), cb(q_w, col)),
                pl.BlockSpec((n, kv_w), cb(kv_w, col + q_w)),
                pl.BlockSpec((n, kv_w), cb(kv_w, col + q_w + kv_w)),
                pl.BlockSpec((None, PAST_LEN, kv_w), lambda b: (b, 0, 0)),
                pl.BlockSpec((None, PAST_LEN, kv_w), lambda b: (b, 0, 0)),
                full(table), full(gains)]
    args = [hp, hp, hp, ck, cv, table, gains]
    if kind == 'swa':
        in_specs.append(pl.BlockSpec(memory_space=pltpu.SMEM))
        args.append(sink)
    in_specs += [full(bd_c), pl.BlockSpec(memory_space=pl.ANY)]
    args += [bd_c, prev_out]
    return pl.pallas_call(
        _swa_latent_kernel if kind == 'swa' else _na_latent_kernel,
        grid=(DEC_BATCH,),
        in_specs=in_specs,
        out_specs=pl.BlockSpec((n, q_w), lambda b: (row0 + b, 0 if kind == 'swa' else 1)),
        out_shape=jax.ShapeDtypeStruct((N_TOK, 2 * q_w), jnp.float32),
        scratch_shapes=[pltpu.VMEM((n, q_w), jnp.bfloat16), pltpu.VMEM((n, kv_w), jnp.bfloat16),
                        pltpu.VMEM((n, kv_w), jnp.bfloat16)],
        input_output_aliases={len(args) - 1: 0},
        compiler_params=_cparams("arbitrary"),
        name=kind + "_latent",
    )(*args)


def _rmsnorm(x, g):
    y = x * lax.rsqrt(jnp.mean(x * x, axis=-1, keepdims=True) + EPS)
    return y * g


def _l2norm(x):
    return x * lax.rsqrt(jnp.sum(x * x, axis=-1, keepdims=True) + EPS)


def _heads(t, n):
    b, s, _ = t.shape
    return t.reshape(b, s, n, -1).transpose(0, 2, 1, 3)


def _merge(t):
    b, h, s, d = t.shape
    return t.transpose(0, 2, 1, 3).reshape(b, s, h * d)


def _flip(t):
    return jnp.flip(t, axis=2)


def _rope_2d(x):
    n = x.shape[2]
    t = jnp.arange(n)
    rows = (t // GRID_W).astype(jnp.float32)
    cols = (t % GRID_W).astype(jnp.float32)
    half = HEAD_DIM // 2
    nf = half // 2
    inv = 1.0 / (ROPE_BASE ** (jnp.arange(nf, dtype=jnp.float32) / nf))

    def rot(xp, pos):
        ang = pos[:, None] * inv[None, :]
        cos, sin = jnp.cos(ang), jnp.sin(ang)
        x1, x2 = xp[..., :nf], xp[..., nf:]
        return jnp.concatenate([x1 * cos - x2 * sin, x1 * sin + x2 * cos], axis=-1)

    return jnp.concatenate([rot(x[..., :half], rows), rot(x[..., half:], cols)], axis=-1)


def _dwconv(x, w):
    k, c = w.shape
    return lax.conv_general_dilated(x, w[:, None, :], window_strides=(1,), padding=[(k // 2, k // 2)],
                                    dimension_numbers=('NWC', 'WIO', 'NWC'), feature_group_count=c)


def _gla_chunked(q, k, v, log_a, s0):
    b, h, n, dk = q.shape
    dv = v.shape[-1]
    L = GLA_CHUNK
    nc = n // L
    qf = q.reshape(b, h, nc, L, dk) * (dk ** -0.5)
    kf = k.reshape(b, h, nc, L, dk)
    vf = v.reshape(b, h, nc, L, dv)
    cum = jnp.cumsum(log_a.reshape(b, h, nc, L, dk), axis=3)
    last = cum[:, :, :, -1:, :]
    q_dec = qf * jnp.exp(cum)
    k_inv = kf * jnp.exp(-cum)
    k_end = kf * jnp.exp(last - cum)
    causal = jnp.tril(jnp.ones((L, L), bool))
    att = jnp.where(causal, jnp.einsum('bhcid,bhcjd->bhcij', q_dec, k_inv), 0.0)
    o_intra = jnp.einsum('bhcij,bhcjv->bhciv', att, vf)

    def step(s, xs):
        qd, ke, dl, vc, oi = xs
        o = oi + jnp.einsum('bhid,bhdv->bhiv', qd, s)
        s = dl[..., None] * s + jnp.einsum('bhjd,bhjv->bhdv', ke, vc)
        return s, o

    xs = tuple(jnp.moveaxis(t, 2, 0) for t in (q_dec, k_end, jnp.exp(last[:, :, :, 0, :]), vf, o_intra))
    s_fin, o = lax.scan(step, s0, xs)
    return jnp.moveaxis(o, 0, 2).reshape(b, h, n, dv), s_fin


def _gdn_chunked(q, k, v, g, beta, s0):
    b, h, n, dk = q.shape
    dv = v.shape[-1]
    L = GDN_CHUNK
    nc = n // L
    f32 = jnp.float32
    qf = q.reshape(b, h, nc, L, dk)
    kf = k.reshape(b, h, nc, L, dk)
    vf = v.reshape(b, h, nc, L, dv)
    bf = beta.reshape(b, h, nc, L)
    gc = jnp.cumsum(g.reshape(b, h, nc, L), axis=-1)
    incl = jnp.tril(jnp.ones((L, L), bool))
    strict = jnp.tril(jnp.ones((L, L), bool), -1)
    diff = gc[..., :, None] - gc[..., None, :]
    dec = jnp.where(incl, jnp.exp(jnp.where(incl, diff, 0.0)), 0.0)
    kk = jnp.einsum('bhcid,bhcjd->bhcij', kf, kf)
    a_mat = jnp.where(strict, bf[..., :, None] * dec * kk, 0.0)
    eye = jnp.eye(L, dtype=f32)
    t_inv = lax.linalg.triangular_solve(eye + a_mat, jnp.broadcast_to(eye, a_mat.shape),
                                        left_side=True, lower=True, unit_diagonal=True)
    w_v = jnp.einsum('bhcij,bhcjv->bhciv', t_inv, bf[..., None] * vf)
    w_k = jnp.einsum('bhcij,bhcjd->bhcid', t_inv, (bf * jnp.exp(gc))[..., None] * kf)
    qk = jnp.einsum('bhcid,bhcjd->bhcij', qf, kf) * dec
    q_dec = qf * jnp.exp(gc)[..., None]
    k_end = kf * jnp.exp(gc[..., -1:] - gc)[..., None]
    g_last = jnp.exp(gc[..., -1])

    def step(s, xs):
        qd, qkc, wv, wk, ke, gl = xs
        u = wv - jnp.einsum('bhid,bhdv->bhiv', wk, s)
        o = jnp.einsum('bhid,bhdv->bhiv', qd, s) + jnp.einsum('bhij,bhjv->bhiv', qkc, u)
        s = gl[..., None, None] * s + jnp.einsum('bhjd,bhjv->bhdv', ke, u)
        return s, o

    xs = tuple(jnp.moveaxis(t, 2, 0) for t in (q_dec, qk, w_v, w_k, k_end, g_last))
    s_fin, o = lax.scan(step, s0, xs)
    return jnp.moveaxis(o, 0, 2).reshape(b, h, n, dv), s_fin


def _ctx_attention(q, k, v, sink):
    b, h, s, d = q.shape
    kvh = k.shape[1]
    g = h // kvh
    nqb = s // CTX_BLOCK
    scale = d ** -0.5
    qb = jnp.moveaxis(q.reshape(b, kvh, g, nqb, CTX_BLOCK, d), 3, 0)

    def block(qi):
        logits = jnp.einsum('bkgqd,bksd->bkgqs', qi, k) * scale
        if sink is not None:
            sk = jnp.broadcast_to(sink.reshape(1, kvh, g, 1, 1), logits.shape[:-1] + (1,))
            logits = jnp.concatenate([logits, sk], axis=-1)
        p = jax.nn.softmax(logits, axis=-1)[..., :s]
        return jnp.einsum('bkgqs,bksd->bkgqd', p, v)

    out = lax.map(block, qb)
    return jnp.moveaxis(out, 0, 3).reshape(b, h, s, d)


def _swa_latent(q, k, v, ck, cv, sink):
    b, h, n, d = q.shape
    kvh = k.shape[1]
    g = h // kvh
    nb = n // SWA_BLOCK
    span = SWA_BLOCK + 2 * SWA_WINDOW
    scale = d ** -0.5
    qg = q.reshape(b, kvh, g, n, d)
    pad = ((0, 0), (0, 0), (SWA_WINDOW, SWA_WINDOW), (0, 0))
    kp = jnp.pad(k, pad)
    vp = jnp.pad(v, pad)
    sk = jnp.broadcast_to(sink.reshape(1, kvh, g, 1, 1), (b, kvh, g, SWA_BLOCK, 1))

    def block(i):
        start = i * SWA_BLOCK
        qi = lax.dynamic_slice_in_dim(qg, start, SWA_BLOCK, axis=3)
        ki = lax.dynamic_slice_in_dim(kp, start, span, axis=2)
        vi = lax.dynamic_slice_in_dim(vp, start, span, axis=2)
        qpos = start + jnp.arange(SWA_BLOCK)
        kpos = start - SWA_WINDOW + jnp.arange(span)
        mask = (jnp.abs(qpos[:, None] - kpos[None, :]) <= SWA_WINDOW) & (kpos[None, :] >= 0) & (kpos[None, :] < n)
        l_loc = jnp.where(mask, jnp.einsum('bkgqd,bksd->bkgqs', qi, ki) * scale, NEG_INF)
        l_ctx = jnp.einsum('bkgqd,bkpd->bkgqp', qi, ck) * scale
        p = jax.nn.softmax(jnp.concatenate([l_loc, l_ctx, sk], axis=-1), axis=-1)
        return (jnp.einsum('bkgqs,bksd->bkgqd', p[..., :span], vi)
                + jnp.einsum('bkgqp,bkpd->bkgqd', p[..., span:-1], cv))

    out = lax.map(block, jnp.arange(nb))
    return out.transpose(1, 2, 3, 0, 4, 5).reshape(b, h, n, d)


def _na_latent(q, k, v, ck, cv, rpb):
    b, h, n, d = q.shape
    rows = n // GRID_W
    kh = min(NA_KH, rows)
    scale = d ** -0.5
    qg = q.reshape(b, h, rows, GRID_W, d)
    kg = k.reshape(b, h, rows, GRID_W, d)
    vg = v.reshape(b, h, rows, GRID_W, d)
    col = jnp.arange(GRID_W)
    cs = jnp.clip(col - NA_KW // 2, 0, GRID_W - NA_KW)
    col_mask = (col[None, :] >= cs[:, None]) & (col[None, :] < cs[:, None] + NA_KW)
    dc = jnp.clip(col[None, :] - col[:, None], -(NA_KW - 1), NA_KW - 1) + NA_KW - 1
    rpb_c = rpb[:, :, dc]
    nloc = kh * GRID_W

    def row(r):
        rs = jnp.clip(r - kh // 2, 0, rows - kh)
        qr = lax.dynamic_index_in_dim(qg, r, axis=2, keepdims=False)
        kr = lax.dynamic_slice_in_dim(kg, rs, kh, axis=2)
        vr = lax.dynamic_slice_in_dim(vg, rs, kh, axis=2)
        dr = rs + jnp.arange(kh) - r + NA_KH - 1
        bias = jnp.take(rpb_c, dr, axis=1).transpose(0, 2, 1, 3)
        l_loc = jnp.einsum('bhqd,bhrkd->bhqrk', qr, kr) * scale + bias[None]
        l_loc = jnp.where(col_mask[:, None, :], l_loc, NEG_INF).reshape(b, h, GRID_W, nloc)
        l_ctx = jnp.einsum('bhqd,bhpd->bhqp', qr, ck) * scale
        p = jax.nn.softmax(jnp.concatenate([l_loc, l_ctx], axis=-1), axis=-1)
        p_loc = p[..., :nloc].reshape(b, h, GRID_W, kh, GRID_W)
        return (jnp.einsum('bhqrk,bhrkd->bhqd', p_loc, vr)
                + jnp.einsum('bhqp,bhpd->bhqd', p[..., nloc:], cv))

    out = lax.map(row, jnp.arange(rows))
    return out.transpose(1, 2, 0, 3, 4).reshape(b, h, n, d)


def _decay(a, alog, dtb):
    g = -jnp.exp(alog) * jax.nn.softplus(a + dtb)
    return g.transpose(0, 2, 1)


def _attention_mixers(hp, p, cache):
    latent = cache is not None
    s_q = hp[..., COL_B:COL_B + GROUP_WIDTH]
    s_k = hp[..., COL_B + GROUP_WIDTH:COL_B + GROUP_WIDTH + SWA_KV_WIDTH]
    s_v = hp[..., COL_B + GROUP_WIDTH + SWA_KV_WIDTH:COL_B + GROUP_WIDTH + 2 * SWA_KV_WIDTH]
    n_q, n_k, n_v = (hp[..., COL_C + i * GROUP_WIDTH:COL_C + (i + 1) * GROUP_WIDTH] for i in range(3))

    q = _rmsnorm(_heads(s_q, SWA_HEADS), p['swa_qnorm'])
    swa_k = _rmsnorm(_heads(s_k, SWA_KV_HEADS), p['swa_knorm'])
    swa_v = _heads(s_v, SWA_KV_HEADS)
    if latent:
        out_b = _swa_latent(_rope_2d(q), _rope_2d(swa_k), swa_v, cache['swa_k'], cache['swa_v'], p['swa_sink'])
    else:
        out_b = _ctx_attention(q, swa_k, swa_v, p['swa_sink'])

    q = _rmsnorm(_heads(n_q, NA_HEADS), p['na_qnorm'])
    na_k = _rmsnorm(_heads(n_k, NA_HEADS), p['na_knorm'])
    na_v = _heads(n_v, NA_HEADS)
    if latent:
        out_c = _na_latent(q, na_k, na_v, cache['na_k'], cache['na_v'], p['na_rpb'])
    else:
        out_c = _ctx_attention(q, na_k, na_v, None)

    y = jnp.concatenate([_merge(out_b), _merge(out_c)], axis=-1)
    if latent:
        return y, None
    return y, (swa_k, swa_v, na_k, na_v)


def kernel(x_prompt, x_sample, cache_swa_k, cache_swa_v, cache_na_k, cache_na_v, state_gla, state_gdn, c, c_ctx, w_mod, b_mod, norm1_g, norm2_g, w_in, w_out, gla_wup, gla_bup, gla_onorm, swa_qnorm, swa_knorm, swa_sink, na_qnorm, na_knorm, na_rpb, gdn_conv, gdn_alog, gdn_dtbias, gdn_onorm, ffn_w1, ffn_w3, ffn_w2, moe_router, moe_w1, moe_w3, moe_w2):
    bf16 = jnp.bfloat16
    x = jnp.concatenate([x_prompt.reshape(N_PROMPT_TOK, D_MODEL), x_sample.reshape(N_SAMPLE_TOK, D_MODEL)], axis=0)
    cvec = jnp.concatenate([c_ctx[None, :], c, jnp.zeros((N_MOD_ROWS - 1 - DEC_BATCH, D_MODEL), jnp.float32)], axis=0)
    mod_all = _modulation_all(cvec, w_mod, b_mod).reshape(DEPTH, N_MOD_ROWS, 1, MOD_W)
    w_in_b = jnp.pad(jnp.take(w_in, jnp.asarray(_IN_COL_PERM), axis=2),
                     ((0, 0), (0, 0), (0, IN_COLS_PAD - IN_COLS))).astype(bf16)
    w_out_b = jnp.take(w_out, jnp.asarray(_OUT_ROW_PERM), axis=1).astype(bf16)
    rope_tables = _rope_tables()
    consts = _linear_consts()
    tile_heads = lambda g: jnp.tile(g, N_SLAB_HEADS)[None, :]
    per_head = lambda a: jnp.repeat(a, HEAD_DIM, axis=-1)
    small_expand = jnp.asarray(_small_expand_matrix())

    ctx_states = []
    for l in range(DEPTH):
        mod = mod_all[l]
        hp = _in_proj(x, mod, norm1_g[l][None, :], w_in_b[l])

        wup = jnp.zeros((2, LANES, SLAB), jnp.float32)
        wup = wup.at[0, 0:GLA_LOWRANK].set(gla_wup[l, 0]).at[1, GLA_LOWRANK:2 * GLA_LOWRANK].set(gla_wup[l, 1])
        gla_params = (wup, gla_bup[l], tile_heads(gla_onorm[l]))
        out_a, st_a = _linear_mixer('gla', hp, COL_A // (4 * SLAB), None, gla_params, consts, None)
        out_a, _ = _linear_mixer('gla', hp, COL_A // (4 * SLAB), _to_blockdiag(state_gla[:, l], True),
                                 gla_params, consts, out_a)
        gdn_params = (gdn_conv[l], jnp.concatenate([per_head(gdn_alog[l]), per_head(gdn_dtbias[l])], axis=0),
                      tile_heads(gdn_onorm[l]), small_expand)
        out_d, st_d = _linear_mixer('gdn', hp, COL_D // (4 * SLAB), None, gdn_params, consts, None)
        out_d, _ = _linear_mixer('gdn', hp, COL_D // (4 * SLAB), _to_blockdiag(state_gdn[:, l], False),
                                 gdn_params, consts, out_d)

        gains = jnp.stack([jnp.tile(g, N_SLAB_HEADS) for g in (swa_qnorm[l], swa_knorm[l], na_qnorm[l], na_knorm[l])])
        sink = swa_sink[l][jnp.asarray(SWA_Q_HEAD_ORDER)]
        att, swa_k, swa_v, na_k, na_v = _attention_ctx(hp, gains, sink, consts[1])
        att = _attention_latent('swa', hp, _cache_slab(cache_swa_k[:, l]), _cache_slab(cache_swa_v[:, l]),
                                rope_tables, gains, sink, consts[1], att)
        att = _attention_latent('na', hp, _cache_slab(cache_na_k[:, l]), _cache_slab(cache_na_v[:, l]),
                                _na_bias_table(na_rpb[l]), gains, None, consts[1], att)
        st = (swa_k, swa_v, na_k, na_v, _from_blockdiag(st_a, True), _from_blockdiag(st_d, False))
        x = _out_proj(out_a, att, out_d, x, mod, w_out_b[l])
        if l % 2 == 0:
            x = _ffn(x, mod, norm2_g[l][None, :], ffn_w1[l // 2].astype(bf16), ffn_w3[l // 2].astype(bf16),
                     ffn_w2[l // 2].astype(bf16))
        else:
            x = _moe(x, mod, norm2_g[l][None, :], moe_router[l // 2], moe_w1[l // 2].astype(bf16),
                     moe_w3[l // 2].astype(bf16), moe_w2[l // 2].astype(bf16))
        ctx_states.append(st)

    outs = [jnp.stack([st[j] for st in ctx_states], axis=1) for j in range(6)]
    y_prompt = x[:N_PROMPT_TOK].reshape(BATCH, SEQ, D_MODEL)
    y_sample = x[N_PROMPT_TOK:].reshape(DEC_BATCH, DEC_SEQ, D_MODEL)
    return (y_prompt, y_sample, *outs)
```

```python
import functools
import math

import numpy as np
import jax
import jax.numpy as jnp
from jax import lax
from jax.experimental import pallas as pl
from jax.experimental.pallas import tpu as pltpu
from jax.experimental.pallas import tpu_sc as plsc

D_MODEL = 1024
BATCH = 32
SEQ = 256
DEPTH = 4
DEC_BATCH = 8
DEC_SEQ = 1024
PAST_LEN = 512

GRID_W = 64
HEAD_DIM = 64
GROUP_WIDTH = D_MODEL // 4
GROUP_HEADS = GROUP_WIDTH // HEAD_DIM
GLA_HEADS = GROUP_HEADS
GLA_LOWRANK = 16
GLA_TAU = 16.0
GLA_CHUNK = 64
SWA_HEADS = GROUP_HEADS
SWA_KV_HEADS = 2
SWA_WINDOW = 128
SWA_BLOCK = 128
NA_HEADS = GROUP_HEADS
NA_KH = 8
NA_KW = 16
GDN_HEADS = GROUP_HEADS
GDN_CONV = 3
GDN_CHUNK = 64
D_FF = 2816
N_EXPERTS = 8
TOP_K = 2
D_FF_EXPERT = 1024
ROPE_BASE = 10000.0
EPS = 1e-6
NEG_INF = -1e30
SWA_KV_WIDTH = SWA_KV_HEADS * HEAD_DIM
IN_SPLITS = ([GROUP_WIDTH] * 4 + [GLA_LOWRANK] * 2 + [GROUP_WIDTH, SWA_KV_WIDTH, SWA_KV_WIDTH]
             + [GROUP_WIDTH] * 3 + [GROUP_WIDTH] * 4 + [GDN_HEADS] * 4)
IN_COLS = sum(IN_SPLITS)

LANES = 128
VMEM_LIMIT_BYTES = 56 * 1024 * 1024

N_PROMPT_TOK = BATCH * SEQ
N_SAMPLE_TOK = DEC_BATCH * DEC_SEQ
N_TOK = N_PROMPT_TOK + N_SAMPLE_TOK
N_MOD_ROWS = 16
MOD_W = 6 * D_MODEL
TM = 512
IN_COLS_PAD = -(-IN_COLS // LANES) * LANES

COL_A = 0
COL_D = 4 * GROUP_WIDTH
COL_C = 8 * GROUP_WIDTH
COL_B = 11 * GROUP_WIDTH
COL_SMALL = COL_B + GROUP_WIDTH + 2 * SWA_KV_WIDTH
SMALL_COL_BLOCK = COL_SMALL // LANES
SMALL_GDN = 2 * GLA_LOWRANK


SWA_Q_HEAD_ORDER = (0, 2, 1, 3)
def _in_col_segments():
    off = [0] + [int(v) for v in np.cumsum(IN_SPLITS)]
    seg = lambda a, b: [(off[a], off[b])]
    swa_q = [(off[6] + h * HEAD_DIM, off[6] + (h + 1) * HEAD_DIM) for h in SWA_Q_HEAD_ORDER]
    return seg(0, 4) + seg(12, 16) + seg(9, 12) + swa_q + seg(7, 9) + seg(4, 6) + seg(16, 20)


_OUT_ROW_SEGMENTS = ([(0, GROUP_WIDTH)]
                     + [(GROUP_WIDTH + h * HEAD_DIM, GROUP_WIDTH + (h + 1) * HEAD_DIM) for h in SWA_Q_HEAD_ORDER]
                     + [(2 * GROUP_WIDTH, D_MODEL)])


def _take_segments(w, segments, axis):
    idx = [slice(None)] * w.ndim
    parts = []
    for a, b in segments:
        idx[axis] = slice(a, b)
        parts.append(w[tuple(idx)])
    return jnp.concatenate(parts, axis=axis)


def _small_expand_matrix():
    e = np.zeros((LANES, 4 * GROUP_WIDTH), np.float32)
    for s in range(4):
        for h in range(GROUP_HEADS):
            e[SMALL_GDN + s * GROUP_HEADS + h, s * GROUP_WIDTH + h * HEAD_DIM:s * GROUP_WIDTH + (h + 1) * HEAD_DIM] = 1.0
    return e


def _mod_row(i):
    n_prompt_tiles = N_PROMPT_TOK // TM
    return jnp.where(i < n_prompt_tiles, 0, 1 + (i - n_prompt_tiles) // (DEC_SEQ // TM))


def _cparams(*sem):
    return pltpu.CompilerParams(dimension_semantics=sem, vmem_limit_bytes=VMEM_LIMIT_BYTES)


def _mod_kernel(c_ref, w_ref, b_ref, o_ref):
    c = c_ref[...]
    s = c * jax.nn.sigmoid(c)
    o_ref[...] = jnp.dot(s, w_ref[...], preferred_element_type=jnp.float32,
                         precision=lax.Precision.HIGHEST) + b_ref[...]


def _modulation_all(cvec, w_mod, b_mod):
    tn = 1536
    return pl.pallas_call(
        _mod_kernel,
        grid=(DEPTH, MOD_W // tn),
        in_specs=[pl.BlockSpec((N_MOD_ROWS, D_MODEL), lambda l, j: (0, 0)),
                  pl.BlockSpec((None, D_MODEL, tn), lambda l, j: (l, 0, j)),
                  pl.BlockSpec((None, 1, tn), lambda l, j: (l, 0, j))],
        out_specs=pl.BlockSpec((None, N_MOD_ROWS, tn), lambda l, j: (l, 0, j)),
        out_shape=jax.ShapeDtypeStruct((DEPTH, N_MOD_ROWS, MOD_W), jnp.float32),
        compiler_params=_cparams("arbitrary", "arbitrary"),
        name="modulation",
    )(cvec, w_mod, b_mod.reshape(DEPTH, 1, MOD_W))


def _modulated_norm(x, g, shift, scale):
    y = x * lax.rsqrt(jnp.mean(x * x, axis=-1, keepdims=True) + EPS) * g
    return y * (1.0 + scale) + shift


def _in_proj_kernel(x_ref, mod_ref, g_ref, w_ref, o_ref):
    h = _modulated_norm(x_ref[...], g_ref[...], mod_ref[:, 0:D_MODEL], mod_ref[:, D_MODEL:2 * D_MODEL])
    o_ref[...] = jnp.dot(h.astype(jnp.bfloat16), w_ref[...], preferred_element_type=jnp.float32)


def _in_proj(layer, x, mod, g, w):
    return pl.pallas_call(
        _in_proj_kernel,
        grid=(N_TOK // TM,),
        in_specs=[pl.BlockSpec((TM, D_MODEL), lambda i: (i, 0)),
                  pl.BlockSpec((None, 1, MOD_W), lambda i: (_mod_row(i), 0, 0)),
                  pl.BlockSpec((1, D_MODEL), lambda i: (0, 0)),
                  pl.BlockSpec((None, D_MODEL, IN_COLS_PAD), lambda i: (layer, 0, 0),
                               pipeline_mode=pl.Buffered(1))],
        out_specs=pl.BlockSpec((TM, IN_COLS_PAD), lambda i: (i, 0)),
        out_shape=jax.ShapeDtypeStruct((N_TOK, IN_COLS_PAD), jnp.float32),
        compiler_params=_cparams("arbitrary"),
        name="in_proj",
    )(x, mod, g, w)


N_CTX_TILES = N_PROMPT_TOK // TM


def _out_proj_kernel(a_c, b_c, d_c, a_l, b_l, c_l, d_l, x_ref, mod_ref, w_ref, o_ref):
    is_ctx = pl.program_id(0) < N_CTX_TILES
    ctx = jnp.concatenate([a_c[...], b_c[...], d_c[...]], axis=1)
    lat = jnp.concatenate([a_l[...], b_l[...], c_l[...], d_l[...]], axis=1)
    mix = jnp.where(is_ctx, ctx, lat)
    y = jnp.dot(mix.astype(jnp.bfloat16), w_ref[...], preferred_element_type=jnp.float32)
    o_ref[...] = x_ref[...] + mod_ref[:, 2 * D_MODEL:3 * D_MODEL] * y


def _out_proj(layer, ctx_outs, lat_outs, x, mod, w):
    ctx_map = lambda i: (jnp.minimum(i, N_CTX_TILES - 1), 0)
    lat_map = lambda i: (jnp.maximum(i - N_CTX_TILES, 0), 0)
    return pl.pallas_call(
        _out_proj_kernel,
        grid=(N_TOK // TM,),
        in_specs=[pl.BlockSpec((TM, a.shape[1]), ctx_map) for a in ctx_outs]
                 + [pl.BlockSpec((TM, a.shape[1]), lat_map) for a in lat_outs]
                 + [pl.BlockSpec((TM, D_MODEL), lambda i: (i, 0)),
                    pl.BlockSpec((None, 1, MOD_W), lambda i: (_mod_row(i), 0, 0)),
                    pl.BlockSpec((None, D_MODEL, D_MODEL), lambda i: (layer, 0, 0), pipeline_mode=pl.Buffered(1))],
        out_specs=pl.BlockSpec((TM, D_MODEL), lambda i: (i, 0)),
        out_shape=jax.ShapeDtypeStruct((N_TOK, D_MODEL), jnp.float32),
        compiler_params=_cparams("arbitrary"),
        name="out_proj",
    )(*ctx_outs, *lat_outs, x, mod, w)


FF_CHUNK = D_FF // 2


def _ffn_kernel(x_ref, mod_ref, g_ref, w1_ref, w3_ref, w2_ref, o_ref):
    x = x_ref[...]
    h = _modulated_norm(x, g_ref[...], mod_ref[:, 3 * D_MODEL:4 * D_MODEL], mod_ref[:, 4 * D_MODEL:5 * D_MODEL])
    hb = h.astype(jnp.bfloat16)
    y = jnp.zeros((TM, D_MODEL), jnp.float32)
    for c0 in range(0, D_FF, FF_CHUNK):
        a = jnp.dot(hb, w1_ref[:, c0:c0 + FF_CHUNK], preferred_element_type=jnp.float32)
        b = jnp.dot(hb, w3_ref[:, c0:c0 + FF_CHUNK], preferred_element_type=jnp.float32)
        s = (a * jax.nn.sigmoid(a) * b).astype(jnp.bfloat16)
        y = y + jnp.dot(s, w2_ref[c0:c0 + FF_CHUNK, :], preferred_element_type=jnp.float32)
    o_ref[...] = x + mod_ref[:, 5 * D_MODEL:6 * D_MODEL] * y


def _ffn(idx, x, mod, g, w1, w3, w2):
    resident = dict(pipeline_mode=pl.Buffered(1))
    return pl.pallas_call(
        _ffn_kernel,
        grid=(N_TOK // TM,),
        in_specs=[pl.BlockSpec((TM, D_MODEL), lambda i: (i, 0)),
                  pl.BlockSpec((None, 1, MOD_W), lambda i: (_mod_row(i), 0, 0)),
                  pl.BlockSpec((1, D_MODEL), lambda i: (0, 0)),
                  pl.BlockSpec((None, D_MODEL, D_FF), lambda i: (idx, 0, 0), **resident),
                  pl.BlockSpec((None, D_MODEL, D_FF), lambda i: (idx, 0, 0), **resident),
                  pl.BlockSpec((None, D_FF, D_MODEL), lambda i: (idx, 0, 0), **resident)],
        out_specs=pl.BlockSpec((TM, D_MODEL), lambda i: (i, 0)),
        out_shape=jax.ShapeDtypeStruct((N_TOK, D_MODEL), jnp.float32),
        compiler_params=_cparams("arbitrary"),
        name="ffn_dense",
    )(x, mod, g, w1, w3, w2)


TMOE = 512
MOE_ROWS = TOP_K * N_TOK + N_EXPERTS * TMOE
MOE_TILES = MOE_ROWS // TMOE
PACK_W = D_MODEL // 2
SC_WINDOW = 64


def _pack_bf16_pairs(a):
    w = a.shape[1] // 2
    bits = lax.bitcast_convert_type(a.astype(jnp.bfloat16).astype(jnp.float32), jnp.uint32)
    return (bits[:, w:] & jnp.uint32(0xFFFF0000)) | (bits[:, 0:w] >> 16)


def _unpack_bf16_pairs(p):
    lo = lax.bitcast_convert_type(p << 16, jnp.float32)
    hi = lax.bitcast_convert_type(p & jnp.uint32(0xFFFF0000), jnp.float32)
    return jnp.concatenate([lo, hi], axis=1)


def _route_kernel(x_ref, mod_ref, g_ref, r_ref, h_ref, gate_ref, sel_ref):
    h = _modulated_norm(x_ref[...], g_ref[...], mod_ref[:, 3 * D_MODEL:4 * D_MODEL],
                        mod_ref[:, 4 * D_MODEL:5 * D_MODEL])
    h_ref[...] = _pack_bf16_pairs(h)
    logits = _mm_x3(h, r_ref[...])
    lane = lax.broadcasted_iota(jnp.int32, logits.shape, 1)
    v1 = jnp.max(logits, axis=-1, keepdims=True)
    i1 = jnp.min(jnp.where(logits == v1, lane, N_EXPERTS), axis=-1, keepdims=True)
    rest = jnp.where(lane == i1, -jnp.inf, logits)
    v2 = jnp.max(rest, axis=-1, keepdims=True)
    i2 = jnp.min(jnp.where(rest == v2, lane, N_EXPERTS), axis=-1, keepdims=True)
    p2 = jnp.exp(v2 - v1)
    den = 1.0 + p2
    gate_ref[...] = jnp.where(lane == i1, 1.0 / den, 0.0) + jnp.where(lane == i2, p2 / den, 0.0)
    sel_ref[...] = jnp.where((lane == i1) | (lane == i2), 1, 0)


def _route(x, mod, g, router):
    return pl.pallas_call(
        _route_kernel,
        grid=(N_TOK // TM,),
        in_specs=[pl.BlockSpec((TM, D_MODEL), lambda i: (i, 0)),
                  pl.BlockSpec((None, 1, MOD_W), lambda i: (_mod_row(i), 0, 0)),
                  pl.BlockSpec((1, D_MODEL), lambda i: (0, 0)),
                  pl.BlockSpec((D_MODEL, N_EXPERTS), lambda i: (0, 0))],
        out_specs=[pl.BlockSpec((TM, PACK_W), lambda i: (i, 0)),
                   pl.BlockSpec((TM, N_EXPERTS), lambda i: (i, 0)),
                   pl.BlockSpec((TM, N_EXPERTS), lambda i: (i, 0))],
        out_shape=[jax.ShapeDtypeStruct((N_TOK, PACK_W), jnp.uint32),
                   jax.ShapeDtypeStruct((N_TOK, N_EXPERTS), jnp.float32),
                   jax.ShapeDtypeStruct((N_TOK, N_EXPERTS), jnp.int32)],
        compiler_params=_cparams("arbitrary"),
        name="moe_route",
    )(x, mod, g, router)


def _dispatch_plan(gates, sel):
    rank = jnp.cumsum(sel, axis=0) - sel
    count = jnp.sum(sel, axis=0)
    padded = -(-count // TMOE) * TMOE
    seg_end = jnp.cumsum(padded)
    pos = (seg_end - padded)[None, :] + rank
    chosen = sel > 0
    pos_lo = jnp.min(jnp.where(chosen, pos, MOE_ROWS), axis=1)
    pos_hi = jnp.max(jnp.where(chosen, pos, -1), axis=1)
    w_lo = jnp.sum(jnp.where(chosen & (pos == pos_lo[:, None]), gates, 0.0), axis=1)
    w_hi = jnp.sum(jnp.where(chosen & (pos == pos_hi[:, None]), gates, 0.0), axis=1)
    fill = jnp.arange(TMOE, dtype=jnp.int32)[None, :]
    pad_dest = jnp.where(fill < (padded - count)[:, None], (seg_end - padded + count)[:, None] + fill, MOE_ROWS)
    dest = jnp.concatenate([pos_lo, pos_hi, pad_dest.reshape(-1)]).astype(jnp.int32)
    tile_row = jnp.arange(MOE_TILES, dtype=jnp.int32) * TMOE
    tile_expert = jnp.minimum(jnp.sum(seg_end[None, :] <= tile_row[:, None], axis=1), N_EXPERTS - 1)
    return (dest, jnp.stack([pos_lo, pos_hi]).astype(jnp.int32), jnp.stack([w_lo, w_hi], axis=1),
            tile_expert.astype(jnp.int32), (seg_end[-1:] // TMOE).astype(jnp.int32))


def _scatter_rows(table, dest):
    k, m = dest.shape[0], table.shape[0]
    mesh = plsc.VectorSubcoreMesh(core_axis_name="core", subcore_axis_name="subcore")

    @functools.partial(pl.kernel, out_type=jax.ShapeDtypeStruct((MOE_ROWS + SC_WINDOW, PACK_W), table.dtype),
                       mesh=mesh, scratch_types=[])
    def scatter(x_hbm, i_hbm, o_hbm):
        def body(x_vmem, i_vmem):
            pltpu.sync_copy(x_vmem, o_hbm.at[i_vmem.at[0, pl.ds(0, SC_WINDOW)]])

        pltpu.emit_pipeline(
            body,
            grid=(k // SC_WINDOW,),
            in_specs=[pl.BlockSpec((SC_WINDOW, PACK_W), index_map=lambda i: (i % (m // SC_WINDOW), 0)),
                      pl.BlockSpec((1, 2 * SC_WINDOW), index_map=lambda i: (i, 0))],
            out_specs=[],
            core_axis_name=('core', 'subcore'),
            dimension_semantics=(pltpu.PARALLEL,),
        )(x_hbm, i_hbm)

    idx_rows = dest.reshape(k // SC_WINDOW, SC_WINDOW)
    return scatter(table, jnp.concatenate([idx_rows, idx_rows], axis=1))


def _gather_rows(table, idx):
    k = idx.shape[0]
    mesh = plsc.VectorSubcoreMesh(core_axis_name="core", subcore_axis_name="subcore")

    @functools.partial(pl.kernel, out_type=jax.ShapeDtypeStruct((k, PACK_W), table.dtype), mesh=mesh)
    def gather(x_hbm, i_hbm, o_hbm):
        def body(i_vmem, o_vmem):
            pltpu.sync_copy(x_hbm.at[i_vmem.at[0, pl.ds(0, SC_WINDOW)]], o_vmem)

        pltpu.emit_pipeline(
            body,
            grid=(k // SC_WINDOW,),
            in_specs=[pl.BlockSpec((1, 2 * SC_WINDOW), index_map=lambda i: (i, 0))],
            out_specs=[pl.BlockSpec((SC_WINDOW, PACK_W), index_map=lambda i: (i, 0))],
            core_axis_name=('core', 'subcore'),
            dimension_semantics=(pltpu.PARALLEL,),
        )(i_hbm, o_hbm)

    idx_rows = idx.reshape(k // SC_WINDOW, SC_WINDOW)
    return gather(table, jnp.concatenate([idx_rows, idx_rows], axis=1))


def _expert_kernel(te_ref, nu_ref, h_ref, w1_ref, w3_ref, w2_ref, y_ref, w1_s, w3_s, w2_s):
    j = pl.program_id(0)

    @pl.when((j == 0) | (te_ref[j] != te_ref[jnp.maximum(j - 1, 0)]))
    def _():
        w1_s[...] = w1_ref[...].astype(jnp.bfloat16)
        w3_s[...] = w3_ref[...].astype(jnp.bfloat16)
        w2_s[...] = w2_ref[...].astype(jnp.bfloat16)

    @pl.when(j < nu_ref[0])
    def _():
        hb = _unpack_bf16_pairs(h_ref[...]).astype(jnp.bfloat16)
        a = jnp.dot(hb, w1_s[...], preferred_element_type=jnp.float32)
        b = jnp.dot(hb, w3_s[...], preferred_element_type=jnp.float32)
        s = (a * jax.nn.sigmoid(a) * b).astype(jnp.bfloat16)
        y_ref[...] = _pack_bf16_pairs(jnp.dot(s, w2_s[...], preferred_element_type=jnp.float32))

    @pl.when(pl.program_id(0) >= nu_ref[0])
    def _():
        y_ref[...] = jnp.zeros_like(y_ref)


def _experts(idx, tile_expert, n_used, hs, w1, w3, w2):
    wspec = lambda shape: pl.BlockSpec((None, None) + shape, lambda j, te, nu: (idx, te[j], 0, 0))
    return pl.pallas_call(
        _expert_kernel,
        grid_spec=pltpu.PrefetchScalarGridSpec(
            num_scalar_prefetch=2,
            grid=(MOE_TILES,),
            in_specs=[pl.BlockSpec((TMOE, PACK_W), lambda j, te, nu: (j, 0)),
                      wspec((D_MODEL, D_FF_EXPERT)), wspec((D_MODEL, D_FF_EXPERT)), wspec((D_FF_EXPERT, D_MODEL))],
            out_specs=pl.BlockSpec((TMOE, PACK_W), lambda j, te, nu: (j, 0)),
            scratch_shapes=[pltpu.VMEM((D_MODEL, D_FF_EXPERT), jnp.bfloat16),
                            pltpu.VMEM((D_MODEL, D_FF_EXPERT), jnp.bfloat16),
                            pltpu.VMEM((D_FF_EXPERT, D_MODEL), jnp.bfloat16)]),
        out_shape=jax.ShapeDtypeStruct((MOE_ROWS, PACK_W), jnp.uint32),
        compiler_params=_cparams("arbitrary"),
        name="moe_experts",
    )(tile_expert, n_used, hs, w1, w3, w2)


def _combine_kernel(x_ref, mod_ref, ylo_ref, yhi_ref, w_ref, o_ref):
    y = w_ref[:, 0:1] * _unpack_bf16_pairs(ylo_ref[...]) + w_ref[:, 1:2] * _unpack_bf16_pairs(yhi_ref[...])
    o_ref[...] = x_ref[...] + mod_ref[:, 5 * D_MODEL:6 * D_MODEL] * y


def _combine(x, mod, y2, w):
    n_tiles = N_TOK // TM
    return pl.pallas_call(
        _combine_kernel,
        grid=(n_tiles,),
        in_specs=[pl.BlockSpec((TM, D_MODEL), lambda i: (i, 0)),
                  pl.BlockSpec((None, 1, MOD_W), lambda i: (_mod_row(i), 0, 0)),
                  pl.BlockSpec((TM, PACK_W), lambda i: (i, 0)),
                  pl.BlockSpec((TM, PACK_W), lambda i: (n_tiles + i, 0)),
                  pl.BlockSpec((TM, TOP_K), lambda i: (i, 0))],
        out_specs=pl.BlockSpec((TM, D_MODEL), lambda i: (i, 0)),
        out_shape=jax.ShapeDtypeStruct((N_TOK, D_MODEL), jnp.float32),
        compiler_params=_cparams("arbitrary"),
        name="moe_combine",
    )(x, mod, y2, y2, w)


def _moe_routed(idx, x, mod, g, router, w1, w3, w2):
    hp, gates, sel = _route(x, mod, g, router)
    dest, pos, w, tile_expert, n_used = _dispatch_plan(gates, sel)
    ys = _experts(idx, tile_expert, n_used, _scatter_rows(hp, dest), w1, w3, w2)
    return _combine(x, mod, _gather_rows(ys, pos.reshape(-1)), w)


CH = GLA_CHUNK
SLAB = GROUP_WIDTH
N_SLAB_HEADS = SLAB // HEAD_DIM
PREP_CHUNKS = 4
_HI = lax.Precision.HIGHEST


def _linear_consts():
    i = np.arange(CH)[:, None]
    j = (np.arange(SLAB) % CH)[None, :]
    slab = np.stack([i >= j, i > j, i <= j, i < j, i == j]).astype(np.float32)
    r = np.arange(SLAB) // CH
    bd = (r[:, None] == r[None, :]).astype(np.float32)
    t = np.arange(CH)
    tri = np.stack([t[:, None] >= t[None, :], t[:, None] <= t[None, :]]).astype(np.float32)
    return jnp.asarray(slab), jnp.asarray(bd), jnp.asarray(tri, dtype=jnp.bfloat16)


def _bf(x):
    return x.astype(jnp.bfloat16)


def _split2(x):
    hi = _bf(x)
    return hi, _bf(x - hi.astype(jnp.float32))


def _split3_lanes(x):
    hi = _bf(x)
    r1 = x - hi.astype(jnp.float32)
    mid = _bf(r1)
    lo = _bf(r1 - mid.astype(jnp.float32))
    return jnp.concatenate([hi, mid, lo], axis=1)


def _mm(a, b):
    return jnp.dot(a, b, preferred_element_type=jnp.float32)


def _mm_nt(a, b):
    return lax.dot_general(a, b, (((1,), (1,)), ((), ())), preferred_element_type=jnp.float32)


def _mm_tn(a, b):
    return lax.dot_general(a, b, (((0,), (0,)), ((), ())), preferred_element_type=jnp.float32)


def _blockdiag(y_b, bd_b):
    return jnp.concatenate([y_b] * N_SLAB_HEADS, axis=0) * bd_b


def _exact_rows_mm(lhs_b, x):
    c = _mm(lhs_b, _split3_lanes(x))
    return c[:, 0:SLAB] + c[:, SLAB:2 * SLAB] + c[:, 2 * SLAB:3 * SLAB]


def _head_mm3(lhs, y, bd_b):
    m = lhs.shape[0]
    lh, ll = _split2(lhs)
    yh, yl = _split2(y)
    a = _mm(jnp.concatenate([lh, ll], axis=0), _blockdiag(yh, bd_b))
    return a[0:m] + a[m:2 * m] + _mm(lh, _blockdiag(yl, bd_b))


def _unit_triangular_inverses(nmats, eye, bd_b):
    ts = [eye + n for n in nmats]
    ps = [_head_mm3(n, n, bd_b) for n in nmats]
    n_doublings = int(math.log2(CH)) - 1
    for it in range(n_doublings):
        if it < n_doublings - 1:
            res = [_head_mm3(jnp.concatenate([t, p], axis=0), p, bd_b) for t, p in zip(ts, ps)]
            ts = [t + r[0:CH] for t, r in zip(ts, res)]
            ps = [r[CH:2 * CH] for r in res]
        else:
            ts = [t + _head_mm3(t, p, bd_b) for t, p in zip(ts, ps)]
    return ts


def _group_sum(x, bd_f):
    m, w = x.shape
    hi, lo = _split2(x)
    s = _mm(jnp.concatenate([hi, lo], axis=0), _bf(bd_f[0:w, 0:w]))
    return s[0:m] + s[m:2 * m]


def _exact_cols_mm(x, rhs_b):
    m = x.shape[0]
    hi = _bf(x)
    r1 = x - hi.astype(jnp.float32)
    mid = _bf(r1)
    lo = _bf(r1 - mid.astype(jnp.float32))
    s = _mm(jnp.concatenate([hi, mid, lo], axis=0), rhs_b)
    return s[0:m] + s[m:2 * m] + s[2 * m:3 * m]


def _mm_x3(x, w):
    m = x.shape[0]
    xh, xl = _split2(x)
    wh, wl = _split2(w)
    s = _mm(jnp.concatenate([xh, xl], axis=0), wh)
    return s[0:m] + s[m:2 * m] + _mm(xh, wl)


def _softplus(z):
    return jnp.maximum(z, 0.0) + jnp.log1p(jnp.exp(-jnp.abs(z)))


def _silu(z):
    return z * jax.nn.sigmoid(z)


def _own_layer(ref, first_call):
    if not first_call:
        return ref
    for l in range(1, DEPTH):
        ref[l] = jnp.zeros(ref.shape[1:], ref.dtype)
    return ref.at[0]


def _load_head_states(ref, bd_f, transposed):
    out = []
    for d in range(2):
        rows = jnp.concatenate([ref[d, h] for h in range(N_SLAB_HEADS)], axis=0)
        wide = jnp.concatenate([rows.T] * N_SLAB_HEADS, axis=0) * bd_f
        out.append(wide if transposed else wide.T)
    return jnp.concatenate(out, axis=0)


def _store_head_states(ref, st, transposed):
    for d in range(2):
        s = st[d * SLAB:(d + 1) * SLAB]
        if transposed:
            s = s.T
        for h in range(N_SLAB_HEADS):
            ref[d, h] = s[h * HEAD_DIM:(h + 1) * HEAD_DIM, h * HEAD_DIM:(h + 1) * HEAD_DIM]


def _gdn_prepare(qns, kns, vs, betas, gs, slab_ref, bd_b, tri_ref):
    eye = slab_ref[4]
    chains = [(j, d) for j in range(len(qns)) for d in (0, 1)]
    kn_b = [_bf(kn) for kn in kns]
    prods = [_mm_nt(jnp.concatenate([kn_b[j], _bf(qns[j])], axis=0), _blockdiag(kn_b[j], bd_b))
             for j in range(len(qns))]
    ones = jnp.ones((8, CH), jnp.bfloat16)
    cs = [_exact_rows_mm(tri_ref[d], gs[i]) for i, (j, d) in enumerate(chains)]
    r_rows = [_exact_rows_mm(ones, c * eye)[0:1] for c in cs]
    decs = [jnp.exp(jnp.where(slab_ref[2 * d] > 0.5, cs[i] - r_rows[i], 0.0)) * slab_ref[2 * d]
            for i, (j, d) in enumerate(chains)]
    t_invs = _unit_triangular_inverses(
        [-(slab_ref[2 * d + 1] * betas[i] * decs[i] * prods[j][0:CH]) for i, (j, d) in enumerate(chains)], eye, bd_b)
    g_lasts = [cs[i][0:1] if d == 1 else cs[i][CH - 1:CH] for i, (j, d) in enumerate(chains)]
    e_cs = [jnp.exp(c) for c in cs]
    rhs = [jnp.concatenate([_blockdiag(_bf(betas[i] * vs[j]), bd_b),
                            _blockdiag(_bf(betas[i] * e_cs[i] * kns[j]), bd_b)], axis=1)
           for i, (j, d) in enumerate(chains)]
    ws = [_mm(_bf(t), r) for t, r in zip(t_invs, rhs)]
    return ([w[:, 0:SLAB] for w in ws],
            [_bf(w[:, SLAB:2 * SLAB]) for w in ws],
            [_bf(prods[j][CH:2 * CH] * decs[i]) for i, (j, d) in enumerate(chains)],
            [_bf(qns[j] * e_cs[i]) for i, (j, d) in enumerate(chains)],
            [_bf(kns[j] * jnp.exp(g_lasts[i] - cs[i])) for i, (j, d) in enumerate(chains)],
            [jnp.broadcast_to(jnp.exp(g), (8, SLAB)) for g in g_lasts])


def _gdn_kernel(n, has_s0, n_aliased, *refs):
    refs = list(refs)
    x_ref, small_ref = refs[0:2]
    k = 2
    if has_s0:
        s0_ref = refs[k]
        k += 1
    conv_ref, dec_ref, gain_ref, e_ref, slab_ref, bd_ref, tri_ref = refs[k:k + 7]
    k += 7 + n_aliased
    o_ref, sfin_ref, qn_s, kn_s, v_s, g_s, b_s, wv_s, wk_s, qk_s, qd_s, ke_s, eg_s, o_s, st_s = refs[k:]
    nc = n // CH
    bd_f = bd_ref[...]
    bd_b = _bf(bd_f)

    x = x_ref[:, 0:3 * SLAB]
    row = lax.broadcasted_iota(jnp.int32, (n, 1), 0)
    prev = jnp.where(row == 0, 0.0, pltpu.roll(x, 1, axis=0))
    nxt = jnp.where(row == n - 1, 0.0, pltpu.roll(x, n - 1, axis=0))
    y = _silu(prev * conv_ref[0:1, :] + x * conv_ref[1:2, :] + nxt * conv_ref[2:3, :])
    q = y[:, 0:SLAB]
    kk = y[:, SLAB:2 * SLAB]
    qn_s[...] = q * lax.rsqrt(_group_sum(q * q, bd_f) + EPS) * (HEAD_DIM ** -0.5)
    kn_s[...] = kk * lax.rsqrt(_group_sum(kk * kk, bd_f) + EPS)
    v_s[...] = y[:, 2 * SLAB:3 * SLAB]
    sm = small_ref[...]
    lane = lax.broadcasted_iota(jnp.int32, sm.shape, 1)
    decay = -jnp.exp(dec_ref[0:1, :]) * _softplus(sm + dec_ref[1:2, :])
    scal = jnp.where(lane < SMALL_GDN + 2 * GDN_HEADS, decay, jax.nn.sigmoid(sm))
    scal = jnp.where((lane >= SMALL_GDN) & (lane < SMALL_GDN + 4 * GDN_HEADS), scal, 0.0)
    bc = _exact_cols_mm(scal, _bf(e_ref[...]))
    for d in range(2):
        g_s[d] = bc[:, d * SLAB:(d + 1) * SLAB]
        b_s[d] = bc[:, (2 + d) * SLAB:(3 + d) * SLAB]
    if has_s0:
        st_s[...] = _load_head_states(s0_ref, bd_f, transposed=False)
    else:
        st_s[...] = jnp.zeros_like(st_s)

    def prepare(i, carry):
        c0 = i * PREP_CHUNKS
        rows = [pl.ds(pl.multiple_of((c0 + j) * CH, CH), CH) for j in range(PREP_CHUNKS)]
        vals = _gdn_prepare([qn_s[r, :] for r in rows], [kn_s[r, :] for r in rows], [v_s[r, :] for r in rows],
                            [b_s[d, r, :] for r in rows for d in range(2)],
                            [g_s[d, r, :] for r in rows for d in range(2)], slab_ref, bd_b, tri_ref)
        for ref, chain_vals in zip((wv_s, wk_s, qk_s, qd_s, ke_s, eg_s), vals):
            for j in range(PREP_CHUNKS):
                ref[c0 + j] = jnp.concatenate(chain_vals[2 * j:2 * j + 2], axis=0)
        return carry

    lax.fori_loop(0, nc // PREP_CHUNKS, prepare, 0)

    def step(t, carry):
        s_all = st_s[...]
        dirs = (0, 1)
        cc = (t, nc - 1 - t)
        half = (slice(0, CH), slice(CH, 2 * CH))
        s = [s_all[d * SLAB:(d + 1) * SLAB] for d in dirs]
        s_b = [_bf(x) for x in s]
        u_b = [_bf(wv_s[cc[d], half[d], :] - _mm(wk_s[cc[d], half[d], :], s_b[d])) for d in dirs]
        upd = [_mm_tn(ke_s[cc[d], half[d], :], u_b[d]) for d in dirs]
        outs = [_mm(qd_s[cc[d], half[d], :], s_b[d]) + _mm(qk_s[cc[d], half[d], :], _blockdiag(u_b[d], bd_b))
                for d in dirs]
        new_s = [s[d] * eg_s[cc[d], d * 8:d * 8 + 1, :] + bd_f * upd[d] for d in dirs]
        st_s[...] = jnp.concatenate(new_s, axis=0)
        o_s[t] = jnp.concatenate(outs, axis=0)
        return carry

    lax.fori_loop(0, nc, step, 0)
    _store_head_states(_own_layer(sfin_ref, not has_s0 and n_aliased == 0), st_s[...], transposed=False)
    o = jnp.concatenate([o_s[c, 0:CH, :] + o_s[nc - 1 - c, CH:2 * CH, :] for c in range(nc)], axis=0)
    o = o * lax.rsqrt(_group_sum(o * o, bd_f) * (1.0 / HEAD_DIM) + EPS) * gain_ref[...]
    o_ref[...] = o * _silu(x_ref[:, 3 * SLAB:4 * SLAB])


def _gla_prepare(qs, ks, vs, las, slab_ref, bd_f, tri_ref):
    bd_b = _bf(bd_f)
    chains = [(j, d) for j in range(len(qs)) for d in (0, 1)]
    cums = [_exact_rows_mm(tri_ref[d], las[i]) for i, (j, d) in enumerate(chains)]
    lasts = [cums[i][0:1] if d == 1 else cums[i][CH - 1:CH] for i, (j, d) in enumerate(chains)]
    q_dec = [_bf(qs[j] * (HEAD_DIM ** -0.5) * jnp.exp(cums[i])) for i, (j, d) in enumerate(chains)]
    k_inv = [_bf(ks[j] * jnp.exp(-cums[i])) for i, (j, d) in enumerate(chains)]
    k_end = [_bf(ks[j] * jnp.exp(lasts[i] - cums[i])) for i, (j, d) in enumerate(chains)]
    v_b = [_bf(v) for v in vs]
    v_bd = [_blockdiag(v, bd_b) for v in v_b]
    att = [_mm_nt(q_dec[i], _blockdiag(k_inv[i], bd_b)) * slab_ref[2 * d] for i, (j, d) in enumerate(chains)]
    upd = [bd_f * _mm_tn(v_b[j], k_end[i]) for i, (j, d) in enumerate(chains)]
    intra = [_mm(_bf(att[i]), v_bd[j]) for i, (j, d) in enumerate(chains)]
    return intra, q_dec, upd, [jnp.broadcast_to(jnp.exp(l), (8, SLAB)) for l in lasts]


def _gla_kernel(n, has_s0, n_aliased, *refs):
    refs = list(refs)
    x_ref, small_ref = refs[0:2]
    k = 2
    if has_s0:
        s0_ref = refs[k]
        k += 1
    wup_ref, bup_ref, gain_ref, slab_ref, bd_ref, tri_ref = refs[k:k + 6]
    k += 6 + n_aliased
    o_ref, sfin_ref, la_s, oi_s, qd_s, up_s, el_s, o_s, st_s = refs[k:]
    nc = n // CH
    bd_f = bd_ref[...]
    small = small_ref[...]
    for d in range(2):
        z = _mm_x3(small, wup_ref[d]) + bup_ref[d:d + 1, :]
        la_s[d] = (jnp.minimum(z, 0.0) - jnp.log1p(jnp.exp(-jnp.abs(z)))) * (1.0 / GLA_TAU)
    if has_s0:
        st_s[...] = _load_head_states(s0_ref, bd_f, transposed=True)
    else:
        st_s[...] = jnp.zeros_like(st_s)

    def prepare(i, carry):
        c0 = i * PREP_CHUNKS
        rows = [pl.ds(pl.multiple_of((c0 + j) * CH, CH), CH) for j in range(PREP_CHUNKS)]
        vals = _gla_prepare([x_ref[r, 0:SLAB] for r in rows], [x_ref[r, SLAB:2 * SLAB] for r in rows],
                            [x_ref[r, 2 * SLAB:3 * SLAB] for r in rows],
                            [la_s[d, r, :] for r in rows for d in range(2)], slab_ref, bd_f, tri_ref)
        for ref, chain_vals in zip((oi_s, qd_s, up_s, el_s), vals):
            for j in range(PREP_CHUNKS):
                ref[c0 + j] = jnp.concatenate(chain_vals[2 * j:2 * j + 2], axis=0)
        return carry

    lax.fori_loop(0, nc // PREP_CHUNKS, prepare, 0)

    def step(t, carry):
        s_all = st_s[...]
        cc = (t, nc - 1 - t)
        s = [s_all[d * SLAB:(d + 1) * SLAB] for d in range(2)]
        outs = [oi_s[cc[d], d * CH:(d + 1) * CH, :] + _mm_nt(qd_s[cc[d], d * CH:(d + 1) * CH, :], _bf(s[d]))
                for d in range(2)]
        new_s = [s[d] * el_s[cc[d], d * 8:d * 8 + 1, :] + up_s[cc[d], d * SLAB:(d + 1) * SLAB, :] for d in range(2)]
        st_s[...] = jnp.concatenate(new_s, axis=0)
        o_s[t] = jnp.concatenate(outs, axis=0)
        return carry

    lax.fori_loop(0, nc, step, 0)
    _store_head_states(_own_layer(sfin_ref, not has_s0 and n_aliased == 0), st_s[...], transposed=True)
    o = jnp.concatenate([o_s[c, 0:CH, :] + o_s[nc - 1 - c, CH:2 * CH, :] for c in range(nc)], axis=0)
    o = o * lax.rsqrt(_group_sum(o * o, bd_f) * (1.0 / HEAD_DIM) + EPS) * gain_ref[...]
    o_ref[...] = o * _silu(x_ref[:, 3 * SLAB:4 * SLAB])


def _linear_mixer(kind, layer, hp, col_block, s0, params, consts, ctx_states=None):
    latent = s0 is not None
    n = DEC_SEQ if latent else SEQ
    n_seq = DEC_BATCH if latent else BATCH
    row0 = N_PROMPT_TOK // n if latent else 0
    full = lambda a: pl.BlockSpec(a.shape, lambda b: (0,) * a.ndim)
    state_block = (2, N_SLAB_HEADS, HEAD_DIM, HEAD_DIM)
    in_specs = [pl.BlockSpec((n, 4 * SLAB), lambda b: (row0 + b, col_block)),
                pl.BlockSpec((n, LANES), lambda b: (row0 + b, SMALL_COL_BLOCK))]
    args = [hp, hp]
    if latent:
        in_specs.append(pl.BlockSpec((None, None) + state_block, lambda b: (b, layer, 0, 0, 0, 0)))
        args.append(s0)
    for a in tuple(params) + tuple(consts):
        in_specs.append(full(a))
        args.append(a)
    aliases = {}
    if ctx_states is not None:
        aliases = {len(args): 1}
        in_specs.append(pl.BlockSpec(memory_space=pl.ANY))
        args.append(ctx_states)
    if latent:
        state_spec = pl.BlockSpec((None,) + state_block, lambda b: (b, 0, 0, 0, 0))
        state_shape = (n_seq,) + state_block
    elif ctx_states is None:
        assert layer == 0
        state_spec = pl.BlockSpec((None, DEPTH) + state_block, lambda b: (b, 0, 0, 0, 0, 0))
        state_shape = (n_seq, DEPTH) + state_block
    else:
        state_spec = pl.BlockSpec((None, None) + state_block, lambda b: (b, layer, 0, 0, 0, 0))
        state_shape = (n_seq, DEPTH) + state_block
    nc = n // CH
    seq_buf = pltpu.VMEM((n, SLAB), jnp.float32)
    dir_buf = pltpu.VMEM((2, n, SLAB), jnp.float32)
    pair_f32 = pltpu.VMEM((nc, 2 * CH, SLAB), jnp.float32)
    pair_b16 = pltpu.VMEM((nc, 2 * CH, SLAB), jnp.bfloat16)
    state_buf = pltpu.VMEM((2 * SLAB, SLAB), jnp.float32)
    if kind == 'gdn':
        body = functools.partial(_gdn_kernel, n, latent, len(aliases))
        scratch = [seq_buf, seq_buf, seq_buf, dir_buf, dir_buf, pair_f32, pair_b16, pair_b16, pair_b16, pair_b16,
                   pltpu.VMEM((nc, 16, SLAB), jnp.float32), pair_f32, state_buf]
    else:
        body = functools.partial(_gla_kernel, n, latent, len(aliases))
        scratch = [dir_buf, pair_f32, pair_b16, pltpu.VMEM((nc, 2 * SLAB, SLAB), jnp.float32),
                   pltpu.VMEM((nc, 16, SLAB), jnp.float32), pair_f32, state_buf]
    return pl.pallas_call(
        body,
        grid=(n_seq,),
        in_specs=in_specs,
        out_specs=[pl.BlockSpec((n, SLAB), lambda b: (b, 0)), state_spec],
        out_shape=[jax.ShapeDtypeStruct((n_seq * n, SLAB), jnp.float32),
                   jax.ShapeDtypeStruct(state_shape, jnp.float32)],
        scratch_shapes=scratch,
        input_output_aliases=aliases,
        compiler_params=_cparams("arbitrary"),
        name=kind + ("_latent" if latent else "_ctx"),
    )(*args)


ATT_SCALE = HEAD_DIM ** -0.5
NA_ROWS = DEC_SEQ // GRID_W
NA_KROWS = min(NA_KH, NA_ROWS)
NA_WIN = NA_KROWS * GRID_W
N_SWA_BLOCKS = DEC_SEQ // SWA_BLOCK
assert SWA_WINDOW == SWA_BLOCK
assert NA_KROWS % 2 == 0 and NA_ROWS % 2 == 0


def _head_rmsnorm(x, bd_f, gain):
    return x * lax.rsqrt(_group_sum(x * x, bd_f) * (1.0 / HEAD_DIM) + EPS) * gain


def _lane_group(shape):
    return lax.broadcasted_iota(jnp.int32, shape, 1) // HEAD_DIM


def _stack_groups(x):
    grp = _lane_group(x.shape)
    return jnp.concatenate([jnp.where(grp == g, x, jnp.zeros_like(x)) for g in range(x.shape[1] // HEAD_DIM)], axis=0)


def _stack_swa_queries(q_b):
    return jnp.concatenate([_stack_groups(q_b[:, 0:LANES]), _stack_groups(q_b[:, LANES:2 * LANES])], axis=0)


def _unstack_swa(o, m):
    low = _lane_group((m, LANES)) == 0
    return jnp.concatenate([jnp.where(low, o[0:m], o[m:2 * m]), jnp.where(low, o[2 * m:3 * m], o[3 * m:4 * m])],
                           axis=1)


def _unstack_groups(o, m):
    grp = _lane_group((m, o.shape[1]))
    out = jnp.where(grp == 0, o[0:m], 0.0)
    for g in range(1, o.shape[1] // HEAD_DIM):
        out = out + jnp.where(grp == g, o[g * m:(g + 1) * m], 0.0)
    return out


def _softmax_pv_chains(chains):
    ms = []
    for logits, _, extra, _ in chains:
        m = jnp.max(logits[0], axis=-1, keepdims=True)
        for l in logits[1:]:
            m = jnp.maximum(m, jnp.max(l, axis=-1, keepdims=True))
        ms.append(m if extra is None else jnp.maximum(m, extra))
    es = [[jnp.exp(l - m) for l in c[0]] for c, m in zip(chains, ms)]
    dens = []
    for c, m, e in zip(chains, ms, es):
        den = jnp.exp(c[2] - m) if c[2] is not None else 0.0
        for piece in e:
            den = den + jnp.sum(piece, axis=-1, keepdims=True)
        dens.append(den)
    pvs = [0.0] * len(chains)
    for p in range(max(len(c[0]) for c in chains)):
        for ci, c in enumerate(chains):
            if p < len(c[0]):
                transposed = c[3] and p == len(c[0]) - 1
                e_b = _bf(es[ci][p])
                pvs[ci] = pvs[ci] + (_mm_nt(e_b, c[1][p]) if transposed else _mm(e_b, c[1][p]))
    return [pv / den for pv, den in zip(pvs, dens)]


def _sink_column(sink_ref, m):
    return jnp.concatenate([jnp.full((m, 1), sink_ref[g], jnp.float32) for g in range(SWA_HEADS)], axis=0)


def _rope(x, rope_ref):
    reps = x.shape[1] // LANES
    wide = lambda i: jnp.concatenate([rope_ref[i]] * reps, axis=1)
    shift = HEAD_DIM // 4
    return (x * wide(0) + pltpu.roll(x, x.shape[1] - shift, axis=1) * wide(1)
            + pltpu.roll(x, shift, axis=1) * wide(2))


def _store_cache(ref, slab):
    t = slab.T
    for h in range(ref.shape[0]):
        ref[h] = t[h * HEAD_DIM:(h + 1) * HEAD_DIM, :]


def _ctx_attn_kernel(n_aliased, bq_ref, bk_ref, bv_ref, cq_ref, ck_ref, cv_ref, gain_ref, sink_ref, bd_ref, *rest):
    o_ref = rest[n_aliased]
    swak_ref, swav_ref, nak_ref, nav_ref = (_own_layer(r, n_aliased == 0) for r in rest[n_aliased + 1:])
    n = SEQ
    bd_f = bd_ref[...]
    q = _head_rmsnorm(bq_ref[...], bd_f, gain_ref[0:1, :])
    k = _head_rmsnorm(bk_ref[...], bd_f[0:LANES, 0:LANES], gain_ref[1:2, 0:LANES])
    v = bv_ref[...]
    _store_cache(swak_ref, k)
    _store_cache(swav_ref, v)
    q2 = _head_rmsnorm(cq_ref[...], bd_f, gain_ref[2:3, :])
    k2 = _head_rmsnorm(ck_ref[...], bd_f, gain_ref[3:4, :])
    v2 = cv_ref[...]
    _store_cache(nak_ref, k2)
    _store_cache(nav_ref, v2)
    logits_b = _mm_nt(_stack_swa_queries(_bf(q)), _bf(k)) * ATT_SCALE
    logits_c = _mm_nt(_stack_groups(_bf(q2)), _bf(k2)) * ATT_SCALE
    out_b, out_c = _softmax_pv_chains([([logits_b], [_bf(v)], _sink_column(sink_ref, n), False),
                                       ([logits_c], [_bf(v2)], None, False)])
    o_ref[...] = jnp.concatenate([_unstack_swa(out_b, n), _unstack_groups(out_c, n)], axis=1)


def _load_cache_slabs(ck_ref, cv_ref, ck_s, cv_s):
    for src, dst in ((ck_ref, ck_s), (cv_ref, cv_s)):
        dst[...] = _bf(jnp.concatenate([src[h] for h in range(src.shape[0])], axis=0))


def _swa_latent_kernel(q_ref, k_ref, v_ref, ck_ref, cv_ref, rope_ref, gain_ref, sink_ref, bd_ref,
                       o_ref, q_s, k_s, v_s, ck_s, cv_s):
    _load_cache_slabs(ck_ref, cv_ref, ck_s, cv_s)
    bd_f = bd_ref[...]
    q_s[...] = _bf(_rope(_head_rmsnorm(q_ref[...], bd_f, gain_ref[0:1, :]), rope_ref))
    k_s[...] = _bf(_rope(_head_rmsnorm(k_ref[...], bd_f[0:LANES, 0:LANES], gain_ref[1:2, 0:LANES]), rope_ref))
    v_s[...] = _bf(v_ref[...])
    m = SWA_BLOCK
    iq = lax.broadcasted_iota(jnp.int32, (SWA_HEADS * m, m), 0) % m
    jk = lax.broadcasted_iota(jnp.int32, (SWA_HEADS * m, m), 1)
    ok_prev = jk >= iq
    ok_next = jk <= iq
    ok_same = jk >= 0
    sink = _sink_column(sink_ref, m)
    for i in range(N_SWA_BLOCKS):
        lo, hi = max(i - 1, 0), min(i + 1, N_SWA_BLOCKS - 1)
        qs = _stack_swa_queries(q_s[i * m:(i + 1) * m, :])
        allowed = jnp.concatenate([ok_prev if j < i else ok_next if j > i else ok_same for j in range(lo, hi + 1)],
                                  axis=1)
        l_loc = jnp.where(allowed, _mm_nt(qs, k_s[lo * m:(hi + 1) * m, :]) * ATT_SCALE, NEG_INF)
        l_ctx = _mm(qs, ck_s[...]) * ATT_SCALE
        o, = _softmax_pv_chains([([l_loc, l_ctx], [v_s[lo * m:(hi + 1) * m, :], cv_s[...]], sink, True)])
        o_ref[i * m:(i + 1) * m, :] = _unstack_swa(o, m)


def _na_latent_kernel(q_ref, k_ref, v_ref, ck_ref, cv_ref, bias_ref, gain_ref, bd_ref,
                      o_ref, q_s, k_s, v_s, ck_s, cv_s):
    _load_cache_slabs(ck_ref, cv_ref, ck_s, cv_s)
    bd_f = bd_ref[...]
    q_s[...] = _bf(_head_rmsnorm(q_ref[...], bd_f, gain_ref[2:3, :]))
    k_s[...] = _bf(_head_rmsnorm(k_ref[...], bd_f, gain_ref[3:4, :]))
    v_s[...] = _bf(v_ref[...])

    def row_pair(i, carry):
        rows = [2 * i, 2 * i + 1]
        rs = [jnp.clip(r - NA_KROWS // 2, 0, NA_ROWS - NA_KROWS) for r in rows]
        qrows = [pl.ds(pl.multiple_of(r * GRID_W, GRID_W), GRID_W) for r in rows]
        wins = [pl.ds(pl.multiple_of(s * GRID_W, GRID_W), NA_WIN) for s in rs]
        qs = [_stack_groups(q_s[qr, :]) for qr in qrows]
        bias = [jnp.concatenate([bias_ref[s - r + NA_KH - 1 + 2 * p] for p in range(NA_KROWS // 2)], axis=1)
                for r, s in zip(rows, rs)]
        l_loc = [_mm_nt(qs[a], k_s[wins[a], :]) * ATT_SCALE + bias[a] for a in range(2)]
        l_ctx = [_mm(qs[a], ck_s[...]) * ATT_SCALE for a in range(2)]
        m = [jnp.maximum(jnp.max(l_loc[a], axis=-1, keepdims=True), jnp.max(l_ctx[a], axis=-1, keepdims=True))
             for a in range(2)]
        e_loc = [jnp.exp(l_loc[a] - m[a]) for a in range(2)]
        e_ctx = [jnp.exp(l_ctx[a] - m[a]) for a in range(2)]
        den = [jnp.sum(e_loc[a], axis=-1, keepdims=True) + jnp.sum(e_ctx[a], axis=-1, keepdims=True)
               for a in range(2)]
        pv_loc = [_mm(_bf(e_loc[a]), v_s[wins[a], :]) for a in range(2)]
        pv_ctx = [_mm_nt(_bf(e_ctx[a]), cv_s[...]) for a in range(2)]
        for a in range(2):
            o_ref[qrows[a], :] = _unstack_groups((pv_loc[a] + pv_ctx[a]) / den[a], GRID_W)
        return carry

    lax.fori_loop(0, NA_ROWS // 2, row_pair, 0)


def _rope_tables():
    t = jnp.arange(DEC_SEQ)
    rows = (t // GRID_W).astype(jnp.float32)
    cols = (t % GRID_W).astype(jnp.float32)
    half = HEAD_DIM // 2
    nf = half // 2
    inv = 1.0 / (ROPE_BASE ** (jnp.arange(nf, dtype=jnp.float32) / nf))
    d = np.arange(LANES) % HEAD_DIM
    pos = jnp.where(jnp.asarray(d < half)[None, :], rows[:, None], cols[:, None])
    ang = pos * inv[jnp.asarray(d % nf)][None, :]
    first = jnp.asarray((d % half) < nf)[None, :]
    cos, sin = jnp.cos(ang), jnp.sin(ang)
    return jnp.stack([cos, jnp.where(first, -sin, 0.0), jnp.where(first, 0.0, sin)])


def _na_bias_table(rpb):
    col = np.arange(GRID_W)
    cs = np.clip(col - NA_KW // 2, 0, GRID_W - NA_KW)
    col_mask = (col[None, :] >= cs[:, None]) & (col[None, :] < cs[:, None] + NA_KW)
    dc = np.clip(col[None, :] - col[:, None], -(NA_KW - 1), NA_KW - 1) + NA_KW - 1
    pick = (np.arange(2 * NA_KW - 1)[:, None, None] == dc[None]).astype(np.float32)
    b = jnp.einsum('hrd,dqk->hrqk', rpb, jnp.asarray(pick), precision=_HI)
    b = jnp.where(jnp.asarray(col_mask)[None, None], b, NEG_INF)
    pair = jnp.concatenate([b[:, 0:2 * NA_KH - 2], b[:, 1:2 * NA_KH - 1]], axis=-1)
    return pair.transpose(1, 0, 2, 3).reshape(2 * NA_KH - 2, NA_HEADS * GRID_W, 2 * GRID_W)


def _attention_ctx(layer, hp, gains, sink, bd_c, caches):
    cb = lambda w, off: (lambda b: (b, off // w))
    q_w, kv_w = GROUP_WIDTH, SWA_KV_WIDTH
    full = lambda a: pl.BlockSpec(a.shape, lambda b: (0,) * a.ndim)
    if caches is None:
        assert layer == 0
        cache_out = lambda h: pl.BlockSpec((None, DEPTH, h, HEAD_DIM, SEQ), lambda b: (b, 0, 0, 0, 0))
    else:
        cache_out = lambda h: pl.BlockSpec((None, None, h, HEAD_DIM, SEQ), lambda b: (b, layer, 0, 0, 0))
    cache_shape = lambda h: jax.ShapeDtypeStruct((BATCH, DEPTH, h, HEAD_DIM, SEQ), jnp.float32)
    n_in = 9
    prev = [] if caches is None else list(caches)
    return pl.pallas_call(
        functools.partial(_ctx_attn_kernel, len(prev)),
        grid=(BATCH,),
        in_specs=[pl.BlockSpec((SEQ, q_w), cb(q_w, COL_B)),
                  pl.BlockSpec((SEQ, kv_w), cb(kv_w, COL_B + q_w)),
                  pl.BlockSpec((SEQ, kv_w), cb(kv_w, COL_B + q_w + kv_w)),
                  pl.BlockSpec((SEQ, q_w), cb(q_w, COL_C)),
                  pl.BlockSpec((SEQ, q_w), cb(q_w, COL_C + q_w)),
                  pl.BlockSpec((SEQ, q_w), cb(q_w, COL_C + 2 * q_w)),
                  full(gains), pl.BlockSpec(memory_space=pltpu.SMEM), full(bd_c)]
                 + [pl.BlockSpec(memory_space=pl.ANY)] * len(prev),
        out_specs=[pl.BlockSpec((SEQ, 2 * q_w), lambda b: (b, 0)),
                   cache_out(SWA_KV_HEADS), cache_out(SWA_KV_HEADS), cache_out(NA_HEADS), cache_out(NA_HEADS)],
        out_shape=[jax.ShapeDtypeStruct((N_PROMPT_TOK, 2 * q_w), jnp.float32),
                   cache_shape(SWA_KV_HEADS), cache_shape(SWA_KV_HEADS), cache_shape(NA_HEADS), cache_shape(NA_HEADS)],
        input_output_aliases={n_in + i: 1 + i for i in range(len(prev))},
        compiler_params=_cparams("arbitrary"),
        name="attn_ctx",
    )(hp, hp, hp, hp, hp, hp, gains, sink, bd_c, *prev)


def _attention_latent(kind, layer, hp, ck, cv, table, gains, sink, bd_c):
    n = DEC_SEQ
    row0 = N_PROMPT_TOK // n
    q_w = GROUP_WIDTH
    kv_w = SWA_KV_WIDTH if kind == 'swa' else q_w
    col = COL_B if kind == 'swa' else COL_C
    cb = lambda w, off: (lambda b: (row0 + b, off // w))
    full = lambda a: pl.BlockSpec(a.shape, lambda b: (0,) * a.ndim)
    cache_spec = pl.BlockSpec((None, None, kv_w // HEAD_DIM, HEAD_DIM, PAST_LEN), lambda b: (b, layer, 0, 0, 0))
    in_specs = [pl.BlockSpec((n, q_w), cb(q_w, col)),
                pl.BlockSpec((n, kv_w), cb(kv_w, col + q_w)),
                pl.BlockSpec((n, kv_w), cb(kv_w, col + q_w + kv_w)),
                cache_spec, cache_spec,
                full(table), full(gains)]
    args = [hp, hp, hp, ck, cv, table, gains]
    if kind == 'swa':
        in_specs.append(pl.BlockSpec(memory_space=pltpu.SMEM))
        args.append(sink)
    in_specs.append(full(bd_c))
    args.append(bd_c)
    return pl.pallas_call(
        _swa_latent_kernel if kind == 'swa' else _na_latent_kernel,
        grid=(DEC_BATCH,),
        in_specs=in_specs,
        out_specs=pl.BlockSpec((n, q_w), lambda b: (b, 0)),
        out_shape=jax.ShapeDtypeStruct((N_SAMPLE_TOK, q_w), jnp.float32),
        scratch_shapes=[pltpu.VMEM((n, q_w), jnp.bfloat16), pltpu.VMEM((n, kv_w), jnp.bfloat16),
                        pltpu.VMEM((n, kv_w), jnp.bfloat16), pltpu.VMEM((kv_w, PAST_LEN), jnp.bfloat16),
                        pltpu.VMEM((kv_w, PAST_LEN), jnp.bfloat16)],
        compiler_params=_cparams("arbitrary"),
        name=kind + "_latent",
    )(*args)


def kernel(x_prompt, x_sample, cache_swa_k, cache_swa_v, cache_na_k, cache_na_v, state_gla, state_gdn, c, c_ctx, w_mod, b_mod, norm1_g, norm2_g, w_in, w_out, gla_wup, gla_bup, gla_onorm, swa_qnorm, swa_knorm, swa_sink, na_qnorm, na_knorm, na_rpb, gdn_conv, gdn_alog, gdn_dtbias, gdn_onorm, ffn_w1, ffn_w3, ffn_w2, moe_router, moe_w1, moe_w3, moe_w2):
    bf16 = jnp.bfloat16
    x = jnp.concatenate([x_prompt.reshape(N_PROMPT_TOK, D_MODEL), x_sample.reshape(N_SAMPLE_TOK, D_MODEL)], axis=0)
    cvec = jnp.concatenate([c_ctx[None, :], c, jnp.zeros((N_MOD_ROWS - 1 - DEC_BATCH, D_MODEL), jnp.float32)], axis=0)
    mod_all = _modulation_all(cvec, w_mod, b_mod).reshape(DEPTH, N_MOD_ROWS, 1, MOD_W)
    w_in_b = jnp.concatenate([_take_segments(w_in.astype(bf16), _in_col_segments(), 2),
                              jnp.zeros((DEPTH, D_MODEL, IN_COLS_PAD - IN_COLS), bf16)], axis=2)
    w_out_b = _take_segments(w_out.astype(bf16), _OUT_ROW_SEGMENTS, 1)
    rope_tables = _rope_tables()
    consts = _linear_consts()
    tile_heads = lambda g: jnp.tile(g, N_SLAB_HEADS)[None, :]
    small_expand = jnp.asarray(_small_expand_matrix())
    ffn_w = [w.astype(bf16) for w in (ffn_w1, ffn_w3, ffn_w2)]
    moe_w = (moe_w1, moe_w3, moe_w2)
    ctx_kv = [jnp.swapaxes(c_, -1, -2) for c_ in (cache_swa_k, cache_swa_v, cache_na_k, cache_na_v)]

    new_kv = st_a = st_d = None
    for l in range(DEPTH):
        mod = mod_all[l]
        hp = _in_proj(l, x, mod, norm1_g[l][None, :], w_in_b)

        wup = jnp.zeros((2, LANES, SLAB), jnp.float32)
        wup = wup.at[0, 0:GLA_LOWRANK].set(gla_wup[l, 0]).at[1, GLA_LOWRANK:2 * GLA_LOWRANK].set(gla_wup[l, 1])
        gla_params = (wup, gla_bup[l], tile_heads(gla_onorm[l]))
        a_ctx, st_a = _linear_mixer('gla', l, hp, COL_A // (4 * SLAB), None, gla_params, consts, st_a)
        a_lat, _ = _linear_mixer('gla', l, hp, COL_A // (4 * SLAB), state_gla, gla_params, consts)
        decay_cols = slice(SMALL_GDN, SMALL_GDN + 2 * GDN_HEADS)
        decay_params = jnp.zeros((2, LANES), jnp.float32).at[:, decay_cols].set(
            jnp.stack([gdn_alog[l].reshape(-1), gdn_dtbias[l].reshape(-1)]))
        gdn_params = (gdn_conv[l], decay_params, tile_heads(gdn_onorm[l]), small_expand)
        d_ctx, st_d = _linear_mixer('gdn', l, hp, COL_D // (4 * SLAB), None, gdn_params, consts, st_d)
        d_lat, _ = _linear_mixer('gdn', l, hp, COL_D // (4 * SLAB), state_gdn, gdn_params, consts)

        gains = jnp.stack([jnp.tile(g, N_SLAB_HEADS) for g in (swa_qnorm[l], swa_knorm[l], na_qnorm[l], na_knorm[l])])
        sink = swa_sink[l][jnp.asarray(SWA_Q_HEAD_ORDER)]
        bc_ctx, *new_kv = _attention_ctx(l, hp, gains, sink, consts[1], new_kv)
        b_lat = _attention_latent('swa', l, hp, ctx_kv[0], ctx_kv[1], rope_tables, gains, sink, consts[1])
        c_lat = _attention_latent('na', l, hp, ctx_kv[2], ctx_kv[3], _na_bias_table(na_rpb[l]), gains, None,
                                  consts[1])
        x = _out_proj(l, (a_ctx, bc_ctx, d_ctx), (a_lat, b_lat, c_lat, d_lat), x, mod, w_out_b)
        if l % 2 == 0:
            x = _ffn(l // 2, x, mod, norm2_g[l][None, :], *ffn_w)
        else:
            x = _moe_routed(l // 2, x, mod, norm2_g[l][None, :], moe_router[l // 2], *moe_w)

    outs = [jnp.swapaxes(o, -1, -2) for o in new_kv] + [st_a, st_d]
    y_prompt = x[:N_PROMPT_TOK].reshape(BATCH, SEQ, D_MODEL)
    y_sample = x[N_PROMPT_TOK:].reshape(DEC_BATCH, DEC_SEQ, D_MODEL)
    return (y_prompt, y_sample, *outs)
```

```python
import functools
import math

import numpy as np
import jax
import jax.numpy as jnp
from jax import lax
from jax.experimental import pallas as pl
from jax.experimental.pallas import tpu as pltpu
from jax.experimental.pallas import tpu_sc as plsc

D_MODEL = 1024
BATCH = 32
SEQ = 256
DEPTH = 4
DEC_BATCH = 8
DEC_SEQ = 1024
PAST_LEN = 512

GRID_W = 64
HEAD_DIM = 64
GROUP_WIDTH = D_MODEL // 4
GROUP_HEADS = GROUP_WIDTH // HEAD_DIM
GLA_HEADS = GROUP_HEADS
GLA_LOWRANK = 16
GLA_TAU = 16.0
GLA_CHUNK = 64
SWA_HEADS = GROUP_HEADS
SWA_KV_HEADS = 2
SWA_WINDOW = 128
SWA_BLOCK = 128
NA_HEADS = GROUP_HEADS
NA_KH = 8
NA_KW = 16
GDN_HEADS = GROUP_HEADS
GDN_CONV = 3
GDN_CHUNK = 64
D_FF = 2816
N_EXPERTS = 8
TOP_K = 2
D_FF_EXPERT = 1024
ROPE_BASE = 10000.0
EPS = 1e-6
NEG_INF = -1e30
SWA_KV_WIDTH = SWA_KV_HEADS * HEAD_DIM
IN_SPLITS = ([GROUP_WIDTH] * 4 + [GLA_LOWRANK] * 2 + [GROUP_WIDTH, SWA_KV_WIDTH, SWA_KV_WIDTH]
             + [GROUP_WIDTH] * 3 + [GROUP_WIDTH] * 4 + [GDN_HEADS] * 4)
IN_COLS = sum(IN_SPLITS)

LANES = 128
VMEM_LIMIT_BYTES = 56 * 1024 * 1024

N_PROMPT_TOK = BATCH * SEQ
N_SAMPLE_TOK = DEC_BATCH * DEC_SEQ
N_TOK = N_PROMPT_TOK + N_SAMPLE_TOK
N_MOD_ROWS = 16
MOD_W = 6 * D_MODEL
TM = 1024
IN_COLS_PAD = -(-IN_COLS // LANES) * LANES

COL_A = 0
COL_D = 4 * GROUP_WIDTH
COL_C = 8 * GROUP_WIDTH
COL_B = 11 * GROUP_WIDTH
COL_SMALL = COL_B + GROUP_WIDTH + 2 * SWA_KV_WIDTH
SMALL_COL_BLOCK = COL_SMALL // LANES
SMALL_GDN = 2 * GLA_LOWRANK


SWA_Q_HEAD_ORDER = (0, 2, 1, 3)
def _in_col_segments():
    off = [0] + [int(v) for v in np.cumsum(IN_SPLITS)]
    seg = lambda a, b: [(off[a], off[b])]
    swa_q = [(off[6] + h * HEAD_DIM, off[6] + (h + 1) * HEAD_DIM) for h in SWA_Q_HEAD_ORDER]
    return seg(0, 4) + seg(12, 16) + seg(9, 12) + swa_q + seg(7, 9) + seg(4, 6) + seg(16, 20)


_OUT_ROW_SEGMENTS = ([(0, GROUP_WIDTH)]
                     + [(GROUP_WIDTH + h * HEAD_DIM, GROUP_WIDTH + (h + 1) * HEAD_DIM) for h in SWA_Q_HEAD_ORDER]
                     + [(2 * GROUP_WIDTH, D_MODEL)])


def _take_segments(w, segments, axis):
    idx = [slice(None)] * w.ndim
    parts = []
    for a, b in segments:
        idx[axis] = slice(a, b)
        parts.append(w[tuple(idx)])
    return jnp.concatenate(parts, axis=axis)


def _small_expand_matrix():
    e = np.zeros((LANES, 4 * GROUP_WIDTH), np.float32)
    for s in range(4):
        for h in range(GROUP_HEADS):
            e[SMALL_GDN + s * GROUP_HEADS + h, s * GROUP_WIDTH + h * HEAD_DIM:s * GROUP_WIDTH + (h + 1) * HEAD_DIM] = 1.0
    return e


def _mod_row(i):
    n_prompt_tiles = N_PROMPT_TOK // TM
    return jnp.where(i < n_prompt_tiles, 0, 1 + (i - n_prompt_tiles) // (DEC_SEQ // TM))


def _cparams(*sem):
    return pltpu.CompilerParams(dimension_semantics=sem, vmem_limit_bytes=VMEM_LIMIT_BYTES)


def _mod_kernel(c_ref, w_ref, b_ref, o_ref):
    c = c_ref[...]
    s = c * jax.nn.sigmoid(c)
    o_ref[...] = jnp.dot(s, w_ref[...], preferred_element_type=jnp.float32,
                         precision=lax.Precision.HIGHEST) + b_ref[...]


def _modulation_all(cvec, w_mod, b_mod):
    tn = 1536
    return pl.pallas_call(
        _mod_kernel,
        grid=(DEPTH, MOD_W // tn),
        in_specs=[pl.BlockSpec((N_MOD_ROWS, D_MODEL), lambda l, j: (0, 0)),
                  pl.BlockSpec((None, D_MODEL, tn), lambda l, j: (l, 0, j)),
                  pl.BlockSpec((None, 1, tn), lambda l, j: (l, 0, j))],
        out_specs=pl.BlockSpec((None, N_MOD_ROWS, tn), lambda l, j: (l, 0, j)),
        out_shape=jax.ShapeDtypeStruct((DEPTH, N_MOD_ROWS, MOD_W), jnp.float32),
        compiler_params=_cparams("arbitrary", "arbitrary"),
        name="modulation",
    )(cvec, w_mod, b_mod.reshape(DEPTH, 1, MOD_W))


def _modulated_norm(x, g, shift, scale):
    y = x * lax.rsqrt(jnp.mean(x * x, axis=-1, keepdims=True) + EPS) * g
    return y * (1.0 + scale) + shift


def _in_proj_kernel(x_ref, mod_ref, g_ref, w_ref, o_ref):
    h = _modulated_norm(x_ref[...], g_ref[...], mod_ref[:, 0:D_MODEL], mod_ref[:, D_MODEL:2 * D_MODEL])
    o_ref[...] = jnp.dot(h.astype(jnp.bfloat16), w_ref[...], preferred_element_type=jnp.float32)


def _in_proj(layer, x, mod, g, w):
    return pl.pallas_call(
        _in_proj_kernel,
        grid=(N_TOK // TM,),
        in_specs=[pl.BlockSpec((TM, D_MODEL), lambda i: (i, 0)),
                  pl.BlockSpec((None, 1, MOD_W), lambda i: (_mod_row(i), 0, 0)),
                  pl.BlockSpec((1, D_MODEL), lambda i: (0, 0)),
                  pl.BlockSpec((None, D_MODEL, IN_COLS_PAD), lambda i: (layer, 0, 0),
                               pipeline_mode=pl.Buffered(1))],
        out_specs=pl.BlockSpec((TM, IN_COLS_PAD), lambda i: (i, 0)),
        out_shape=jax.ShapeDtypeStruct((N_TOK, IN_COLS_PAD), jnp.float32),
        compiler_params=_cparams("arbitrary"),
        name="in_proj",
    )(x, mod, g, w)


N_CTX_TILES = N_PROMPT_TOK // TM


def _out_proj_kernel(a_c, b_c, d_c, a_l, b_l, c_l, d_l, x_ref, mod_ref, w_ref, o_ref):
    is_ctx = pl.program_id(0) < N_CTX_TILES
    ctx = jnp.concatenate([a_c[...], b_c[...], d_c[...]], axis=1)
    lat = jnp.concatenate([a_l[...], b_l[...], c_l[...], d_l[...]], axis=1)
    mix = jnp.where(is_ctx, ctx, lat)
    y = jnp.dot(mix.astype(jnp.bfloat16), w_ref[...], preferred_element_type=jnp.float32)
    o_ref[...] = x_ref[...] + mod_ref[:, 2 * D_MODEL:3 * D_MODEL] * y


def _out_proj(layer, ctx_outs, lat_outs, x, mod, w):
    ctx_map = lambda i: (jnp.minimum(i, N_CTX_TILES - 1), 0)
    lat_map = lambda i: (jnp.maximum(i - N_CTX_TILES, 0), 0)
    return pl.pallas_call(
        _out_proj_kernel,
        grid=(N_TOK // TM,),
        in_specs=[pl.BlockSpec((TM, a.shape[1]), ctx_map) for a in ctx_outs]
                 + [pl.BlockSpec((TM, a.shape[1]), lat_map) for a in lat_outs]
                 + [pl.BlockSpec((TM, D_MODEL), lambda i: (i, 0)),
                    pl.BlockSpec((None, 1, MOD_W), lambda i: (_mod_row(i), 0, 0)),
                    pl.BlockSpec((None, D_MODEL, D_MODEL), lambda i: (layer, 0, 0), pipeline_mode=pl.Buffered(1))],
        out_specs=pl.BlockSpec((TM, D_MODEL), lambda i: (i, 0)),
        out_shape=jax.ShapeDtypeStruct((N_TOK, D_MODEL), jnp.float32),
        compiler_params=_cparams("arbitrary"),
        name="out_proj",
    )(*ctx_outs, *lat_outs, x, mod, w)


FF_CHUNK = D_FF // 2


def _ffn_kernel(x_ref, mod_ref, g_ref, w1_ref, w3_ref, w2_ref, o_ref):
    x = x_ref[...]
    h = _modulated_norm(x, g_ref[...], mod_ref[:, 3 * D_MODEL:4 * D_MODEL], mod_ref[:, 4 * D_MODEL:5 * D_MODEL])
    hb = h.astype(jnp.bfloat16)
    y = jnp.zeros((TM, D_MODEL), jnp.float32)
    for c0 in range(0, D_FF, FF_CHUNK):
        a = jnp.dot(hb, w1_ref[:, c0:c0 + FF_CHUNK], preferred_element_type=jnp.float32)
        b = jnp.dot(hb, w3_ref[:, c0:c0 + FF_CHUNK], preferred_element_type=jnp.float32)
        s = (a * jax.nn.sigmoid(a) * b).astype(jnp.bfloat16)
        y = y + jnp.dot(s, w2_ref[c0:c0 + FF_CHUNK, :], preferred_element_type=jnp.float32)
    o_ref[...] = x + mod_ref[:, 5 * D_MODEL:6 * D_MODEL] * y


def _ffn(idx, x, mod, g, w1, w3, w2):
    resident = dict(pipeline_mode=pl.Buffered(1))
    return pl.pallas_call(
        _ffn_kernel,
        grid=(N_TOK // TM,),
        in_specs=[pl.BlockSpec((TM, D_MODEL), lambda i: (i, 0)),
                  pl.BlockSpec((None, 1, MOD_W), lambda i: (_mod_row(i), 0, 0)),
                  pl.BlockSpec((1, D_MODEL), lambda i: (0, 0)),
                  pl.BlockSpec((None, D_MODEL, D_FF), lambda i: (idx, 0, 0), **resident),
                  pl.BlockSpec((None, D_MODEL, D_FF), lambda i: (idx, 0, 0), **resident),
                  pl.BlockSpec((None, D_FF, D_MODEL), lambda i: (idx, 0, 0), **resident)],
        out_specs=pl.BlockSpec((TM, D_MODEL), lambda i: (i, 0)),
        out_shape=jax.ShapeDtypeStruct((N_TOK, D_MODEL), jnp.float32),
        compiler_params=_cparams("arbitrary"),
        name="ffn_dense",
    )(x, mod, g, w1, w3, w2)


TMOE = 512
MOE_ROWS = TOP_K * N_TOK + N_EXPERTS * TMOE
MOE_TILES = MOE_ROWS // TMOE
PACK_W = D_MODEL // 2
SC_WINDOW = 64


def _pack_bf16_pairs(a):
    w = a.shape[1] // 2
    bits = lax.bitcast_convert_type(a.astype(jnp.bfloat16).astype(jnp.float32), jnp.uint32)
    return (bits[:, w:] & jnp.uint32(0xFFFF0000)) | (bits[:, 0:w] >> 16)


def _unpack_bf16_pairs(p):
    lo = lax.bitcast_convert_type(p << 16, jnp.float32)
    hi = lax.bitcast_convert_type(p & jnp.uint32(0xFFFF0000), jnp.float32)
    return jnp.concatenate([lo, hi], axis=1)


def _route_kernel(x_ref, mod_ref, g_ref, r_ref, h_ref, gate_ref, sel_ref):
    h = _modulated_norm(x_ref[...], g_ref[...], mod_ref[:, 3 * D_MODEL:4 * D_MODEL],
                        mod_ref[:, 4 * D_MODEL:5 * D_MODEL])
    h_ref[...] = _pack_bf16_pairs(h)
    logits = _mm_x3(h, r_ref[...])
    lane = lax.broadcasted_iota(jnp.int32, logits.shape, 1)
    v1 = jnp.max(logits, axis=-1, keepdims=True)
    i1 = jnp.min(jnp.where(logits == v1, lane, N_EXPERTS), axis=-1, keepdims=True)
    rest = jnp.where(lane == i1, -jnp.inf, logits)
    v2 = jnp.max(rest, axis=-1, keepdims=True)
    i2 = jnp.min(jnp.where(rest == v2, lane, N_EXPERTS), axis=-1, keepdims=True)
    p2 = jnp.exp(v2 - v1)
    den = 1.0 + p2
    gate_ref[...] = jnp.where(lane == i1, 1.0 / den, 0.0) + jnp.where(lane == i2, p2 / den, 0.0)
    sel_ref[...] = jnp.where((lane == i1) | (lane == i2), 1, 0)


def _route(x, mod, g, router):
    return pl.pallas_call(
        _route_kernel,
        grid=(N_TOK // TM,),
        in_specs=[pl.BlockSpec((TM, D_MODEL), lambda i: (i, 0)),
                  pl.BlockSpec((None, 1, MOD_W), lambda i: (_mod_row(i), 0, 0)),
                  pl.BlockSpec((1, D_MODEL), lambda i: (0, 0)),
                  pl.BlockSpec((D_MODEL, N_EXPERTS), lambda i: (0, 0))],
        out_specs=[pl.BlockSpec((TM, PACK_W), lambda i: (i, 0)),
                   pl.BlockSpec((TM, N_EXPERTS), lambda i: (i, 0)),
                   pl.BlockSpec((TM, N_EXPERTS), lambda i: (i, 0))],
        out_shape=[jax.ShapeDtypeStruct((N_TOK, PACK_W), jnp.uint32),
                   jax.ShapeDtypeStruct((N_TOK, N_EXPERTS), jnp.float32),
                   jax.ShapeDtypeStruct((N_TOK, N_EXPERTS), jnp.int32)],
        compiler_params=_cparams("arbitrary"),
        name="moe_route",
    )(x, mod, g, router)


def _dispatch_plan(gates, sel):
    rank = jnp.cumsum(sel, axis=0) - sel
    count = jnp.sum(sel, axis=0)
    padded = -(-count // TMOE) * TMOE
    seg_end = jnp.cumsum(padded)
    pos = (seg_end - padded)[None, :] + rank
    chosen = sel > 0
    pos_lo = jnp.min(jnp.where(chosen, pos, MOE_ROWS), axis=1)
    pos_hi = jnp.max(jnp.where(chosen, pos, -1), axis=1)
    w_lo = jnp.sum(jnp.where(chosen & (pos == pos_lo[:, None]), gates, 0.0), axis=1)
    w_hi = jnp.sum(jnp.where(chosen & (pos == pos_hi[:, None]), gates, 0.0), axis=1)
    fill = jnp.arange(TMOE, dtype=jnp.int32)[None, :]
    pad_dest = jnp.where(fill < (padded - count)[:, None], (seg_end - padded + count)[:, None] + fill, MOE_ROWS)
    dest = jnp.concatenate([pos_lo, pos_hi, pad_dest.reshape(-1)]).astype(jnp.int32)
    tile_row = jnp.arange(MOE_TILES, dtype=jnp.int32) * TMOE
    tile_expert = jnp.minimum(jnp.sum(seg_end[None, :] <= tile_row[:, None], axis=1), N_EXPERTS - 1)
    return (dest, jnp.stack([pos_lo, pos_hi]).astype(jnp.int32), jnp.stack([w_lo, w_hi], axis=1),
            tile_expert.astype(jnp.int32), (seg_end[-1:] // TMOE).astype(jnp.int32))


def _scatter_rows(table, dest):
    k, m = dest.shape[0], table.shape[0]
    mesh = plsc.VectorSubcoreMesh(core_axis_name="core", subcore_axis_name="subcore")

    @functools.partial(pl.kernel, out_type=jax.ShapeDtypeStruct((MOE_ROWS + SC_WINDOW, PACK_W), table.dtype),
                       mesh=mesh, scratch_types=[])
    def scatter(x_hbm, i_hbm, o_hbm):
        def body(x_vmem, i_vmem):
            pltpu.sync_copy(x_vmem, o_hbm.at[i_vmem.at[0, pl.ds(0, SC_WINDOW)]])

        pltpu.emit_pipeline(
            body,
            grid=(k // SC_WINDOW,),
            in_specs=[pl.BlockSpec((SC_WINDOW, PACK_W), index_map=lambda i: (i % (m // SC_WINDOW), 0)),
                      pl.BlockSpec((1, 2 * SC_WINDOW), index_map=lambda i: (i, 0))],
            out_specs=[],
            core_axis_name=('core', 'subcore'),
            dimension_semantics=(pltpu.PARALLEL,),
        )(x_hbm, i_hbm)

    idx_rows = dest.reshape(k // SC_WINDOW, SC_WINDOW)
    return scatter(table, jnp.concatenate([idx_rows, idx_rows], axis=1))


def _gather_rows(table, idx):
    k = idx.shape[0]
    mesh = plsc.VectorSubcoreMesh(core_axis_name="core", subcore_axis_name="subcore")

    @functools.partial(pl.kernel, out_type=jax.ShapeDtypeStruct((k, PACK_W), table.dtype), mesh=mesh)
    def gather(x_hbm, i_hbm, o_hbm):
        def body(i_vmem, o_vmem):
            pltpu.sync_copy(x_hbm.at[i_vmem.at[0, pl.ds(0, SC_WINDOW)]], o_vmem)

        pltpu.emit_pipeline(
            body,
            grid=(k // SC_WINDOW,),
            in_specs=[pl.BlockSpec((1, 2 * SC_WINDOW), index_map=lambda i: (i, 0))],
            out_specs=[pl.BlockSpec((SC_WINDOW, PACK_W), index_map=lambda i: (i, 0))],
            core_axis_name=('core', 'subcore'),
            dimension_semantics=(pltpu.PARALLEL,),
        )(i_hbm, o_hbm)

    idx_rows = idx.reshape(k // SC_WINDOW, SC_WINDOW)
    return gather(table, jnp.concatenate([idx_rows, idx_rows], axis=1))


def _expert_kernel(te_ref, nu_ref, h_ref, w1_ref, w3_ref, w2_ref, y_ref, w1_s, w3_s, w2_s):
    j = pl.program_id(0)

    @pl.when((j == 0) | (te_ref[j] != te_ref[jnp.maximum(j - 1, 0)]))
    def _():
        w1_s[...] = w1_ref[...].astype(jnp.bfloat16)
        w3_s[...] = w3_ref[...].astype(jnp.bfloat16)
        w2_s[...] = w2_ref[...].astype(jnp.bfloat16)

    @pl.when(j < nu_ref[0])
    def _():
        hb = _unpack_bf16_pairs(h_ref[...]).astype(jnp.bfloat16)
        a = jnp.dot(hb, w1_s[...], preferred_element_type=jnp.float32)
        b = jnp.dot(hb, w3_s[...], preferred_element_type=jnp.float32)
        s = (a * jax.nn.sigmoid(a) * b).astype(jnp.bfloat16)
        y_ref[...] = _pack_bf16_pairs(jnp.dot(s, w2_s[...], preferred_element_type=jnp.float32))

    @pl.when(pl.program_id(0) >= nu_ref[0])
    def _():
        y_ref[...] = jnp.zeros_like(y_ref)


def _experts(idx, tile_expert, n_used, hs, w1, w3, w2):
    wspec = lambda shape: pl.BlockSpec((None, None) + shape, lambda j, te, nu: (idx, te[j], 0, 0))
    return pl.pallas_call(
        _expert_kernel,
        grid_spec=pltpu.PrefetchScalarGridSpec(
            num_scalar_prefetch=2,
            grid=(MOE_TILES,),
            in_specs=[pl.BlockSpec((TMOE, PACK_W), lambda j, te, nu: (j, 0)),
                      wspec((D_MODEL, D_FF_EXPERT)), wspec((D_MODEL, D_FF_EXPERT)), wspec((D_FF_EXPERT, D_MODEL))],
            out_specs=pl.BlockSpec((TMOE, PACK_W), lambda j, te, nu: (j, 0)),
            scratch_shapes=[pltpu.VMEM((D_MODEL, D_FF_EXPERT), jnp.bfloat16),
                            pltpu.VMEM((D_MODEL, D_FF_EXPERT), jnp.bfloat16),
                            pltpu.VMEM((D_FF_EXPERT, D_MODEL), jnp.bfloat16)]),
        out_shape=jax.ShapeDtypeStruct((MOE_ROWS, PACK_W), jnp.uint32),
        compiler_params=_cparams("arbitrary"),
        name="moe_experts",
    )(tile_expert, n_used, hs, w1, w3, w2)


def _combine_kernel(x_ref, mod_ref, ylo_ref, yhi_ref, w_ref, o_ref):
    y = w_ref[:, 0:1] * _unpack_bf16_pairs(ylo_ref[...]) + w_ref[:, 1:2] * _unpack_bf16_pairs(yhi_ref[...])
    o_ref[...] = x_ref[...] + mod_ref[:, 5 * D_MODEL:6 * D_MODEL] * y


def _combine(x, mod, y2, w):
    n_tiles = N_TOK // TM
    return pl.pallas_call(
        _combine_kernel,
        grid=(n_tiles,),
        in_specs=[pl.BlockSpec((TM, D_MODEL), lambda i: (i, 0)),
                  pl.BlockSpec((None, 1, MOD_W), lambda i: (_mod_row(i), 0, 0)),
                  pl.BlockSpec((TM, PACK_W), lambda i: (i, 0)),
                  pl.BlockSpec((TM, PACK_W), lambda i: (n_tiles + i, 0)),
                  pl.BlockSpec((TM, TOP_K), lambda i: (i, 0))],
        out_specs=pl.BlockSpec((TM, D_MODEL), lambda i: (i, 0)),
        out_shape=jax.ShapeDtypeStruct((N_TOK, D_MODEL), jnp.float32),
        compiler_params=_cparams("arbitrary"),
        name="moe_combine",
    )(x, mod, y2, y2, w)


def _moe_routed(idx, x, mod, g, router, w1, w3, w2):
    hp, gates, sel = _route(x, mod, g, router)
    dest, pos, w, tile_expert, n_used = _dispatch_plan(gates, sel)
    ys = _experts(idx, tile_expert, n_used, _scatter_rows(hp, dest), w1, w3, w2)
    return _combine(x, mod, _gather_rows(ys, pos.reshape(-1)), w)


CH = GLA_CHUNK
SLAB = GROUP_WIDTH
N_SLAB_HEADS = SLAB // HEAD_DIM
PREP_CHUNKS = 4
_HI = lax.Precision.HIGHEST


def _linear_consts():
    i = np.arange(CH)[:, None]
    j = (np.arange(SLAB) % CH)[None, :]
    slab = np.stack([i >= j, i > j, i <= j, i < j, i == j]).astype(np.float32)
    r = np.arange(SLAB) // CH
    bd = (r[:, None] == r[None, :]).astype(np.float32)
    t = np.arange(CH)
    tri = np.stack([t[:, None] >= t[None, :], t[:, None] <= t[None, :]]).astype(np.float32)
    return jnp.asarray(slab), jnp.asarray(bd), jnp.asarray(tri, dtype=jnp.bfloat16)


def _bf(x):
    return x.astype(jnp.bfloat16)


def _split2(x):
    hi = _bf(x)
    return hi, _bf(x - hi.astype(jnp.float32))


def _split3_lanes(x):
    hi = _bf(x)
    r1 = x - hi.astype(jnp.float32)
    mid = _bf(r1)
    lo = _bf(r1 - mid.astype(jnp.float32))
    return jnp.concatenate([hi, mid, lo], axis=1)


def _mm(a, b):
    return jnp.dot(a, b, preferred_element_type=jnp.float32)


def _mm_nt(a, b):
    return lax.dot_general(a, b, (((1,), (1,)), ((), ())), preferred_element_type=jnp.float32)


def _mm_tn(a, b):
    return lax.dot_general(a, b, (((0,), (0,)), ((), ())), preferred_element_type=jnp.float32)


def _blockdiag(y_b, bd_b):
    return jnp.concatenate([y_b] * N_SLAB_HEADS, axis=0) * bd_b


def _exact_rows_mm(lhs_b, x):
    c = _mm(lhs_b, _split3_lanes(x))
    return c[:, 0:SLAB] + c[:, SLAB:2 * SLAB] + c[:, 2 * SLAB:3 * SLAB]


def _head_mm3(lhs, y, bd_b):
    m = lhs.shape[0]
    lh, ll = _split2(lhs)
    yh, yl = _split2(y)
    a = _mm(jnp.concatenate([lh, ll], axis=0), _blockdiag(yh, bd_b))
    return a[0:m] + a[m:2 * m] + _mm(lh, _blockdiag(yl, bd_b))


def _unit_triangular_inverses(nmats, eye, bd_b):
    ts = [eye + n for n in nmats]
    ps = [_head_mm3(n, n, bd_b) for n in nmats]
    n_doublings = int(math.log2(CH)) - 1
    for it in range(n_doublings):
        if it < n_doublings - 1:
            res = [_head_mm3(jnp.concatenate([t, p], axis=0), p, bd_b) for t, p in zip(ts, ps)]
            ts = [t + r[0:CH] for t, r in zip(ts, res)]
            ps = [r[CH:2 * CH] for r in res]
        else:
            ts = [t + _head_mm3(t, p, bd_b) for t, p in zip(ts, ps)]
    return ts


def _group_sum(x, bd_f):
    m, w = x.shape
    hi, lo = _split2(x)
    s = _mm(jnp.concatenate([hi, lo], axis=0), _bf(bd_f[0:w, 0:w]))
    return s[0:m] + s[m:2 * m]


def _exact_cols_mm(x, rhs_b):
    m = x.shape[0]
    hi = _bf(x)
    r1 = x - hi.astype(jnp.float32)
    mid = _bf(r1)
    lo = _bf(r1 - mid.astype(jnp.float32))
    s = _mm(jnp.concatenate([hi, mid, lo], axis=0), rhs_b)
    return s[0:m] + s[m:2 * m] + s[2 * m:3 * m]


def _mm_x3(x, w):
    m = x.shape[0]
    xh, xl = _split2(x)
    wh, wl = _split2(w)
    s = _mm(jnp.concatenate([xh, xl], axis=0), wh)
    return s[0:m] + s[m:2 * m] + _mm(xh, wl)


def _softplus(z):
    return jnp.maximum(z, 0.0) + jnp.log1p(jnp.exp(-jnp.abs(z)))


def _silu(z):
    return z * jax.nn.sigmoid(z)


def _own_layer(ref, first_call):
    if not first_call:
        return ref
    for l in range(1, DEPTH):
        ref[l] = jnp.zeros(ref.shape[1:], ref.dtype)
    return ref.at[0]


def _load_head_states(ref, bd_f, transposed):
    out = []
    for d in range(2):
        rows = jnp.concatenate([ref[d, h] for h in range(N_SLAB_HEADS)], axis=0)
        wide = jnp.concatenate([rows.T] * N_SLAB_HEADS, axis=0) * bd_f
        out.append(wide if transposed else wide.T)
    return jnp.concatenate(out, axis=0)


def _store_head_states(ref, st, transposed):
    for d in range(2):
        s = st[d * SLAB:(d + 1) * SLAB]
        if transposed:
            s = s.T
        for h in range(N_SLAB_HEADS):
            ref[d, h] = s[h * HEAD_DIM:(h + 1) * HEAD_DIM, h * HEAD_DIM:(h + 1) * HEAD_DIM]


def _gdn_prepare(qns, kns, vs, betas, gs, slab_ref, bd_b, tri_ref):
    eye = slab_ref[4]
    chains = [(j, d) for j in range(len(qns)) for d in (0, 1)]
    kn_b = [_bf(kn) for kn in kns]
    prods = [_mm_nt(jnp.concatenate([kn_b[j], _bf(qns[j])], axis=0), _blockdiag(kn_b[j], bd_b))
             for j in range(len(qns))]
    ones = jnp.ones((8, CH), jnp.bfloat16)
    cs = [_exact_rows_mm(tri_ref[d], gs[i]) for i, (j, d) in enumerate(chains)]
    r_rows = [_exact_rows_mm(ones, c * eye)[0:1] for c in cs]
    decs = [jnp.exp(jnp.where(slab_ref[2 * d] > 0.5, cs[i] - r_rows[i], 0.0)) * slab_ref[2 * d]
            for i, (j, d) in enumerate(chains)]
    t_invs = _unit_triangular_inverses(
        [-(slab_ref[2 * d + 1] * betas[i] * decs[i] * prods[j][0:CH]) for i, (j, d) in enumerate(chains)], eye, bd_b)
    g_lasts = [cs[i][0:1] if d == 1 else cs[i][CH - 1:CH] for i, (j, d) in enumerate(chains)]
    e_cs = [jnp.exp(c) for c in cs]
    rhs = [jnp.concatenate([_blockdiag(_bf(betas[i] * vs[j]), bd_b),
                            _blockdiag(_bf(betas[i] * e_cs[i] * kns[j]), bd_b)], axis=1)
           for i, (j, d) in enumerate(chains)]
    ws = [_mm(_bf(t), r) for t, r in zip(t_invs, rhs)]
    return ([w[:, 0:SLAB] for w in ws],
            [_bf(w[:, SLAB:2 * SLAB]) for w in ws],
            [_bf(prods[j][CH:2 * CH] * decs[i]) for i, (j, d) in enumerate(chains)],
            [_bf(qns[j] * e_cs[i]) for i, (j, d) in enumerate(chains)],
            [_bf(kns[j] * jnp.exp(g_lasts[i] - cs[i])) for i, (j, d) in enumerate(chains)],
            [jnp.broadcast_to(jnp.exp(g), (8, SLAB)) for g in g_lasts])


def _gdn_kernel(n, has_s0, n_aliased, *refs):
    refs = list(refs)
    x_ref, small_ref = refs[0:2]
    k = 2
    if has_s0:
        s0_ref = refs[k]
        k += 1
    conv_ref, dec_ref, gain_ref, e_ref, slab_ref, bd_ref, tri_ref = refs[k:k + 7]
    k += 7 + n_aliased
    o_ref, sfin_ref, qn_s, kn_s, v_s, g_s, b_s, wv_s, wk_s, qk_s, qd_s, ke_s, eg_s, o_s, st_s = refs[k:]
    nc = n // CH
    bd_f = bd_ref[...]
    bd_b = _bf(bd_f)

    x = x_ref[:, 0:3 * SLAB]
    row = lax.broadcasted_iota(jnp.int32, (n, 1), 0)
    prev = jnp.where(row == 0, 0.0, pltpu.roll(x, 1, axis=0))
    nxt = jnp.where(row == n - 1, 0.0, pltpu.roll(x, n - 1, axis=0))
    y = _silu(prev * conv_ref[0:1, :] + x * conv_ref[1:2, :] + nxt * conv_ref[2:3, :])
    q = y[:, 0:SLAB]
    kk = y[:, SLAB:2 * SLAB]
    qn_s[...] = q * lax.rsqrt(_group_sum(q * q, bd_f) + EPS) * (HEAD_DIM ** -0.5)
    kn_s[...] = kk * lax.rsqrt(_group_sum(kk * kk, bd_f) + EPS)
    v_s[...] = y[:, 2 * SLAB:3 * SLAB]
    sm = small_ref[...]
    lane = lax.broadcasted_iota(jnp.int32, sm.shape, 1)
    decay = -jnp.exp(dec_ref[0:1, :]) * _softplus(sm + dec_ref[1:2, :])
    scal = jnp.where(lane < SMALL_GDN + 2 * GDN_HEADS, decay, jax.nn.sigmoid(sm))
    scal = jnp.where((lane >= SMALL_GDN) & (lane < SMALL_GDN + 4 * GDN_HEADS), scal, 0.0)
    bc = _exact_cols_mm(scal, _bf(e_ref[...]))
    for d in range(2):
        g_s[d] = bc[:, d * SLAB:(d + 1) * SLAB]
        b_s[d] = bc[:, (2 + d) * SLAB:(3 + d) * SLAB]
    if has_s0:
        st_s[...] = _load_head_states(s0_ref, bd_f, transposed=False)
    else:
        st_s[...] = jnp.zeros_like(st_s)

    def prepare(i, carry):
        c0 = i * PREP_CHUNKS
        rows = [pl.ds(pl.multiple_of((c0 + j) * CH, CH), CH) for j in range(PREP_CHUNKS)]
        vals = _gdn_prepare([qn_s[r, :] for r in rows], [kn_s[r, :] for r in rows], [v_s[r, :] for r in rows],
                            [b_s[d, r, :] for r in rows for d in range(2)],
                            [g_s[d, r, :] for r in rows for d in range(2)], slab_ref, bd_b, tri_ref)
        for ref, chain_vals in zip((wv_s, wk_s, qk_s, qd_s, ke_s, eg_s), vals):
            for j in range(PREP_CHUNKS):
                ref[c0 + j] = jnp.concatenate(chain_vals[2 * j:2 * j + 2], axis=0)
        return carry

    lax.fori_loop(0, nc // PREP_CHUNKS, prepare, 0)

    def step(t, carry):
        s_all = st_s[...]
        dirs = (0, 1)
        cc = (t, nc - 1 - t)
        half = (slice(0, CH), slice(CH, 2 * CH))
        s = [s_all[d * SLAB:(d + 1) * SLAB] for d in dirs]
        s_b = [_bf(x) for x in s]
        u_b = [_bf(wv_s[cc[d], half[d], :] - _mm(wk_s[cc[d], half[d], :], s_b[d])) for d in dirs]
        upd = [_mm_tn(ke_s[cc[d], half[d], :], u_b[d]) for d in dirs]
        outs = [_mm(qd_s[cc[d], half[d], :], s_b[d]) + _mm(qk_s[cc[d], half[d], :], _blockdiag(u_b[d], bd_b))
                for d in dirs]
        new_s = [s[d] * eg_s[cc[d], d * 8:d * 8 + 1, :] + bd_f * upd[d] for d in dirs]
        st_s[...] = jnp.concatenate(new_s, axis=0)
        o_s[t] = jnp.concatenate(outs, axis=0)
        return carry

    lax.fori_loop(0, nc, step, 0)
    _store_head_states(_own_layer(sfin_ref, not has_s0 and n_aliased == 0), st_s[...], transposed=False)
    o = jnp.concatenate([o_s[c, 0:CH, :] + o_s[nc - 1 - c, CH:2 * CH, :] for c in range(nc)], axis=0)
    o = o * lax.rsqrt(_group_sum(o * o, bd_f) * (1.0 / HEAD_DIM) + EPS) * gain_ref[...]
    o_ref[...] = o * _silu(x_ref[:, 3 * SLAB:4 * SLAB])


def _gla_prepare(qs, ks, vs, las, slab_ref, bd_f, tri_ref):
    bd_b = _bf(bd_f)
    chains = [(j, d) for j in range(len(qs)) for d in (0, 1)]
    cums = [_exact_rows_mm(tri_ref[d], las[i]) for i, (j, d) in enumerate(chains)]
    lasts = [cums[i][0:1] if d == 1 else cums[i][CH - 1:CH] for i, (j, d) in enumerate(chains)]
    q_dec = [_bf(qs[j] * (HEAD_DIM ** -0.5) * jnp.exp(cums[i])) for i, (j, d) in enumerate(chains)]
    k_inv = [_bf(ks[j] * jnp.exp(-cums[i])) for i, (j, d) in enumerate(chains)]
    k_end = [_bf(ks[j] * jnp.exp(lasts[i] - cums[i])) for i, (j, d) in enumerate(chains)]
    v_b = [_bf(v) for v in vs]
    v_bd = [_blockdiag(v, bd_b) for v in v_b]
    att = [_mm_nt(q_dec[i], _blockdiag(k_inv[i], bd_b)) * slab_ref[2 * d] for i, (j, d) in enumerate(chains)]
    upd = [bd_f * _mm_tn(v_b[j], k_end[i]) for i, (j, d) in enumerate(chains)]
    intra = [_mm(_bf(att[i]), v_bd[j]) for i, (j, d) in enumerate(chains)]
    return intra, q_dec, upd, [jnp.broadcast_to(jnp.exp(l), (8, SLAB)) for l in lasts]


def _gla_kernel(n, has_s0, n_aliased, *refs):
    refs = list(refs)
    x_ref, small_ref = refs[0:2]
    k = 2
    if has_s0:
        s0_ref = refs[k]
        k += 1
    wup_ref, bup_ref, gain_ref, slab_ref, bd_ref, tri_ref = refs[k:k + 6]
    k += 6 + n_aliased
    o_ref, sfin_ref, la_s, oi_s, qd_s, up_s, el_s, o_s, st_s = refs[k:]
    nc = n // CH
    bd_f = bd_ref[...]
    small = small_ref[...]
    for d in range(2):
        z = _mm_x3(small, wup_ref[d]) + bup_ref[d:d + 1, :]
        la_s[d] = (jnp.minimum(z, 0.0) - jnp.log1p(jnp.exp(-jnp.abs(z)))) * (1.0 / GLA_TAU)
    if has_s0:
        st_s[...] = _load_head_states(s0_ref, bd_f, transposed=True)
    else:
        st_s[...] = jnp.zeros_like(st_s)

    def prepare(i, carry):
        c0 = i * PREP_CHUNKS
        rows = [pl.ds(pl.multiple_of((c0 + j) * CH, CH), CH) for j in range(PREP_CHUNKS)]
        vals = _gla_prepare([x_ref[r, 0:SLAB] for r in rows], [x_ref[r, SLAB:2 * SLAB] for r in rows],
                            [x_ref[r, 2 * SLAB:3 * SLAB] for r in rows],
                            [la_s[d, r, :] for r in rows for d in range(2)], slab_ref, bd_f, tri_ref)
        for ref, chain_vals in zip((oi_s, qd_s, up_s, el_s), vals):
            for j in range(PREP_CHUNKS):
                ref[c0 + j] = jnp.concatenate(chain_vals[2 * j:2 * j + 2], axis=0)
        return carry

    lax.fori_loop(0, nc // PREP_CHUNKS, prepare, 0)

    def step(t, carry):
        s_all = st_s[...]
        cc = (t, nc - 1 - t)
        s = [s_all[d * SLAB:(d + 1) * SLAB] for d in range(2)]
        outs = [oi_s[cc[d], d * CH:(d + 1) * CH, :] + _mm_nt(qd_s[cc[d], d * CH:(d + 1) * CH, :], _bf(s[d]))
                for d in range(2)]
        new_s = [s[d] * el_s[cc[d], d * 8:d * 8 + 1, :] + up_s[cc[d], d * SLAB:(d + 1) * SLAB, :] for d in range(2)]
        st_s[...] = jnp.concatenate(new_s, axis=0)
        o_s[t] = jnp.concatenate(outs, axis=0)
        return carry

    lax.fori_loop(0, nc, step, 0)
    _store_head_states(_own_layer(sfin_ref, not has_s0 and n_aliased == 0), st_s[...], transposed=True)
    o = jnp.concatenate([o_s[c, 0:CH, :] + o_s[nc - 1 - c, CH:2 * CH, :] for c in range(nc)], axis=0)
    o = o * lax.rsqrt(_group_sum(o * o, bd_f) * (1.0 / HEAD_DIM) + EPS) * gain_ref[...]
    o_ref[...] = o * _silu(x_ref[:, 3 * SLAB:4 * SLAB])


def _linear_mixer(kind, layer, hp, col_block, s0, params, consts, ctx_states=None):
    latent = s0 is not None
    n = DEC_SEQ if latent else SEQ
    n_seq = DEC_BATCH if latent else BATCH
    row0 = N_PROMPT_TOK // n if latent else 0
    full = lambda a: pl.BlockSpec(a.shape, lambda b: (0,) * a.ndim)
    state_block = (2, N_SLAB_HEADS, HEAD_DIM, HEAD_DIM)
    in_specs = [pl.BlockSpec((n, 4 * SLAB), lambda b: (row0 + b, col_block)),
                pl.BlockSpec((n, LANES), lambda b: (row0 + b, SMALL_COL_BLOCK))]
    args = [hp, hp]
    if latent:
        in_specs.append(pl.BlockSpec((None, None) + state_block, lambda b: (b, layer, 0, 0, 0, 0)))
        args.append(s0)
    for a in tuple(params) + tuple(consts):
        in_specs.append(full(a))
        args.append(a)
    aliases = {}
    if ctx_states is not None:
        aliases = {len(args): 1}
        in_specs.append(pl.BlockSpec(memory_space=pl.ANY))
        args.append(ctx_states)
    if latent:
        state_spec = pl.BlockSpec((None,) + state_block, lambda b: (b, 0, 0, 0, 0))
        state_shape = (n_seq,) + state_block
    elif ctx_states is None:
        assert layer == 0
        state_spec = pl.BlockSpec((None, DEPTH) + state_block, lambda b: (b, 0, 0, 0, 0, 0))
        state_shape = (n_seq, DEPTH) + state_block
    else:
        state_spec = pl.BlockSpec((None, None) + state_block, lambda b: (b, layer, 0, 0, 0, 0))
        state_shape = (n_seq, DEPTH) + state_block
    nc = n // CH
    seq_buf = pltpu.VMEM((n, SLAB), jnp.float32)
    dir_buf = pltpu.VMEM((2, n, SLAB), jnp.float32)
    pair_f32 = pltpu.VMEM((nc, 2 * CH, SLAB), jnp.float32)
    pair_b16 = pltpu.VMEM((nc, 2 * CH, SLAB), jnp.bfloat16)
    state_buf = pltpu.VMEM((2 * SLAB, SLAB), jnp.float32)
    if kind == 'gdn':
        body = functools.partial(_gdn_kernel, n, latent, len(aliases))
        scratch = [seq_buf, seq_buf, seq_buf, dir_buf, dir_buf, pair_f32, pair_b16, pair_b16, pair_b16, pair_b16,
                   pltpu.VMEM((nc, 16, SLAB), jnp.float32), pair_f32, state_buf]
    else:
        body = functools.partial(_gla_kernel, n, latent, len(aliases))
        scratch = [dir_buf, pair_f32, pair_b16, pltpu.VMEM((nc, 2 * SLAB, SLAB), jnp.float32),
                   pltpu.VMEM((nc, 16, SLAB), jnp.float32), pair_f32, state_buf]
    return pl.pallas_call(
        body,
        grid=(n_seq,),
        in_specs=in_specs,
        out_specs=[pl.BlockSpec((n, SLAB), lambda b: (b, 0)), state_spec],
        out_shape=[jax.ShapeDtypeStruct((n_seq * n, SLAB), jnp.float32),
                   jax.ShapeDtypeStruct(state_shape, jnp.float32)],
        scratch_shapes=scratch,
        input_output_aliases=aliases,
        compiler_params=_cparams("arbitrary"),
        name=kind + ("_latent" if latent else "_ctx"),
    )(*args)


ATT_SCALE = HEAD_DIM ** -0.5
NA_ROWS = DEC_SEQ // GRID_W
NA_KROWS = min(NA_KH, NA_ROWS)
NA_WIN = NA_KROWS * GRID_W
N_SWA_BLOCKS = DEC_SEQ // SWA_BLOCK
assert SWA_WINDOW == SWA_BLOCK
assert NA_KROWS % 2 == 0 and NA_ROWS % 2 == 0


def _head_rmsnorm(x, bd_f, gain):
    return x * lax.rsqrt(_group_sum(x * x, bd_f) * (1.0 / HEAD_DIM) + EPS) * gain


def _lane_group(shape):
    return lax.broadcasted_iota(jnp.int32, shape, 1) // HEAD_DIM


def _stack_groups(x):
    grp = _lane_group(x.shape)
    return jnp.concatenate([jnp.where(grp == g, x, jnp.zeros_like(x)) for g in range(x.shape[1] // HEAD_DIM)], axis=0)


def _stack_swa_queries(q_b):
    return jnp.concatenate([_stack_groups(q_b[:, 0:LANES]), _stack_groups(q_b[:, LANES:2 * LANES])], axis=0)


def _unstack_swa(o, m):
    low = _lane_group((m, LANES)) == 0
    return jnp.concatenate([jnp.where(low, o[0:m], o[m:2 * m]), jnp.where(low, o[2 * m:3 * m], o[3 * m:4 * m])],
                           axis=1)


def _unstack_groups(o, m):
    grp = _lane_group((m, o.shape[1]))
    out = jnp.where(grp == 0, o[0:m], 0.0)
    for g in range(1, o.shape[1] // HEAD_DIM):
        out = out + jnp.where(grp == g, o[g * m:(g + 1) * m], 0.0)
    return out


def _softmax_pv_chains(chains):
    ms = []
    for logits, _, extra, _ in chains:
        m = jnp.max(logits[0], axis=-1, keepdims=True)
        for l in logits[1:]:
            m = jnp.maximum(m, jnp.max(l, axis=-1, keepdims=True))
        ms.append(m if extra is None else jnp.maximum(m, extra))
    es = [[jnp.exp(l - m) for l in c[0]] for c, m in zip(chains, ms)]
    dens = []
    for c, m, e in zip(chains, ms, es):
        den = jnp.exp(c[2] - m) if c[2] is not None else 0.0
        for piece in e:
            den = den + jnp.sum(piece, axis=-1, keepdims=True)
        dens.append(den)
    pvs = [0.0] * len(chains)
    for p in range(max(len(c[0]) for c in chains)):
        for ci, c in enumerate(chains):
            if p < len(c[0]):
                transposed = c[3] and p == len(c[0]) - 1
                e_b = _bf(es[ci][p])
                pvs[ci] = pvs[ci] + (_mm_nt(e_b, c[1][p]) if transposed else _mm(e_b, c[1][p]))
    return [pv / den for pv, den in zip(pvs, dens)]


def _sink_column(sink_ref, m):
    return jnp.concatenate([jnp.full((m, 1), sink_ref[g], jnp.float32) for g in range(SWA_HEADS)], axis=0)


def _rope(x, rope_ref):
    reps = x.shape[1] // LANES
    wide = lambda i: jnp.concatenate([rope_ref[i]] * reps, axis=1)
    shift = HEAD_DIM // 4
    return (x * wide(0) + pltpu.roll(x, x.shape[1] - shift, axis=1) * wide(1)
            + pltpu.roll(x, shift, axis=1) * wide(2))


def _store_cache(ref, slab):
    t = slab.T
    for h in range(ref.shape[0]):
        ref[h] = t[h * HEAD_DIM:(h + 1) * HEAD_DIM, :]


def _ctx_attn_kernel(n_aliased, bq_ref, bk_ref, bv_ref, cq_ref, ck_ref, cv_ref, gain_ref, sink_ref, bd_ref, *rest):
    o_ref = rest[n_aliased]
    swak_ref, swav_ref, nak_ref, nav_ref = (_own_layer(r, n_aliased == 0) for r in rest[n_aliased + 1:])
    n = SEQ
    bd_f = bd_ref[...]
    q = _head_rmsnorm(bq_ref[...], bd_f, gain_ref[0:1, :])
    k = _head_rmsnorm(bk_ref[...], bd_f[0:LANES, 0:LANES], gain_ref[1:2, 0:LANES])
    v = bv_ref[...]
    _store_cache(swak_ref, k)
    _store_cache(swav_ref, v)
    q2 = _head_rmsnorm(cq_ref[...], bd_f, gain_ref[2:3, :])
    k2 = _head_rmsnorm(ck_ref[...], bd_f, gain_ref[3:4, :])
    v2 = cv_ref[...]
    _store_cache(nak_ref, k2)
    _store_cache(nav_ref, v2)
    logits_b = _mm_nt(_stack_swa_queries(_bf(q)), _bf(k)) * ATT_SCALE
    logits_c = _mm_nt(_stack_groups(_bf(q2)), _bf(k2)) * ATT_SCALE
    out_b, out_c = _softmax_pv_chains([([logits_b], [_bf(v)], _sink_column(sink_ref, n), False),
                                       ([logits_c], [_bf(v2)], None, False)])
    o_ref[...] = jnp.concatenate([_unstack_swa(out_b, n), _unstack_groups(out_c, n)], axis=1)


def _load_cache_slabs(ck_ref, cv_ref, ck_s, cv_s):
    for src, dst in ((ck_ref, ck_s), (cv_ref, cv_s)):
        dst[...] = _bf(jnp.concatenate([src[h] for h in range(src.shape[0])], axis=0))


def _swa_latent_kernel(q_ref, k_ref, v_ref, ck_ref, cv_ref, rope_ref, gain_ref, sink_ref, bd_ref,
                       o_ref, q_s, k_s, v_s, ck_s, cv_s):
    _load_cache_slabs(ck_ref, cv_ref, ck_s, cv_s)
    bd_f = bd_ref[...]
    q_s[...] = _bf(_rope(_head_rmsnorm(q_ref[...], bd_f, gain_ref[0:1, :]), rope_ref))
    k_s[...] = _bf(_rope(_head_rmsnorm(k_ref[...], bd_f[0:LANES, 0:LANES], gain_ref[1:2, 0:LANES]), rope_ref))
    v_s[...] = _bf(v_ref[...])
    m = SWA_BLOCK
    iq = lax.broadcasted_iota(jnp.int32, (SWA_HEADS * m, m), 0) % m
    jk = lax.broadcasted_iota(jnp.int32, (SWA_HEADS * m, m), 1)
    ok_prev = jk >= iq
    ok_next = jk <= iq
    ok_same = jk >= 0
    sink = _sink_column(sink_ref, m)
    for i in range(N_SWA_BLOCKS):
        lo, hi = max(i - 1, 0), min(i + 1, N_SWA_BLOCKS - 1)
        qs = _stack_swa_queries(q_s[i * m:(i + 1) * m, :])
        allowed = jnp.concatenate([ok_prev if j < i else ok_next if j > i else ok_same for j in range(lo, hi + 1)],
                                  axis=1)
        l_loc = jnp.where(allowed, _mm_nt(qs, k_s[lo * m:(hi + 1) * m, :]) * ATT_SCALE, NEG_INF)
        l_ctx = _mm(qs, ck_s[...]) * ATT_SCALE
        o, = _softmax_pv_chains([([l_loc, l_ctx], [v_s[lo * m:(hi + 1) * m, :], cv_s[...]], sink, True)])
        o_ref[i * m:(i + 1) * m, :] = _unstack_swa(o, m)


def _na_latent_kernel(q_ref, k_ref, v_ref, ck_ref, cv_ref, bias_ref, gain_ref, bd_ref,
                      o_ref, q_s, k_s, v_s, ck_s, cv_s):
    _load_cache_slabs(ck_ref, cv_ref, ck_s, cv_s)
    bd_f = bd_ref[...]
    q_s[...] = _bf(_head_rmsnorm(q_ref[...], bd_f, gain_ref[2:3, :]))
    k_s[...] = _bf(_head_rmsnorm(k_ref[...], bd_f, gain_ref[3:4, :]))
    v_s[...] = _bf(v_ref[...])

    def row_pair(i, carry):
        rows = [2 * i, 2 * i + 1]
        rs = [jnp.clip(r - NA_KROWS // 2, 0, NA_ROWS - NA_KROWS) for r in rows]
        qrows = [pl.ds(pl.multiple_of(r * GRID_W, GRID_W), GRID_W) for r in rows]
        wins = [pl.ds(pl.multiple_of(s * GRID_W, GRID_W), NA_WIN) for s in rs]
        qs = [_stack_groups(q_s[qr, :]) for qr in qrows]
        bias = [jnp.concatenate([bias_ref[s - r + NA_KH - 1 + 2 * p] for p in range(NA_KROWS // 2)], axis=1)
                for r, s in zip(rows, rs)]
        l_loc = [_mm_nt(qs[a], k_s[wins[a], :]) * ATT_SCALE + bias[a] for a in range(2)]
        l_ctx = [_mm(qs[a], ck_s[...]) * ATT_SCALE for a in range(2)]
        m = [jnp.maximum(jnp.max(l_loc[a], axis=-1, keepdims=True), jnp.max(l_ctx[a], axis=-1, keepdims=True))
             for a in range(2)]
        e_loc = [jnp.exp(l_loc[a] - m[a]) for a in range(2)]
        e_ctx = [jnp.exp(l_ctx[a] - m[a]) for a in range(2)]
        den = [jnp.sum(e_loc[a], axis=-1, keepdims=True) + jnp.sum(e_ctx[a], axis=-1, keepdims=True)
               for a in range(2)]
        pv_loc = [_mm(_bf(e_loc[a]), v_s[wins[a], :]) for a in range(2)]
        pv_ctx = [_mm_nt(_bf(e_ctx[a]), cv_s[...]) for a in range(2)]
        for a in range(2):
            o_ref[qrows[a], :] = _unstack_groups((pv_loc[a] + pv_ctx[a]) / den[a], GRID_W)
        return carry

    lax.fori_loop(0, NA_ROWS // 2, row_pair, 0)


def _rope_tables():
    t = jnp.arange(DEC_SEQ)
    rows = (t // GRID_W).astype(jnp.float32)
    cols = (t % GRID_W).astype(jnp.float32)
    half = HEAD_DIM // 2
    nf = half // 2
    inv = 1.0 / (ROPE_BASE ** (jnp.arange(nf, dtype=jnp.float32) / nf))
    d = np.arange(LANES) % HEAD_DIM
    pos = jnp.where(jnp.asarray(d < half)[None, :], rows[:, None], cols[:, None])
    ang = pos * inv[jnp.asarray(d % nf)][None, :]
    first = jnp.asarray((d % half) < nf)[None, :]
    cos, sin = jnp.cos(ang), jnp.sin(ang)
    return jnp.stack([cos, jnp.where(first, -sin, 0.0), jnp.where(first, 0.0, sin)])


def _na_bias_table(rpb):
    col = np.arange(GRID_W)
    cs = np.clip(col - NA_KW // 2, 0, GRID_W - NA_KW)
    col_mask = (col[None, :] >= cs[:, None]) & (col[None, :] < cs[:, None] + NA_KW)
    dc = np.clip(col[None, :] - col[:, None], -(NA_KW - 1), NA_KW - 1) + NA_KW - 1
    pick = (np.arange(2 * NA_KW - 1)[:, None, None] == dc[None]).astype(np.float32)
    b = jnp.einsum('hrd,dqk->hrqk', rpb, jnp.asarray(pick), precision=_HI)
    b = jnp.where(jnp.asarray(col_mask)[None, None], b, NEG_INF)
    pair = jnp.concatenate([b[:, 0:2 * NA_KH - 2], b[:, 1:2 * NA_KH - 1]], axis=-1)
    return pair.transpose(1, 0, 2, 3).reshape(2 * NA_KH - 2, NA_HEADS * GRID_W, 2 * GRID_W)


def _attention_ctx(layer, hp, gains, sink, bd_c, caches):
    cb = lambda w, off: (lambda b: (b, off // w))
    q_w, kv_w = GROUP_WIDTH, SWA_KV_WIDTH
    full = lambda a: pl.BlockSpec(a.shape, lambda b: (0,) * a.ndim)
    if caches is None:
        assert layer == 0
        cache_out = lambda h: pl.BlockSpec((None, DEPTH, h, HEAD_DIM, SEQ), lambda b: (b, 0, 0, 0, 0))
    else:
        cache_out = lambda h: pl.BlockSpec((None, None, h, HEAD_DIM, SEQ), lambda b: (b, layer, 0, 0, 0))
    cache_shape = lambda h: jax.ShapeDtypeStruct((BATCH, DEPTH, h, HEAD_DIM, SEQ), jnp.float32)
    n_in = 9
    prev = [] if caches is None else list(caches)
    return pl.pallas_call(
        functools.partial(_ctx_attn_kernel, len(prev)),
        grid=(BATCH,),
        in_specs=[pl.BlockSpec((SEQ, q_w), cb(q_w, COL_B)),
                  pl.BlockSpec((SEQ, kv_w), cb(kv_w, COL_B + q_w)),
                  pl.BlockSpec((SEQ, kv_w), cb(kv_w, COL_B + q_w + kv_w)),
                  pl.BlockSpec((SEQ, q_w), cb(q_w, COL_C)),
                  pl.BlockSpec((SEQ, q_w), cb(q_w, COL_C + q_w)),
                  pl.BlockSpec((SEQ, q_w), cb(q_w, COL_C + 2 * q_w)),
                  full(gains), pl.BlockSpec(memory_space=pltpu.SMEM), full(bd_c)]
                 + [pl.BlockSpec(memory_space=pl.ANY)] * len(prev),
        out_specs=[pl.BlockSpec((SEQ, 2 * q_w), lambda b: (b, 0)),
                   cache_out(SWA_KV_HEADS), cache_out(SWA_KV_HEADS), cache_out(NA_HEADS), cache_out(NA_HEADS)],
        out_shape=[jax.ShapeDtypeStruct((N_PROMPT_TOK, 2 * q_w), jnp.float32),
                   cache_shape(SWA_KV_HEADS), cache_shape(SWA_KV_HEADS), cache_shape(NA_HEADS), cache_shape(NA_HEADS)],
        input_output_aliases={n_in + i: 1 + i for i in range(len(prev))},
        compiler_params=_cparams("arbitrary"),
        name="attn_ctx",
    )(hp, hp, hp, hp, hp, hp, gains, sink, bd_c, *prev)


def _attention_latent(kind, layer, hp, ck, cv, table, gains, sink, bd_c):
    n = DEC_SEQ
    row0 = N_PROMPT_TOK // n
    q_w = GROUP_WIDTH
    kv_w = SWA_KV_WIDTH if kind == 'swa' else q_w
    col = COL_B if kind == 'swa' else COL_C
    cb = lambda w, off: (lambda b: (row0 + b, off // w))
    full = lambda a: pl.BlockSpec(a.shape, lambda b: (0,) * a.ndim)
    cache_spec = pl.BlockSpec((None, None, kv_w // HEAD_DIM, HEAD_DIM, PAST_LEN), lambda b: (b, layer, 0, 0, 0))
    in_specs = [pl.BlockSpec((n, q_w), cb(q_w, col)),
                pl.BlockSpec((n, kv_w), cb(kv_w, col + q_w)),
                pl.BlockSpec((n, kv_w), cb(kv_w, col + q_w + kv_w)),
                cache_spec, cache_spec,
                full(table), full(gains)]
    args = [hp, hp, hp, ck, cv, table, gains]
    if kind == 'swa':
        in_specs.append(pl.BlockSpec(memory_space=pltpu.SMEM))
        args.append(sink)
    in_specs.append(full(bd_c))
    args.append(bd_c)
    return pl.pallas_call(
        _swa_latent_kernel if kind == 'swa' else _na_latent_kernel,
        grid=(DEC_BATCH,),
        in_specs=in_specs,
        out_specs=pl.BlockSpec((n, q_w), lambda b: (b, 0)),
        out_shape=jax.ShapeDtypeStruct((N_SAMPLE_TOK, q_w), jnp.float32),
        scratch_shapes=[pltpu.VMEM((n, q_w), jnp.bfloat16), pltpu.VMEM((n, kv_w), jnp.bfloat16),
                        pltpu.VMEM((n, kv_w), jnp.bfloat16), pltpu.VMEM((kv_w, PAST_LEN), jnp.bfloat16),
                        pltpu.VMEM((kv_w, PAST_LEN), jnp.bfloat16)],
        compiler_params=_cparams("arbitrary"),
        name=kind + "_latent",
    )(*args)


def kernel(x_prompt, x_sample, cache_swa_k, cache_swa_v, cache_na_k, cache_na_v, state_gla, state_gdn, c, c_ctx, w_mod, b_mod, norm1_g, norm2_g, w_in, w_out, gla_wup, gla_bup, gla_onorm, swa_qnorm, swa_knorm, swa_sink, na_qnorm, na_knorm, na_rpb, gdn_conv, gdn_alog, gdn_dtbias, gdn_onorm, ffn_w1, ffn_w3, ffn_w2, moe_router, moe_w1, moe_w3, moe_w2):
    bf16 = jnp.bfloat16
    x = jnp.concatenate([x_prompt.reshape(N_PROMPT_TOK, D_MODEL), x_sample.reshape(N_SAMPLE_TOK, D_MODEL)], axis=0)
    cvec = jnp.concatenate([c_ctx[None, :], c, jnp.zeros((N_MOD_ROWS - 1 - DEC_BATCH, D_MODEL), jnp.float32)], axis=0)
    mod_all = _modulation_all(cvec, w_mod, b_mod).reshape(DEPTH, N_MOD_ROWS, 1, MOD_W)
    w_in_b = jnp.concatenate([_take_segments(w_in.astype(bf16), _in_col_segments(), 2),
                              jnp.zeros((DEPTH, D_MODEL, IN_COLS_PAD - IN_COLS), bf16)], axis=2)
    w_out_b = _take_segments(w_out.astype(bf16), _OUT_ROW_SEGMENTS, 1)
    rope_tables = _rope_tables()
    consts = _linear_consts()
    tile_heads = lambda g: jnp.tile(g, N_SLAB_HEADS)[None, :]
    small_expand = jnp.asarray(_small_expand_matrix())
    ffn_w = [w.astype(bf16) for w in (ffn_w1, ffn_w3, ffn_w2)]
    moe_w = (moe_w1, moe_w3, moe_w2)
    ctx_kv = [jnp.swapaxes(c_, -1, -2) for c_ in (cache_swa_k, cache_swa_v, cache_na_k, cache_na_v)]

    new_kv = st_a = st_d = None
    for l in range(DEPTH):
        mod = mod_all[l]
        hp = _in_proj(l, x, mod, norm1_g[l][None, :], w_in_b)

        wup = jnp.zeros((2, LANES, SLAB), jnp.float32)
        wup = wup.at[0, 0:GLA_LOWRANK].set(gla_wup[l, 0]).at[1, GLA_LOWRANK:2 * GLA_LOWRANK].set(gla_wup[l, 1])
        gla_params = (wup, gla_bup[l], tile_heads(gla_onorm[l]))
        a_ctx, st_a = _linear_mixer('gla', l, hp, COL_A // (4 * SLAB), None, gla_params, consts, st_a)
        a_lat, _ = _linear_mixer('gla', l, hp, COL_A // (4 * SLAB), state_gla, gla_params, consts)
        decay_cols = slice(SMALL_GDN, SMALL_GDN + 2 * GDN_HEADS)
        decay_params = jnp.zeros((2, LANES), jnp.float32).at[:, decay_cols].set(
            jnp.stack([gdn_alog[l].reshape(-1), gdn_dtbias[l].reshape(-1)]))
        gdn_params = (gdn_conv[l], decay_params, tile_heads(gdn_onorm[l]), small_expand)
        d_ctx, st_d = _linear_mixer('gdn', l, hp, COL_D // (4 * SLAB), None, gdn_params, consts, st_d)
        d_lat, _ = _linear_mixer('gdn', l, hp, COL_D // (4 * SLAB), state_gdn, gdn_params, consts)

        gains = jnp.stack([jnp.tile(g, N_SLAB_HEADS) for g in (swa_qnorm[l], swa_knorm[l], na_qnorm[l], na_knorm[l])])
        sink = swa_sink[l][jnp.asarray(SWA_Q_HEAD_ORDER)]
        bc_ctx, *new_kv = _attention_ctx(l, hp, gains, sink, consts[1], new_kv)
        b_lat = _attention_latent('swa', l, hp, ctx_kv[0], ctx_kv[1], rope_tables, gains, sink, consts[1])
        c_lat = _attention_latent('na', l, hp, ctx_kv[2], ctx_kv[3], _na_bias_table(na_rpb[l]), gains, None,
                                  consts[1])
        x = _out_proj(l, (a_ctx, bc_ctx, d_ctx), (a_lat, b_lat, c_lat, d_lat), x, mod, w_out_b)
        if l % 2 == 0:
            x = _ffn(l // 2, x, mod, norm2_g[l][None, :], *ffn_w)
        else:
            x = _moe_routed(l // 2, x, mod, norm2_g[l][None, :], moe_router[l // 2], *moe_w)

    outs = [jnp.swapaxes(o, -1, -2) for o in new_kv] + [st_a, st_d]
    y_prompt = x[:N_PROMPT_TOK].reshape(BATCH, SEQ, D_MODEL)
    y_sample = x[N_PROMPT_TOK:].reshape(DEC_BATCH, DEC_SEQ, D_MODEL)
    return (y_prompt, y_sample, *outs)
```

```python
import functools
import math

import numpy as np
import jax
import jax.numpy as jnp
from jax import lax
from jax.experimental import pallas as pl
from jax.experimental.pallas import tpu as pltpu
from jax.experimental.pallas import tpu_sc as plsc

D_MODEL = 1024
BATCH = 32
SEQ = 256
DEPTH = 4
DEC_BATCH = 8
DEC_SEQ = 1024
PAST_LEN = 512

GRID_W = 64
HEAD_DIM = 64
GROUP_WIDTH = D_MODEL // 4
GROUP_HEADS = GROUP_WIDTH // HEAD_DIM
GLA_HEADS = GROUP_HEADS
GLA_LOWRANK = 16
GLA_TAU = 16.0
GLA_CHUNK = 64
SWA_HEADS = GROUP_HEADS
SWA_KV_HEADS = 2
SWA_WINDOW = 128
SWA_BLOCK = 128
NA_HEADS = GROUP_HEADS
NA_KH = 8
NA_KW = 16
GDN_HEADS = GROUP_HEADS
GDN_CONV = 3
GDN_CHUNK = 64
D_FF = 2816
N_EXPERTS = 8
TOP_K = 2
D_FF_EXPERT = 1024
ROPE_BASE = 10000.0
EPS = 1e-6
NEG_INF = -1e30
SWA_KV_WIDTH = SWA_KV_HEADS * HEAD_DIM
IN_SPLITS = ([GROUP_WIDTH] * 4 + [GLA_LOWRANK] * 2 + [GROUP_WIDTH, SWA_KV_WIDTH, SWA_KV_WIDTH]
             + [GROUP_WIDTH] * 3 + [GROUP_WIDTH] * 4 + [GDN_HEADS] * 4)
IN_COLS = sum(IN_SPLITS)

LANES = 128
VMEM_LIMIT_BYTES = 56 * 1024 * 1024

N_PROMPT_TOK = BATCH * SEQ
N_SAMPLE_TOK = DEC_BATCH * DEC_SEQ
N_TOK = N_PROMPT_TOK + N_SAMPLE_TOK
N_MOD_ROWS = 16
MOD_W = 6 * D_MODEL
TM = 1024
IN_COLS_PAD = -(-IN_COLS // LANES) * LANES

COL_A = 0
COL_D = 4 * GROUP_WIDTH
COL_C = 8 * GROUP_WIDTH
COL_B = 11 * GROUP_WIDTH
COL_SMALL = COL_B + GROUP_WIDTH + 2 * SWA_KV_WIDTH
SMALL_COL_BLOCK = COL_SMALL // LANES
SMALL_GDN = 2 * GLA_LOWRANK


SWA_Q_HEAD_ORDER = (0, 2, 1, 3)
def _in_col_segments():
    off = [0] + [int(v) for v in np.cumsum(IN_SPLITS)]
    seg = lambda a, b: [(off[a], off[b])]
    swa_q = [(off[6] + h * HEAD_DIM, off[6] + (h + 1) * HEAD_DIM) for h in SWA_Q_HEAD_ORDER]
    return seg(0, 4) + seg(12, 16) + seg(9, 12) + swa_q + seg(7, 9) + seg(4, 6) + seg(16, 20)


_OUT_ROW_SEGMENTS = ([(0, GROUP_WIDTH)]
                     + [(GROUP_WIDTH + h * HEAD_DIM, GROUP_WIDTH + (h + 1) * HEAD_DIM) for h in SWA_Q_HEAD_ORDER]
                     + [(2 * GROUP_WIDTH, D_MODEL)])


def _take_segments(w, segments, axis):
    idx = [slice(None)] * w.ndim
    parts = []
    for a, b in segments:
        idx[axis] = slice(a, b)
        parts.append(w[tuple(idx)])
    return jnp.concatenate(parts, axis=axis)


def _small_expand_matrix():
    e = np.zeros((LANES, 4 * GROUP_WIDTH), np.float32)
    for s in range(4):
        for h in range(GROUP_HEADS):
            e[SMALL_GDN + s * GROUP_HEADS + h, s * GROUP_WIDTH + h * HEAD_DIM:s * GROUP_WIDTH + (h + 1) * HEAD_DIM] = 1.0
    return e


def _mod_row(i):
    n_prompt_tiles = N_PROMPT_TOK // TM
    return jnp.where(i < n_prompt_tiles, 0, 1 + (i - n_prompt_tiles) // (DEC_SEQ // TM))


def _cparams(*sem):
    return pltpu.CompilerParams(dimension_semantics=sem, vmem_limit_bytes=VMEM_LIMIT_BYTES)


def _mod_kernel(c_ref, w_ref, b_ref, o_ref):
    c = c_ref[...]
    s = c * jax.nn.sigmoid(c)
    o_ref[...] = jnp.dot(s, w_ref[...], preferred_element_type=jnp.float32,
                         precision=lax.Precision.HIGHEST) + b_ref[...]


def _modulation_all(cvec, w_mod, b_mod):
    tn = 1536
    return pl.pallas_call(
        _mod_kernel,
        grid=(DEPTH, MOD_W // tn),
        in_specs=[pl.BlockSpec((N_MOD_ROWS, D_MODEL), lambda l, j: (0, 0)),
                  pl.BlockSpec((None, D_MODEL, tn), lambda l, j: (l, 0, j)),
                  pl.BlockSpec((None, 1, tn), lambda l, j: (l, 0, j))],
        out_specs=pl.BlockSpec((None, N_MOD_ROWS, tn), lambda l, j: (l, 0, j)),
        out_shape=jax.ShapeDtypeStruct((DEPTH, N_MOD_ROWS, MOD_W), jnp.float32),
        compiler_params=_cparams("arbitrary", "arbitrary"),
        name="modulation",
    )(cvec, w_mod, b_mod.reshape(DEPTH, 1, MOD_W))


def _modulated_norm(x, g, shift, scale):
    y = x * lax.rsqrt(jnp.mean(x * x, axis=-1, keepdims=True) + EPS) * g
    return y * (1.0 + scale) + shift


def _in_proj_kernel(x_ref, mod_ref, g_ref, w_ref, o_ref):
    h = _modulated_norm(x_ref[...], g_ref[...], mod_ref[:, 0:D_MODEL], mod_ref[:, D_MODEL:2 * D_MODEL])
    o_ref[...] = jnp.dot(h.astype(jnp.bfloat16), w_ref[...], preferred_element_type=jnp.float32)


def _in_proj(layer, x, mod, g, w):
    return pl.pallas_call(
        _in_proj_kernel,
        grid=(N_TOK // TM,),
        in_specs=[pl.BlockSpec((TM, D_MODEL), lambda i: (i, 0)),
                  pl.BlockSpec((None, 1, MOD_W), lambda i: (_mod_row(i), 0, 0)),
                  pl.BlockSpec((1, D_MODEL), lambda i: (0, 0)),
                  pl.BlockSpec((None, D_MODEL, IN_COLS_PAD), lambda i: (layer, 0, 0),
                               pipeline_mode=pl.Buffered(1))],
        out_specs=pl.BlockSpec((TM, IN_COLS_PAD), lambda i: (i, 0)),
        out_shape=jax.ShapeDtypeStruct((N_TOK, IN_COLS_PAD), jnp.float32),
        compiler_params=_cparams("arbitrary"),
        name="in_proj",
    )(x, mod, g, w)


N_CTX_TILES = N_PROMPT_TOK // TM


def _out_proj_kernel(a_c, b_c, d_c, a_l, b_l, c_l, d_l, x_ref, mod_ref, w_ref, o_ref):
    is_ctx = pl.program_id(0) < N_CTX_TILES
    ctx = jnp.concatenate([a_c[...], b_c[...], d_c[...]], axis=1)
    lat = jnp.concatenate([a_l[...], b_l[...], c_l[...], d_l[...]], axis=1)
    mix = jnp.where(is_ctx, ctx, lat)
    y = jnp.dot(mix.astype(jnp.bfloat16), w_ref[...], preferred_element_type=jnp.float32)
    o_ref[...] = x_ref[...] + mod_ref[:, 2 * D_MODEL:3 * D_MODEL] * y


def _out_proj(layer, ctx_outs, lat_outs, x, mod, w):
    ctx_map = lambda i: (jnp.minimum(i, N_CTX_TILES - 1), 0)
    lat_map = lambda i: (jnp.maximum(i - N_CTX_TILES, 0), 0)
    return pl.pallas_call(
        _out_proj_kernel,
        grid=(N_TOK // TM,),
        in_specs=[pl.BlockSpec((TM, a.shape[1]), ctx_map) for a in ctx_outs]
                 + [pl.BlockSpec((TM, a.shape[1]), lat_map) for a in lat_outs]
                 + [pl.BlockSpec((TM, D_MODEL), lambda i: (i, 0)),
                    pl.BlockSpec((None, 1, MOD_W), lambda i: (_mod_row(i), 0, 0)),
                    pl.BlockSpec((None, D_MODEL, D_MODEL), lambda i: (layer, 0, 0), pipeline_mode=pl.Buffered(1))],
        out_specs=pl.BlockSpec((TM, D_MODEL), lambda i: (i, 0)),
        out_shape=jax.ShapeDtypeStruct((N_TOK, D_MODEL), jnp.float32),
        compiler_params=_cparams("arbitrary"),
        name="out_proj",
    )(*ctx_outs, *lat_outs, x, mod, w)


FF_CHUNK = D_FF // 2


def _ffn_kernel(x_ref, mod_ref, g_ref, w1_ref, w3_ref, w2_ref, o_ref):
    x = x_ref[...]
    h = _modulated_norm(x, g_ref[...], mod_ref[:, 3 * D_MODEL:4 * D_MODEL], mod_ref[:, 4 * D_MODEL:5 * D_MODEL])
    hb = h.astype(jnp.bfloat16)
    y = jnp.zeros((TM, D_MODEL), jnp.float32)
    for c0 in range(0, D_FF, FF_CHUNK):
        a = jnp.dot(hb, w1_ref[:, c0:c0 + FF_CHUNK], preferred_element_type=jnp.float32)
        b = jnp.dot(hb, w3_ref[:, c0:c0 + FF_CHUNK], preferred_element_type=jnp.float32)
        s = (a * jax.nn.sigmoid(a) * b).astype(jnp.bfloat16)
        y = y + jnp.dot(s, w2_ref[c0:c0 + FF_CHUNK, :], preferred_element_type=jnp.float32)
    o_ref[...] = x + mod_ref[:, 5 * D_MODEL:6 * D_MODEL] * y


def _ffn(idx, x, mod, g, w1, w3, w2):
    resident = dict(pipeline_mode=pl.Buffered(1))
    return pl.pallas_call(
        _ffn_kernel,
        grid=(N_TOK // TM,),
        in_specs=[pl.BlockSpec((TM, D_MODEL), lambda i: (i, 0)),
                  pl.BlockSpec((None, 1, MOD_W), lambda i: (_mod_row(i), 0, 0)),
                  pl.BlockSpec((1, D_MODEL), lambda i: (0, 0)),
                  pl.BlockSpec((None, D_MODEL, D_FF), lambda i: (idx, 0, 0), **resident),
                  pl.BlockSpec((None, D_MODEL, D_FF), lambda i: (idx, 0, 0), **resident),
                  pl.BlockSpec((None, D_FF, D_MODEL), lambda i: (idx, 0, 0), **resident)],
        out_specs=pl.BlockSpec((TM, D_MODEL), lambda i: (i, 0)),
        out_shape=jax.ShapeDtypeStruct((N_TOK, D_MODEL), jnp.float32),
        compiler_params=_cparams("arbitrary"),
        name="ffn_dense",
    )(x, mod, g, w1, w3, w2)


TMOE = 512
MOE_ROWS = TOP_K * N_TOK + N_EXPERTS * TMOE
MOE_TILES = MOE_ROWS // TMOE
PACK_W = D_MODEL // 2
SC_WINDOW = 64


def _pack_bf16_pairs(a):
    w = a.shape[1] // 2
    bits = lax.bitcast_convert_type(a.astype(jnp.bfloat16).astype(jnp.float32), jnp.uint32)
    return (bits[:, w:] & jnp.uint32(0xFFFF0000)) | (bits[:, 0:w] >> 16)


def _unpack_bf16_pairs(p):
    lo = lax.bitcast_convert_type(p << 16, jnp.float32)
    hi = lax.bitcast_convert_type(p & jnp.uint32(0xFFFF0000), jnp.float32)
    return jnp.concatenate([lo, hi], axis=1)


def _route_kernel(x_ref, mod_ref, g_ref, r_ref, h_ref, gate_ref, sel_ref):
    h = _modulated_norm(x_ref[...], g_ref[...], mod_ref[:, 3 * D_MODEL:4 * D_MODEL],
                        mod_ref[:, 4 * D_MODEL:5 * D_MODEL])
    h_ref[...] = _pack_bf16_pairs(h)
    logits = _mm_x3(h, r_ref[...])
    lane = lax.broadcasted_iota(jnp.int32, logits.shape, 1)
    v1 = jnp.max(logits, axis=-1, keepdims=True)
    i1 = jnp.min(jnp.where(logits == v1, lane, N_EXPERTS), axis=-1, keepdims=True)
    rest = jnp.where(lane == i1, -jnp.inf, logits)
    v2 = jnp.max(rest, axis=-1, keepdims=True)
    i2 = jnp.min(jnp.where(rest == v2, lane, N_EXPERTS), axis=-1, keepdims=True)
    p2 = jnp.exp(v2 - v1)
    den = 1.0 + p2
    gate_ref[...] = jnp.where(lane == i1, 1.0 / den, 0.0) + jnp.where(lane == i2, p2 / den, 0.0)
    sel_ref[...] = jnp.where((lane == i1) | (lane == i2), 1, 0)


def _route(x, mod, g, router):
    return pl.pallas_call(
        _route_kernel,
        grid=(N_TOK // TM,),
        in_specs=[pl.BlockSpec((TM, D_MODEL), lambda i: (i, 0)),
                  pl.BlockSpec((None, 1, MOD_W), lambda i: (_mod_row(i), 0, 0)),
                  pl.BlockSpec((1, D_MODEL), lambda i: (0, 0)),
                  pl.BlockSpec((D_MODEL, N_EXPERTS), lambda i: (0, 0))],
        out_specs=[pl.BlockSpec((TM, PACK_W), lambda i: (i, 0)),
                   pl.BlockSpec((TM, N_EXPERTS), lambda i: (i, 0)),
                   pl.BlockSpec((TM, N_EXPERTS), lambda i: (i, 0))],
        out_shape=[jax.ShapeDtypeStruct((N_TOK, PACK_W), jnp.uint32),
                   jax.ShapeDtypeStruct((N_TOK, N_EXPERTS), jnp.float32),
                   jax.ShapeDtypeStruct((N_TOK, N_EXPERTS), jnp.int32)],
        compiler_params=_cparams("arbitrary"),
        name="moe_route",
    )(x, mod, g, router)


def _dispatch_plan(gates, sel):
    rank = jnp.cumsum(sel, axis=0) - sel
    count = jnp.sum(sel, axis=0)
    padded = -(-count // TMOE) * TMOE
    seg_end = jnp.cumsum(padded)
    pos = (seg_end - padded)[None, :] + rank
    chosen = sel > 0
    pos_lo = jnp.min(jnp.where(chosen, pos, MOE_ROWS), axis=1)
    pos_hi = jnp.max(jnp.where(chosen, pos, -1), axis=1)
    w_lo = jnp.sum(jnp.where(chosen & (pos == pos_lo[:, None]), gates, 0.0), axis=1)
    w_hi = jnp.sum(jnp.where(chosen & (pos == pos_hi[:, None]), gates, 0.0), axis=1)
    fill = jnp.arange(TMOE, dtype=jnp.int32)[None, :]
    pad_dest = jnp.where(fill < (padded - count)[:, None], (seg_end - padded + count)[:, None] + fill, MOE_ROWS)
    dest = jnp.concatenate([pos_lo, pos_hi, pad_dest.reshape(-1)]).astype(jnp.int32)
    tile_row = jnp.arange(MOE_TILES, dtype=jnp.int32) * TMOE
    tile_expert = jnp.minimum(jnp.sum(seg_end[None, :] <= tile_row[:, None], axis=1), N_EXPERTS - 1)
    return (dest, jnp.stack([pos_lo, pos_hi]).astype(jnp.int32), jnp.stack([w_lo, w_hi], axis=1),
            tile_expert.astype(jnp.int32), (seg_end[-1:] // TMOE).astype(jnp.int32))


def _scatter_rows(table, dest):
    k, m = dest.shape[0], table.shape[0]
    mesh = plsc.VectorSubcoreMesh(core_axis_name="core", subcore_axis_name="subcore")

    @functools.partial(pl.kernel, out_type=jax.ShapeDtypeStruct((MOE_ROWS + SC_WINDOW, PACK_W), table.dtype),
                       mesh=mesh, scratch_types=[])
    def scatter(x_hbm, i_hbm, o_hbm):
        def body(x_vmem, i_vmem):
            pltpu.sync_copy(x_vmem, o_hbm.at[i_vmem.at[0, pl.ds(0, SC_WINDOW)]])

        pltpu.emit_pipeline(
            body,
            grid=(k // SC_WINDOW,),
            in_specs=[pl.BlockSpec((SC_WINDOW, PACK_W), index_map=lambda i: (i % (m // SC_WINDOW), 0)),
                      pl.BlockSpec((1, 2 * SC_WINDOW), index_map=lambda i: (i, 0))],
            out_specs=[],
            core_axis_name=('core', 'subcore'),
            dimension_semantics=(pltpu.PARALLEL,),
        )(x_hbm, i_hbm)

    idx_rows = dest.reshape(k // SC_WINDOW, SC_WINDOW)
    return scatter(table, jnp.concatenate([idx_rows, idx_rows], axis=1))


def _gather_rows(table, idx):
    k = idx.shape[0]
    mesh = plsc.VectorSubcoreMesh(core_axis_name="core", subcore_axis_name="subcore")

    @functools.partial(pl.kernel, out_type=jax.ShapeDtypeStruct((k, PACK_W), table.dtype), mesh=mesh)
    def gather(x_hbm, i_hbm, o_hbm):
        def body(i_vmem, o_vmem):
            pltpu.sync_copy(x_hbm.at[i_vmem.at[0, pl.ds(0, SC_WINDOW)]], o_vmem)

        pltpu.emit_pipeline(
            body,
            grid=(k // SC_WINDOW,),
            in_specs=[pl.BlockSpec((1, 2 * SC_WINDOW), index_map=lambda i: (i, 0))],
            out_specs=[pl.BlockSpec((SC_WINDOW, PACK_W), index_map=lambda i: (i, 0))],
            core_axis_name=('core', 'subcore'),
            dimension_semantics=(pltpu.PARALLEL,),
        )(i_hbm, o_hbm)

    idx_rows = idx.reshape(k // SC_WINDOW, SC_WINDOW)
    return gather(table, jnp.concatenate([idx_rows, idx_rows], axis=1))


def _expert_kernel(te_ref, nu_ref, h_ref, w1_ref, w3_ref, w2_ref, y_ref, w1_s, w3_s, w2_s):
    j = pl.program_id(0)

    @pl.when((j == 0) | (te_ref[j] != te_ref[jnp.maximum(j - 1, 0)]))
    def _():
        w1_s[...] = w1_ref[...].astype(jnp.bfloat16)
        w3_s[...] = w3_ref[...].astype(jnp.bfloat16)
        w2_s[...] = w2_ref[...].astype(jnp.bfloat16)

    @pl.when(j < nu_ref[0])
    def _():
        hb = _unpack_bf16_pairs(h_ref[...]).astype(jnp.bfloat16)
        a = jnp.dot(hb, w1_s[...], preferred_element_type=jnp.float32)
        b = jnp.dot(hb, w3_s[...], preferred_element_type=jnp.float32)
        s = (a * jax.nn.sigmoid(a) * b).astype(jnp.bfloat16)
        y_ref[...] = _pack_bf16_pairs(jnp.dot(s, w2_s[...], preferred_element_type=jnp.float32))

    @pl.when(pl.program_id(0) >= nu_ref[0])
    def _():
        y_ref[...] = jnp.zeros_like(y_ref)


def _experts(idx, tile_expert, n_used, hs, w1, w3, w2):
    wspec = lambda shape: pl.BlockSpec((None, None) + shape, lambda j, te, nu: (idx, te[j], 0, 0))
    return pl.pallas_call(
        _expert_kernel,
        grid_spec=pltpu.PrefetchScalarGridSpec(
            num_scalar_prefetch=2,
            grid=(MOE_TILES,),
            in_specs=[pl.BlockSpec((TMOE, PACK_W), lambda j, te, nu: (j, 0)),
                      wspec((D_MODEL, D_FF_EXPERT)), wspec((D_MODEL, D_FF_EXPERT)), wspec((D_FF_EXPERT, D_MODEL))],
            out_specs=pl.BlockSpec((TMOE, PACK_W), lambda j, te, nu: (j, 0)),
            scratch_shapes=[pltpu.VMEM((D_MODEL, D_FF_EXPERT), jnp.bfloat16),
                            pltpu.VMEM((D_MODEL, D_FF_EXPERT), jnp.bfloat16),
                            pltpu.VMEM((D_FF_EXPERT, D_MODEL), jnp.bfloat16)]),
        out_shape=jax.ShapeDtypeStruct((MOE_ROWS, PACK_W), jnp.uint32),
        compiler_params=_cparams("arbitrary"),
        name="moe_experts",
    )(tile_expert, n_used, hs, w1, w3, w2)


def _combine_kernel(x_ref, mod_ref, ylo_ref, yhi_ref, w_ref, o_ref):
    y = w_ref[:, 0:1] * _unpack_bf16_pairs(ylo_ref[...]) + w_ref[:, 1:2] * _unpack_bf16_pairs(yhi_ref[...])
    o_ref[...] = x_ref[...] + mod_ref[:, 5 * D_MODEL:6 * D_MODEL] * y


def _combine(x, mod, y2, w):
    n_tiles = N_TOK // TM
    return pl.pallas_call(
        _combine_kernel,
        grid=(n_tiles,),
        in_specs=[pl.BlockSpec((TM, D_MODEL), lambda i: (i, 0)),
                  pl.BlockSpec((None, 1, MOD_W), lambda i: (_mod_row(i), 0, 0)),
                  pl.BlockSpec((TM, PACK_W), lambda i: (i, 0)),
                  pl.BlockSpec((TM, PACK_W), lambda i: (n_tiles + i, 0)),
                  pl.BlockSpec((TM, TOP_K), lambda i: (i, 0))],
        out_specs=pl.BlockSpec((TM, D_MODEL), lambda i: (i, 0)),
        out_shape=jax.ShapeDtypeStruct((N_TOK, D_MODEL), jnp.float32),
        compiler_params=_cparams("arbitrary"),
        name="moe_combine",
    )(x, mod, y2, y2, w)


def _moe_routed(idx, x, mod, g, router, w1, w3, w2):
    hp, gates, sel = _route(x, mod, g, router)
    dest, pos, w, tile_expert, n_used = _dispatch_plan(gates, sel)
    ys = _experts(idx, tile_expert, n_used, _scatter_rows(hp, dest), w1, w3, w2)
    return _combine(x, mod, _gather_rows(ys, pos.reshape(-1)), w)


CH = GLA_CHUNK
SLAB = GROUP_WIDTH
N_SLAB_HEADS = SLAB // HEAD_DIM
PREP_CHUNKS = 4
_HI = lax.Precision.HIGHEST


def _linear_consts():
    i = np.arange(CH)[:, None]
    j = (np.arange(SLAB) % CH)[None, :]
    slab = np.stack([i >= j, i > j, i <= j, i < j, i == j]).astype(np.float32)
    r = np.arange(SLAB) // CH
    bd = (r[:, None] == r[None, :]).astype(np.float32)
    t = np.arange(CH)
    tri = np.stack([t[:, None] >= t[None, :], t[:, None] <= t[None, :]]).astype(np.float32)
    return jnp.asarray(slab), jnp.asarray(bd), jnp.asarray(tri, dtype=jnp.bfloat16)


def _bf(x):
    return x.astype(jnp.bfloat16)


def _split2(x):
    hi = _bf(x)
    return hi, _bf(x - hi.astype(jnp.float32))


def _split3_lanes(x):
    hi = _bf(x)
    r1 = x - hi.astype(jnp.float32)
    mid = _bf(r1)
    lo = _bf(r1 - mid.astype(jnp.float32))
    return jnp.concatenate([hi, mid, lo], axis=1)


def _mm(a, b):
    return jnp.dot(a, b, preferred_element_type=jnp.float32)


def _mm_nt(a, b):
    return lax.dot_general(a, b, (((1,), (1,)), ((), ())), preferred_element_type=jnp.float32)


def _mm_tn(a, b):
    return lax.dot_general(a, b, (((0,), (0,)), ((), ())), preferred_element_type=jnp.float32)


def _blockdiag(y_b, bd_b):
    return jnp.concatenate([y_b] * N_SLAB_HEADS, axis=0) * bd_b


def _exact_rows_mm(lhs_b, x):
    c = _mm(lhs_b, _split3_lanes(x))
    return c[:, 0:SLAB] + c[:, SLAB:2 * SLAB] + c[:, 2 * SLAB:3 * SLAB]


def _head_mm3(lhs, y, bd_b):
    m = lhs.shape[0]
    lh, ll = _split2(lhs)
    yh, yl = _split2(y)
    a = _mm(jnp.concatenate([lh, ll], axis=0), _blockdiag(yh, bd_b))
    return a[0:m] + a[m:2 * m] + _mm(lh, _blockdiag(yl, bd_b))


def _unit_triangular_inverses(nmats, eye, bd_b):
    ts = [eye + n for n in nmats]
    ps = [_head_mm3(n, n, bd_b) for n in nmats]
    n_doublings = int(math.log2(CH)) - 1
    for it in range(n_doublings):
        if it < n_doublings - 1:
            res = [_head_mm3(jnp.concatenate([t, p], axis=0), p, bd_b) for t, p in zip(ts, ps)]
            ts = [t + r[0:CH] for t, r in zip(ts, res)]
            ps = [r[CH:2 * CH] for r in res]
        else:
            ts = [t + _head_mm3(t, p, bd_b) for t, p in zip(ts, ps)]
    return ts


def _group_sum(x, bd_f):
    m, w = x.shape
    hi, lo = _split2(x)
    s = _mm(jnp.concatenate([hi, lo], axis=0), _bf(bd_f[0:w, 0:w]))
    return s[0:m] + s[m:2 * m]


def _exact_cols_mm(x, rhs_b):
    m = x.shape[0]
    hi = _bf(x)
    r1 = x - hi.astype(jnp.float32)
    mid = _bf(r1)
    lo = _bf(r1 - mid.astype(jnp.float32))
    s = _mm(jnp.concatenate([hi, mid, lo], axis=0), rhs_b)
    return s[0:m] + s[m:2 * m] + s[2 * m:3 * m]


def _mm_x3(x, w):
    m = x.shape[0]
    xh, xl = _split2(x)
    wh, wl = _split2(w)
    s = _mm(jnp.concatenate([xh, xl], axis=0), wh)
    return s[0:m] + s[m:2 * m] + _mm(xh, wl)


def _softplus(z):
    return jnp.maximum(z, 0.0) + jnp.log1p(jnp.exp(-jnp.abs(z)))


def _silu(z):
    return z * jax.nn.sigmoid(z)


def _own_layer(ref, first_call):
    if not first_call:
        return ref
    for l in range(1, DEPTH):
        ref[l] = jnp.zeros(ref.shape[1:], ref.dtype)
    return ref.at[0]


def _load_head_states(ref, bd_f, transposed):
    out = []
    for d in range(2):
        rows = jnp.concatenate([ref[d, h] for h in range(N_SLAB_HEADS)], axis=0)
        wide = jnp.concatenate([rows.T] * N_SLAB_HEADS, axis=0) * bd_f
        out.append(wide if transposed else wide.T)
    return jnp.concatenate(out, axis=0)


def _store_head_states(ref, st, transposed):
    for d in range(2):
        s = st[d * SLAB:(d + 1) * SLAB]
        if transposed:
            s = s.T
        for h in range(N_SLAB_HEADS):
            ref[d, h] = s[h * HEAD_DIM:(h + 1) * HEAD_DIM, h * HEAD_DIM:(h + 1) * HEAD_DIM]


def _gdn_prepare(qns, kns, vs, betas, gs, slab_ref, bd_b, tri_ref):
    eye = slab_ref[4]
    chains = [(j, d) for j in range(len(qns)) for d in (0, 1)]
    kn_b = [_bf(kn) for kn in kns]
    prods = [_mm_nt(jnp.concatenate([kn_b[j], _bf(qns[j])], axis=0), _blockdiag(kn_b[j], bd_b))
             for j in range(len(qns))]
    ones = jnp.ones((8, CH), jnp.bfloat16)
    cs = [_exact_rows_mm(tri_ref[d], gs[i]) for i, (j, d) in enumerate(chains)]
    r_rows = [_exact_rows_mm(ones, c * eye)[0:1] for c in cs]
    decs = [jnp.exp(jnp.where(slab_ref[2 * d] > 0.5, cs[i] - r_rows[i], 0.0)) * slab_ref[2 * d]
            for i, (j, d) in enumerate(chains)]
    t_invs = _unit_triangular_inverses(
        [-(slab_ref[2 * d + 1] * betas[i] * decs[i] * prods[j][0:CH]) for i, (j, d) in enumerate(chains)], eye, bd_b)
    g_lasts = [cs[i][0:1] if d == 1 else cs[i][CH - 1:CH] for i, (j, d) in enumerate(chains)]
    e_cs = [jnp.exp(c) for c in cs]
    rhs = [jnp.concatenate([_blockdiag(_bf(betas[i] * vs[j]), bd_b),
                            _blockdiag(_bf(betas[i] * e_cs[i] * kns[j]), bd_b)], axis=1)
           for i, (j, d) in enumerate(chains)]
    ws = [_mm(_bf(t), r) for t, r in zip(t_invs, rhs)]
    return ([w[:, 0:SLAB] for w in ws],
            [_bf(w[:, SLAB:2 * SLAB]) for w in ws],
            [_bf(prods[j][CH:2 * CH] * decs[i]) for i, (j, d) in enumerate(chains)],
            [_bf(qns[j] * e_cs[i]) for i, (j, d) in enumerate(chains)],
            [_bf(kns[j] * jnp.exp(g_lasts[i] - cs[i])) for i, (j, d) in enumerate(chains)],
            [jnp.broadcast_to(jnp.exp(g), (8, SLAB)) for g in g_lasts])


def _gdn_kernel(n, has_s0, n_aliased, *refs):
    refs = list(refs)
    x_ref, small_ref = refs[0:2]
    k = 2
    if has_s0:
        s0_ref = refs[k]
        k += 1
    conv_ref, dec_ref, gain_ref, e_ref, slab_ref, bd_ref, tri_ref = refs[k:k + 7]
    k += 7 + n_aliased
    o_ref, sfin_ref, qn_s, kn_s, v_s, g_s, b_s, wv_s, wk_s, qk_s, qd_s, ke_s, eg_s, o_s, st_s = refs[k:]
    nc = n // CH
    bd_f = bd_ref[...]
    bd_b = _bf(bd_f)

    x = x_ref[:, 0:3 * SLAB]
    row = lax.broadcasted_iota(jnp.int32, (n, 1), 0)
    prev = jnp.where(row == 0, 0.0, pltpu.roll(x, 1, axis=0))
    nxt = jnp.where(row == n - 1, 0.0, pltpu.roll(x, n - 1, axis=0))
    y = _silu(prev * conv_ref[0:1, :] + x * conv_ref[1:2, :] + nxt * conv_ref[2:3, :])
    q = y[:, 0:SLAB]
    kk = y[:, SLAB:2 * SLAB]
    qn_s[...] = q * lax.rsqrt(_group_sum(q * q, bd_f) + EPS) * (HEAD_DIM ** -0.5)
    kn_s[...] = kk * lax.rsqrt(_group_sum(kk * kk, bd_f) + EPS)
    v_s[...] = y[:, 2 * SLAB:3 * SLAB]
    sm = small_ref[...]
    lane = lax.broadcasted_iota(jnp.int32, sm.shape, 1)
    decay = -jnp.exp(dec_ref[0:1, :]) * _softplus(sm + dec_ref[1:2, :])
    scal = jnp.where(lane < SMALL_GDN + 2 * GDN_HEADS, decay, jax.nn.sigmoid(sm))
    scal = jnp.where((lane >= SMALL_GDN) & (lane < SMALL_GDN + 4 * GDN_HEADS), scal, 0.0)
    bc = _exact_cols_mm(scal, _bf(e_ref[...]))
    for d in range(2):
        g_s[d] = bc[:, d * SLAB:(d + 1) * SLAB]
        b_s[d] = bc[:, (2 + d) * SLAB:(3 + d) * SLAB]
    if has_s0:
        st_s[...] = _load_head_states(s0_ref, bd_f, transposed=False)
    else:
        st_s[...] = jnp.zeros_like(st_s)

    def prepare(i, carry):
        c0 = i * PREP_CHUNKS
        rows = [pl.ds(pl.multiple_of((c0 + j) * CH, CH), CH) for j in range(PREP_CHUNKS)]
        vals = _gdn_prepare([qn_s[r, :] for r in rows], [kn_s[r, :] for r in rows], [v_s[r, :] for r in rows],
                            [b_s[d, r, :] for r in rows for d in range(2)],
                            [g_s[d, r, :] for r in rows for d in range(2)], slab_ref, bd_b, tri_ref)
        for ref, chain_vals in zip((wv_s, wk_s, qk_s, qd_s, ke_s, eg_s), vals):
            for j in range(PREP_CHUNKS):
                ref[c0 + j] = jnp.concatenate(chain_vals[2 * j:2 * j + 2], axis=0)
        return carry

    lax.fori_loop(0, nc // PREP_CHUNKS, prepare, 0)

    def step(t, carry):
        s_all = st_s[...]
        dirs = (0, 1)
        cc = (t, nc - 1 - t)
        half = (slice(0, CH), slice(CH, 2 * CH))
        s = [s_all[d * SLAB:(d + 1) * SLAB] for d in dirs]
        s_b = [_bf(x) for x in s]
        u_b = [_bf(wv_s[cc[d], half[d], :] - _mm(wk_s[cc[d], half[d], :], s_b[d])) for d in dirs]
        upd = [_mm_tn(ke_s[cc[d], half[d], :], u_b[d]) for d in dirs]
        outs = [_mm(qd_s[cc[d], half[d], :], s_b[d]) + _mm(qk_s[cc[d], half[d], :], _blockdiag(u_b[d], bd_b))
                for d in dirs]
        new_s = [s[d] * eg_s[cc[d], d * 8:d * 8 + 1, :] + bd_f * upd[d] for d in dirs]
        st_s[...] = jnp.concatenate(new_s, axis=0)
        o_s[t] = jnp.concatenate(outs, axis=0)
        return carry

    lax.fori_loop(0, nc, step, 0)
    _store_head_states(_own_layer(sfin_ref, not has_s0 and n_aliased == 0), st_s[...], transposed=False)
    o = jnp.concatenate([o_s[c, 0:CH, :] + o_s[nc - 1 - c, CH:2 * CH, :] for c in range(nc)], axis=0)
    o = o * lax.rsqrt(_group_sum(o * o, bd_f) * (1.0 / HEAD_DIM) + EPS) * gain_ref[...]
    o_ref[...] = o * _silu(x_ref[:, 3 * SLAB:4 * SLAB])


def _gla_prepare(qs, ks, vs, las, slab_ref, bd_f, tri_ref):
    bd_b = _bf(bd_f)
    chains = [(j, d) for j in range(len(qs)) for d in (0, 1)]
    cums = [_exact_rows_mm(tri_ref[d], las[i]) for i, (j, d) in enumerate(chains)]
    lasts = [cums[i][0:1] if d == 1 else cums[i][CH - 1:CH] for i, (j, d) in enumerate(chains)]
    q_dec = [_bf(qs[j] * (HEAD_DIM ** -0.5) * jnp.exp(cums[i])) for i, (j, d) in enumerate(chains)]
    k_inv = [_bf(ks[j] * jnp.exp(-cums[i])) for i, (j, d) in enumerate(chains)]
    k_end = [_bf(ks[j] * jnp.exp(lasts[i] - cums[i])) for i, (j, d) in enumerate(chains)]
    v_b = [_bf(v) for v in vs]
    v_bd = [_blockdiag(v, bd_b) for v in v_b]
    att = [_mm_nt(q_dec[i], _blockdiag(k_inv[i], bd_b)) * slab_ref[2 * d] for i, (j, d) in enumerate(chains)]
    upd = [bd_f * _mm_tn(v_b[j], k_end[i]) for i, (j, d) in enumerate(chains)]
    intra = [_mm(_bf(att[i]), v_bd[j]) for i, (j, d) in enumerate(chains)]
    return intra, q_dec, upd, [jnp.broadcast_to(jnp.exp(l), (8, SLAB)) for l in lasts]


def _gla_kernel(n, has_s0, n_aliased, *refs):
    refs = list(refs)
    x_ref, small_ref = refs[0:2]
    k = 2
    if has_s0:
        s0_ref = refs[k]
        k += 1
    wup_ref, bup_ref, gain_ref, slab_ref, bd_ref, tri_ref = refs[k:k + 6]
    k += 6 + n_aliased
    o_ref, sfin_ref, la_s, oi_s, qd_s, up_s, el_s, o_s, st_s = refs[k:]
    nc = n // CH
    bd_f = bd_ref[...]
    small = small_ref[...]
    for d in range(2):
        z = _mm_x3(small, wup_ref[d]) + bup_ref[d:d + 1, :]
        la_s[d] = (jnp.minimum(z, 0.0) - jnp.log1p(jnp.exp(-jnp.abs(z)))) * (1.0 / GLA_TAU)
    if has_s0:
        st_s[...] = _load_head_states(s0_ref, bd_f, transposed=True)
    else:
        st_s[...] = jnp.zeros_like(st_s)

    def prepare(i, carry):
        c0 = i * PREP_CHUNKS
        rows = [pl.ds(pl.multiple_of((c0 + j) * CH, CH), CH) for j in range(PREP_CHUNKS)]
        vals = _gla_prepare([x_ref[r, 0:SLAB] for r in rows], [x_ref[r, SLAB:2 * SLAB] for r in rows],
                            [x_ref[r, 2 * SLAB:3 * SLAB] for r in rows],
                            [la_s[d, r, :] for r in rows for d in range(2)], slab_ref, bd_f, tri_ref)
        for ref, chain_vals in zip((oi_s, qd_s, up_s, el_s), vals):
            for j in range(PREP_CHUNKS):
                ref[c0 + j] = jnp.concatenate(chain_vals[2 * j:2 * j + 2], axis=0)
        return carry

    lax.fori_loop(0, nc // PREP_CHUNKS, prepare, 0)

    def step(t, carry):
        s_all = st_s[...]
        cc = (t, nc - 1 - t)
        s = [s_all[d * SLAB:(d + 1) * SLAB] for d in range(2)]
        outs = [oi_s[cc[d], d * CH:(d + 1) * CH, :] + _mm_nt(qd_s[cc[d], d * CH:(d + 1) * CH, :], _bf(s[d]))
                for d in range(2)]
        new_s = [s[d] * el_s[cc[d], d * 8:d * 8 + 1, :] + up_s[cc[d], d * SLAB:(d + 1) * SLAB, :] for d in range(2)]
        st_s[...] = jnp.concatenate(new_s, axis=0)
        o_s[t] = jnp.concatenate(outs, axis=0)
        return carry

    lax.fori_loop(0, nc, step, 0)
    _store_head_states(_own_layer(sfin_ref, not has_s0 and n_aliased == 0), st_s[...], transposed=True)
    o = jnp.concatenate([o_s[c, 0:CH, :] + o_s[nc - 1 - c, CH:2 * CH, :] for c in range(nc)], axis=0)
    o = o * lax.rsqrt(_group_sum(o * o, bd_f) * (1.0 / HEAD_DIM) + EPS) * gain_ref[...]
    o_ref[...] = o * _silu(x_ref[:, 3 * SLAB:4 * SLAB])


def _linear_mixer(kind, layer, hp, col_block, s0, params, consts, ctx_states=None):
    latent = s0 is not None
    n = DEC_SEQ if latent else SEQ
    n_seq = DEC_BATCH if latent else BATCH
    row0 = N_PROMPT_TOK // n if latent else 0
    full = lambda a: pl.BlockSpec(a.shape, lambda b: (0,) * a.ndim)
    state_block = (2, N_SLAB_HEADS, HEAD_DIM, HEAD_DIM)
    in_specs = [pl.BlockSpec((n, 4 * SLAB), lambda b: (row0 + b, col_block)),
                pl.BlockSpec((n, LANES), lambda b: (row0 + b, SMALL_COL_BLOCK))]
    args = [hp, hp]
    if latent:
        in_specs.append(pl.BlockSpec((None, None) + state_block, lambda b: (b, layer, 0, 0, 0, 0)))
        args.append(s0)
    for a in tuple(params) + tuple(consts):
        in_specs.append(full(a))
        args.append(a)
    aliases = {}
    if ctx_states is not None:
        aliases = {len(args): 1}
        in_specs.append(pl.BlockSpec(memory_space=pl.ANY))
        args.append(ctx_states)
    if latent:
        state_spec = pl.BlockSpec((None,) + state_block, lambda b: (b, 0, 0, 0, 0))
        state_shape = (n_seq,) + state_block
    elif ctx_states is None:
        assert layer == 0
        state_spec = pl.BlockSpec((None, DEPTH) + state_block, lambda b: (b, 0, 0, 0, 0, 0))
        state_shape = (n_seq, DEPTH) + state_block
    else:
        state_spec = pl.BlockSpec((None, None) + state_block, lambda b: (b, layer, 0, 0, 0, 0))
        state_shape = (n_seq, DEPTH) + state_block
    nc = n // CH
    seq_buf = pltpu.VMEM((n, SLAB), jnp.float32)
    dir_buf = pltpu.VMEM((2, n, SLAB), jnp.float32)
    pair_f32 = pltpu.VMEM((nc, 2 * CH, SLAB), jnp.float32)
    pair_b16 = pltpu.VMEM((nc, 2 * CH, SLAB), jnp.bfloat16)
    state_buf = pltpu.VMEM((2 * SLAB, SLAB), jnp.float32)
    if kind == 'gdn':
        body = functools.partial(_gdn_kernel, n, latent, len(aliases))
        scratch = [seq_buf, seq_buf, seq_buf, dir_buf, dir_buf, pair_f32, pair_b16, pair_b16, pair_b16, pair_b16,
                   pltpu.VMEM((nc, 16, SLAB), jnp.float32), pair_f32, state_buf]
    else:
        body = functools.partial(_gla_kernel, n, latent, len(aliases))
        scratch = [dir_buf, pair_f32, pair_b16, pltpu.VMEM((nc, 2 * SLAB, SLAB), jnp.float32),
                   pltpu.VMEM((nc, 16, SLAB), jnp.float32), pair_f32, state_buf]
    return pl.pallas_call(
        body,
        grid=(n_seq,),
        in_specs=in_specs,
        out_specs=[pl.BlockSpec((n, SLAB), lambda b: (b, 0)), state_spec],
        out_shape=[jax.ShapeDtypeStruct((n_seq * n, SLAB), jnp.float32),
                   jax.ShapeDtypeStruct(state_shape, jnp.float32)],
        scratch_shapes=scratch,
        input_output_aliases=aliases,
        compiler_params=_cparams("arbitrary"),
        name=kind + ("_latent" if latent else "_ctx"),
    )(*args)


ATT_SCALE = HEAD_DIM ** -0.5
NA_ROWS = DEC_SEQ // GRID_W
NA_KROWS = min(NA_KH, NA_ROWS)
NA_WIN = NA_KROWS * GRID_W
N_SWA_BLOCKS = DEC_SEQ // SWA_BLOCK
assert SWA_WINDOW == SWA_BLOCK
NA_GROUP = 4
assert NA_KROWS % 2 == 0 and NA_ROWS % NA_GROUP == 0


def _head_rmsnorm(x, bd_f, gain):
    return x * lax.rsqrt(_group_sum(x * x, bd_f) * (1.0 / HEAD_DIM) + EPS) * gain


def _lane_group(shape):
    return lax.broadcasted_iota(jnp.int32, shape, 1) // HEAD_DIM


def _stack_groups(x):
    grp = _lane_group(x.shape)
    return jnp.concatenate([jnp.where(grp == g, x, jnp.zeros_like(x)) for g in range(x.shape[1] // HEAD_DIM)], axis=0)


def _stack_swa_queries(q_b):
    return jnp.concatenate([_stack_groups(q_b[:, 0:LANES]), _stack_groups(q_b[:, LANES:2 * LANES])], axis=0)


def _unstack_swa(o, m):
    low = _lane_group((m, LANES)) == 0
    return jnp.concatenate([jnp.where(low, o[0:m], o[m:2 * m]), jnp.where(low, o[2 * m:3 * m], o[3 * m:4 * m])],
                           axis=1)


def _unstack_groups(o, m):
    grp = _lane_group((m, o.shape[1]))
    out = jnp.where(grp == 0, o[0:m], 0.0)
    for g in range(1, o.shape[1] // HEAD_DIM):
        out = out + jnp.where(grp == g, o[g * m:(g + 1) * m], 0.0)
    return out


def _softmax_pv_chains(chains):
    ms = []
    for logits, _, extra, _ in chains:
        m = jnp.max(logits[0], axis=-1, keepdims=True)
        for l in logits[1:]:
            m = jnp.maximum(m, jnp.max(l, axis=-1, keepdims=True))
        ms.append(m if extra is None else jnp.maximum(m, extra))
    es = [[jnp.exp(l - m) for l in c[0]] for c, m in zip(chains, ms)]
    dens = []
    for c, m, e in zip(chains, ms, es):
        den = jnp.exp(c[2] - m) if c[2] is not None else 0.0
        for piece in e:
            den = den + jnp.sum(piece, axis=-1, keepdims=True)
        dens.append(den)
    pvs = [0.0] * len(chains)
    for p in range(max(len(c[0]) for c in chains)):
        for ci, c in enumerate(chains):
            if p < len(c[0]):
                transposed = c[3] and p == len(c[0]) - 1
                e_b = _bf(es[ci][p])
                pvs[ci] = pvs[ci] + (_mm_nt(e_b, c[1][p]) if transposed else _mm(e_b, c[1][p]))
    return [pv / den for pv, den in zip(pvs, dens)]


def _sink_column(sink_ref, m):
    return jnp.concatenate([jnp.full((m, 1), sink_ref[g], jnp.float32) for g in range(SWA_HEADS)], axis=0)


def _rope(x, rope_ref):
    reps = x.shape[1] // LANES
    wide = lambda i: jnp.concatenate([rope_ref[i]] * reps, axis=1)
    shift = HEAD_DIM // 4
    return (x * wide(0) + pltpu.roll(x, x.shape[1] - shift, axis=1) * wide(1)
            + pltpu.roll(x, shift, axis=1) * wide(2))


def _store_cache(ref, slab):
    t = slab.T
    for h in range(ref.shape[0]):
        ref[h] = t[h * HEAD_DIM:(h + 1) * HEAD_DIM, :]


def _ctx_attn_kernel(n_aliased, bq_ref, bk_ref, bv_ref, cq_ref, ck_ref, cv_ref, gain_ref, sink_ref, bd_ref, *rest):
    o_ref = rest[n_aliased]
    swak_ref, swav_ref, nak_ref, nav_ref = (_own_layer(r, n_aliased == 0) for r in rest[n_aliased + 1:])
    n = SEQ
    bd_f = bd_ref[...]
    q = _head_rmsnorm(bq_ref[...], bd_f, gain_ref[0:1, :])
    k = _head_rmsnorm(bk_ref[...], bd_f[0:LANES, 0:LANES], gain_ref[1:2, 0:LANES])
    v = bv_ref[...]
    _store_cache(swak_ref, k)
    _store_cache(swav_ref, v)
    q2 = _head_rmsnorm(cq_ref[...], bd_f, gain_ref[2:3, :])
    k2 = _head_rmsnorm(ck_ref[...], bd_f, gain_ref[3:4, :])
    v2 = cv_ref[...]
    _store_cache(nak_ref, k2)
    _store_cache(nav_ref, v2)
    logits_b = _mm_nt(_stack_swa_queries(_bf(q)), _bf(k)) * ATT_SCALE
    logits_c = _mm_nt(_stack_groups(_bf(q2)), _bf(k2)) * ATT_SCALE
    out_b, out_c = _softmax_pv_chains([([logits_b], [_bf(v)], _sink_column(sink_ref, n), False),
                                       ([logits_c], [_bf(v2)], None, False)])
    o_ref[...] = jnp.concatenate([_unstack_swa(out_b, n), _unstack_groups(out_c, n)], axis=1)


def _load_cache_slabs(ck_ref, cv_ref, ck_s, cv_s):
    for src, dst in ((ck_ref, ck_s), (cv_ref, cv_s)):
        dst[...] = _bf(jnp.concatenate([src[h] for h in range(src.shape[0])], axis=0))


def _swa_latent_kernel(q_ref, k_ref, v_ref, ck_ref, cv_ref, rope_ref, gain_ref, sink_ref, bd_ref,
                       o_ref, q_s, k_s, v_s, ck_s, cv_s):
    _load_cache_slabs(ck_ref, cv_ref, ck_s, cv_s)
    bd_f = bd_ref[...]
    q_s[...] = _bf(_rope(_head_rmsnorm(q_ref[...], bd_f, gain_ref[0:1, :]), rope_ref))
    k_s[...] = _bf(_rope(_head_rmsnorm(k_ref[...], bd_f[0:LANES, 0:LANES], gain_ref[1:2, 0:LANES]), rope_ref))
    v_s[...] = _bf(v_ref[...])
    m = SWA_BLOCK
    iq = lax.broadcasted_iota(jnp.int32, (SWA_HEADS * m, m), 0) % m
    jk = lax.broadcasted_iota(jnp.int32, (SWA_HEADS * m, m), 1)
    ok_prev = jk >= iq
    ok_next = jk <= iq
    ok_same = jk >= 0
    sink = _sink_column(sink_ref, m)
    for i in range(N_SWA_BLOCKS):
        lo, hi = max(i - 1, 0), min(i + 1, N_SWA_BLOCKS - 1)
        qs = _stack_swa_queries(q_s[i * m:(i + 1) * m, :])
        allowed = jnp.concatenate([ok_prev if j < i else ok_next if j > i else ok_same for j in range(lo, hi + 1)],
                                  axis=1)
        l_loc = jnp.where(allowed, _mm_nt(qs, k_s[lo * m:(hi + 1) * m, :]) * ATT_SCALE, NEG_INF)
        l_ctx = _mm(qs, ck_s[...]) * ATT_SCALE
        o, = _softmax_pv_chains([([l_loc, l_ctx], [v_s[lo * m:(hi + 1) * m, :], cv_s[...]], sink, True)])
        o_ref[i * m:(i + 1) * m, :] = _unstack_swa(o, m)


def _na_latent_kernel(q_ref, k_ref, v_ref, ck_ref, cv_ref, bias_ref, gain_ref, bd_ref,
                      o_ref, q_s, k_s, v_s, ck_s, cv_s):
    _load_cache_slabs(ck_ref, cv_ref, ck_s, cv_s)
    bd_f = bd_ref[...]
    q_s[...] = _bf(_head_rmsnorm(q_ref[...], bd_f, gain_ref[2:3, :]))
    k_s[...] = _bf(_head_rmsnorm(k_ref[...], bd_f, gain_ref[3:4, :]))
    v_s[...] = _bf(v_ref[...])

    grp = range(NA_GROUP)

    def row_group(i, carry):
        rows = [NA_GROUP * i + a for a in grp]
        rs = [jnp.clip(r - NA_KROWS // 2, 0, NA_ROWS - NA_KROWS) for r in rows]
        qrows = [pl.ds(pl.multiple_of(r * GRID_W, GRID_W), GRID_W) for r in rows]
        wins = [pl.ds(pl.multiple_of(s * GRID_W, GRID_W), NA_WIN) for s in rs]
        qs = [_stack_groups(q_s[qr, :]) for qr in qrows]
        bias = [jnp.concatenate([bias_ref[s - r + NA_KH - 1 + 2 * p] for p in range(NA_KROWS // 2)], axis=1)
                for r, s in zip(rows, rs)]
        l_loc = [_mm_nt(qs[a], k_s[wins[a], :]) * ATT_SCALE + bias[a] for a in grp]
        l_ctx = [_mm(qs[a], ck_s[...]) * ATT_SCALE for a in grp]
        m = [jnp.maximum(jnp.max(l_loc[a], axis=-1, keepdims=True), jnp.max(l_ctx[a], axis=-1, keepdims=True))
             for a in grp]
        e_loc = [jnp.exp(l_loc[a] - m[a]) for a in grp]
        e_ctx = [jnp.exp(l_ctx[a] - m[a]) for a in grp]
        den = [jnp.sum(e_loc[a], axis=-1, keepdims=True) + jnp.sum(e_ctx[a], axis=-1, keepdims=True) for a in grp]
        pv_loc = [_mm(_bf(e_loc[a]), v_s[wins[a], :]) for a in grp]
        pv_ctx = [_mm_nt(_bf(e_ctx[a]), cv_s[...]) for a in grp]
        for a in grp:
            o_ref[qrows[a], :] = _unstack_groups((pv_loc[a] + pv_ctx[a]) / den[a], GRID_W)
        return carry

    lax.fori_loop(0, NA_ROWS // NA_GROUP, row_group, 0)


def _rope_tables():
    t = jnp.arange(DEC_SEQ)
    rows = (t // GRID_W).astype(jnp.float32)
    cols = (t % GRID_W).astype(jnp.float32)
    half = HEAD_DIM // 2
    nf = half // 2
    inv = 1.0 / (ROPE_BASE ** (jnp.arange(nf, dtype=jnp.float32) / nf))
    d = np.arange(LANES) % HEAD_DIM
    pos = jnp.where(jnp.asarray(d < half)[None, :], rows[:, None], cols[:, None])
    ang = pos * inv[jnp.asarray(d % nf)][None, :]
    first = jnp.asarray((d % half) < nf)[None, :]
    cos, sin = jnp.cos(ang), jnp.sin(ang)
    return jnp.stack([cos, jnp.where(first, -sin, 0.0), jnp.where(first, 0.0, sin)])


def _na_bias_table(rpb):
    col = np.arange(GRID_W)
    cs = np.clip(col - NA_KW // 2, 0, GRID_W - NA_KW)
    col_mask = (col[None, :] >= cs[:, None]) & (col[None, :] < cs[:, None] + NA_KW)
    dc = np.clip(col[None, :] - col[:, None], -(NA_KW - 1), NA_KW - 1) + NA_KW - 1
    pick = (np.arange(2 * NA_KW - 1)[:, None, None] == dc[None]).astype(np.float32)
    b = jnp.einsum('hrd,dqk->hrqk', rpb, jnp.asarray(pick), precision=_HI)
    b = jnp.where(jnp.asarray(col_mask)[None, None], b, NEG_INF)
    pair = jnp.concatenate([b[:, 0:2 * NA_KH - 2], b[:, 1:2 * NA_KH - 1]], axis=-1)
    return pair.transpose(1, 0, 2, 3).reshape(2 * NA_KH - 2, NA_HEADS * GRID_W, 2 * GRID_W)


def _attention_ctx(layer, hp, gains, sink, bd_c, caches):
    cb = lambda w, off: (lambda b: (b, off // w))
    q_w, kv_w = GROUP_WIDTH, SWA_KV_WIDTH
    full = lambda a: pl.BlockSpec(a.shape, lambda b: (0,) * a.ndim)
    if caches is None:
        assert layer == 0
        cache_out = lambda h: pl.BlockSpec((None, DEPTH, h, HEAD_DIM, SEQ), lambda b: (b, 0, 0, 0, 0))
    else:
        cache_out = lambda h: pl.BlockSpec((None, None, h, HEAD_DIM, SEQ), lambda b: (b, layer, 0, 0, 0))
    cache_shape = lambda h: jax.ShapeDtypeStruct((BATCH, DEPTH, h, HEAD_DIM, SEQ), jnp.float32)
    n_in = 9
    prev = [] if caches is None else list(caches)
    return pl.pallas_call(
        functools.partial(_ctx_attn_kernel, len(prev)),
        grid=(BATCH,),
        in_specs=[pl.BlockSpec((SEQ, q_w), cb(q_w, COL_B)),
                  pl.BlockSpec((SEQ, kv_w), cb(kv_w, COL_B + q_w)),
                  pl.BlockSpec((SEQ, kv_w), cb(kv_w, COL_B + q_w + kv_w)),
                  pl.BlockSpec((SEQ, q_w), cb(q_w, COL_C)),
                  pl.BlockSpec((SEQ, q_w), cb(q_w, COL_C + q_w)),
                  pl.BlockSpec((SEQ, q_w), cb(q_w, COL_C + 2 * q_w)),
                  full(gains), pl.BlockSpec(memory_space=pltpu.SMEM), full(bd_c)]
                 + [pl.BlockSpec(memory_space=pl.ANY)] * len(prev),
        out_specs=[pl.BlockSpec((SEQ, 2 * q_w), lambda b: (b, 0)),
                   cache_out(SWA_KV_HEADS), cache_out(SWA_KV_HEADS), cache_out(NA_HEADS), cache_out(NA_HEADS)],
        out_shape=[jax.ShapeDtypeStruct((N_PROMPT_TOK, 2 * q_w), jnp.float32),
                   cache_shape(SWA_KV_HEADS), cache_shape(SWA_KV_HEADS), cache_shape(NA_HEADS), cache_shape(NA_HEADS)],
        input_output_aliases={n_in + i: 1 + i for i in range(len(prev))},
        compiler_params=_cparams("arbitrary"),
        name="attn_ctx",
    )(hp, hp, hp, hp, hp, hp, gains, sink, bd_c, *prev)


def _attention_latent(kind, layer, hp, ck, cv, table, gains, sink, bd_c):
    n = DEC_SEQ
    row0 = N_PROMPT_TOK // n
    q_w = GROUP_WIDTH
    kv_w = SWA_KV_WIDTH if kind == 'swa' else q_w
    col = COL_B if kind == 'swa' else COL_C
    cb = lambda w, off: (lambda b: (row0 + b, off // w))
    full = lambda a: pl.BlockSpec(a.shape, lambda b: (0,) * a.ndim)
    cache_spec = pl.BlockSpec((None, None, kv_w // HEAD_DIM, HEAD_DIM, PAST_LEN), lambda b: (b, layer, 0, 0, 0))
    in_specs = [pl.BlockSpec((n, q_w), cb(q_w, col)),
                pl.BlockSpec((n, kv_w), cb(kv_w, col + q_w)),
                pl.BlockSpec((n, kv_w), cb(kv_w, col + q_w + kv_w)),
                cache_spec, cache_spec,
                full(table), full(gains)]
    args = [hp, hp, hp, ck, cv, table, gains]
    if kind == 'swa':
        in_specs.append(pl.BlockSpec(memory_space=pltpu.SMEM))
        args.append(sink)
    in_specs.append(full(bd_c))
    args.append(bd_c)
    return pl.pallas_call(
        _swa_latent_kernel if kind == 'swa' else _na_latent_kernel,
        grid=(DEC_BATCH,),
        in_specs=in_specs,
        out_specs=pl.BlockSpec((n, q_w), lambda b: (b, 0)),
        out_shape=jax.ShapeDtypeStruct((N_SAMPLE_TOK, q_w), jnp.float32),
        scratch_shapes=[pltpu.VMEM((n, q_w), jnp.bfloat16), pltpu.VMEM((n, kv_w), jnp.bfloat16),
                        pltpu.VMEM((n, kv_w), jnp.bfloat16), pltpu.VMEM((kv_w, PAST_LEN), jnp.bfloat16),
                        pltpu.VMEM((kv_w, PAST_LEN), jnp.bfloat16)],
        compiler_params=_cparams("arbitrary"),
        name=kind + "_latent",
    )(*args)


def kernel(x_prompt, x_sample, cache_swa_k, cache_swa_v, cache_na_k, cache_na_v, state_gla, state_gdn, c, c_ctx, w_mod, b_mod, norm1_g, norm2_g, w_in, w_out, gla_wup, gla_bup, gla_onorm, swa_qnorm, swa_knorm, swa_sink, na_qnorm, na_knorm, na_rpb, gdn_conv, gdn_alog, gdn_dtbias, gdn_onorm, ffn_w1, ffn_w3, ffn_w2, moe_router, moe_w1, moe_w3, moe_w2):
    bf16 = jnp.bfloat16
    x = jnp.concatenate([x_prompt.reshape(N_PROMPT_TOK, D_MODEL), x_sample.reshape(N_SAMPLE_TOK, D_MODEL)], axis=0)
    cvec = jnp.concatenate([c_ctx[None, :], c, jnp.zeros((N_MOD_ROWS - 1 - DEC_BATCH, D_MODEL), jnp.float32)], axis=0)
    mod_all = _modulation_all(cvec, w_mod, b_mod).reshape(DEPTH, N_MOD_ROWS, 1, MOD_W)
    w_in_b = jnp.concatenate([_take_segments(w_in.astype(bf16), _in_col_segments(), 2),
                              jnp.zeros((DEPTH, D_MODEL, IN_COLS_PAD - IN_COLS), bf16)], axis=2)
    w_out_b = _take_segments(w_out.astype(bf16), _OUT_ROW_SEGMENTS, 1)
    rope_tables = _rope_tables()
    consts = _linear_consts()
    tile_heads = lambda g: jnp.tile(g, N_SLAB_HEADS)[None, :]
    small_expand = jnp.asarray(_small_expand_matrix())
    ffn_w = [w.astype(bf16) for w in (ffn_w1, ffn_w3, ffn_w2)]
    moe_w = (moe_w1, moe_w3, moe_w2)
    ctx_kv = [jnp.swapaxes(c_, -1, -2) for c_ in (cache_swa_k, cache_swa_v, cache_na_k, cache_na_v)]

    new_kv = st_a = st_d = None
    for l in range(DEPTH):
        mod = mod_all[l]
        hp = _in_proj(l, x, mod, norm1_g[l][None, :], w_in_b)

        wup = jnp.zeros((2, LANES, SLAB), jnp.float32)
        wup = wup.at[0, 0:GLA_LOWRANK].set(gla_wup[l, 0]).at[1, GLA_LOWRANK:2 * GLA_LOWRANK].set(gla_wup[l, 1])
        gla_params = (wup, gla_bup[l], tile_heads(gla_onorm[l]))
        a_ctx, st_a = _linear_mixer('gla', l, hp, COL_A // (4 * SLAB), None, gla_params, consts, st_a)
        a_lat, _ = _linear_mixer('gla', l, hp, COL_A // (4 * SLAB), state_gla, gla_params, consts)
        decay_cols = slice(SMALL_GDN, SMALL_GDN + 2 * GDN_HEADS)
        decay_params = jnp.zeros((2, LANES), jnp.float32).at[:, decay_cols].set(
            jnp.stack([gdn_alog[l].reshape(-1), gdn_dtbias[l].reshape(-1)]))
        gdn_params = (gdn_conv[l], decay_params, tile_heads(gdn_onorm[l]), small_expand)
        d_ctx, st_d = _linear_mixer('gdn', l, hp, COL_D // (4 * SLAB), None, gdn_params, consts, st_d)
        d_lat, _ = _linear_mixer('gdn', l, hp, COL_D // (4 * SLAB), state_gdn, gdn_params, consts)

        gains = jnp.stack([jnp.tile(g, N_SLAB_HEADS) for g in (swa_qnorm[l], swa_knorm[l], na_qnorm[l], na_knorm[l])])
        sink = swa_sink[l][jnp.asarray(SWA_Q_HEAD_ORDER)]
        bc_ctx, *new_kv = _attention_ctx(l, hp, gains, sink, consts[1], new_kv)
        b_lat = _attention_latent('swa', l, hp, ctx_kv[0], ctx_kv[1], rope_tables, gains, sink, consts[1])
        c_lat = _attention_latent('na', l, hp, ctx_kv[2], ctx_kv[3], _na_bias_table(na_rpb[l]), gains, None,
                                  consts[1])
        x = _out_proj(l, (a_ctx, bc_ctx, d_ctx), (a_lat, b_lat, c_lat, d_lat), x, mod, w_out_b)
        if l % 2 == 0:
            x = _ffn(l // 2, x, mod, norm2_g[l][None, :], *ffn_w)
        else:
            x = _moe_routed(l // 2, x, mod, norm2_g[l][None, :], moe_router[l // 2], *moe_w)

    outs = [jnp.swapaxes(o, -1, -2) for o in new_kv] + [st_a, st_d]
    y_prompt = x[:N_PROMPT_TOK].reshape(BATCH, SEQ, D_MODEL)
    y_sample = x[N_PROMPT_TOK:].reshape(DEC_BATCH, DEC_SEQ, D_MODEL)
    return (y_prompt, y_sample, *outs)
```

```python
import functools
import math

import numpy as np
import jax
import jax.numpy as jnp
from jax import lax
from jax.experimental import pallas as pl
from jax.experimental.pallas import tpu as pltpu
from jax.experimental.pallas import tpu_sc as plsc

D_MODEL = 1024
BATCH = 32
SEQ = 256
DEPTH = 4
DEC_BATCH = 8
DEC_SEQ = 1024
PAST_LEN = 512

GRID_W = 64
HEAD_DIM = 64
GROUP_WIDTH = D_MODEL // 4
GROUP_HEADS = GROUP_WIDTH // HEAD_DIM
GLA_HEADS = GROUP_HEADS
GLA_LOWRANK = 16
GLA_TAU = 16.0
GLA_CHUNK = 64
SWA_HEADS = GROUP_HEADS
SWA_KV_HEADS = 2
SWA_WINDOW = 128
SWA_BLOCK = 128
NA_HEADS = GROUP_HEADS
NA_KH = 8
NA_KW = 16
GDN_HEADS = GROUP_HEADS
GDN_CONV = 3
GDN_CHUNK = 64
D_FF = 2816
N_EXPERTS = 8
TOP_K = 2
D_FF_EXPERT = 1024
ROPE_BASE = 10000.0
EPS = 1e-6
NEG_INF = -1e30
SWA_KV_WIDTH = SWA_KV_HEADS * HEAD_DIM
IN_SPLITS = ([GROUP_WIDTH] * 4 + [GLA_LOWRANK] * 2 + [GROUP_WIDTH, SWA_KV_WIDTH, SWA_KV_WIDTH]
             + [GROUP_WIDTH] * 3 + [GROUP_WIDTH] * 4 + [GDN_HEADS] * 4)
IN_COLS = sum(IN_SPLITS)

LANES = 128
VMEM_LIMIT_BYTES = 56 * 1024 * 1024

N_PROMPT_TOK = BATCH * SEQ
N_SAMPLE_TOK = DEC_BATCH * DEC_SEQ
N_TOK = N_PROMPT_TOK + N_SAMPLE_TOK
N_MOD_ROWS = 16
MOD_W = 6 * D_MODEL
TM = 1024
IN_COLS_PAD = -(-IN_COLS // LANES) * LANES

COL_A = 0
COL_D = 4 * GROUP_WIDTH
COL_C = 8 * GROUP_WIDTH
COL_B = 11 * GROUP_WIDTH
COL_SMALL = COL_B + GROUP_WIDTH + 2 * SWA_KV_WIDTH
SMALL_COL_BLOCK = COL_SMALL // LANES
SMALL_GDN = 2 * GLA_LOWRANK


SWA_Q_HEAD_ORDER = (0, 2, 1, 3)
def _in_col_segments():
    off = [0] + [int(v) for v in np.cumsum(IN_SPLITS)]
    seg = lambda a, b: [(off[a], off[b])]
    swa_q = [(off[6] + h * HEAD_DIM, off[6] + (h + 1) * HEAD_DIM) for h in SWA_Q_HEAD_ORDER]
    return seg(0, 4) + seg(12, 16) + seg(9, 12) + swa_q + seg(7, 9) + seg(4, 6) + seg(16, 20)


_OUT_ROW_SEGMENTS = ([(0, GROUP_WIDTH)]
                     + [(GROUP_WIDTH + h * HEAD_DIM, GROUP_WIDTH + (h + 1) * HEAD_DIM) for h in SWA_Q_HEAD_ORDER]
                     + [(2 * GROUP_WIDTH, D_MODEL)])


def _take_segments(w, segments, axis):
    idx = [slice(None)] * w.ndim
    parts = []
    for a, b in segments:
        idx[axis] = slice(a, b)
        parts.append(w[tuple(idx)])
    return jnp.concatenate(parts, axis=axis)


def _small_expand_matrix():
    e = np.zeros((LANES, 4 * GROUP_WIDTH), np.float32)
    for s in range(4):
        for h in range(GROUP_HEADS):
            e[SMALL_GDN + s * GROUP_HEADS + h, s * GROUP_WIDTH + h * HEAD_DIM:s * GROUP_WIDTH + (h + 1) * HEAD_DIM] = 1.0
    return e


def _mod_row(i):
    n_prompt_tiles = N_PROMPT_TOK // TM
    return jnp.where(i < n_prompt_tiles, 0, 1 + (i - n_prompt_tiles) // (DEC_SEQ // TM))


def _cparams(*sem):
    return pltpu.CompilerParams(dimension_semantics=sem, vmem_limit_bytes=VMEM_LIMIT_BYTES)


def _mod_kernel(c_ref, w_ref, b_ref, o_ref):
    c = c_ref[...]
    s = c * jax.nn.sigmoid(c)
    o_ref[...] = jnp.dot(s, w_ref[...], preferred_element_type=jnp.float32,
                         precision=lax.Precision.HIGHEST) + b_ref[...]


def _modulation_all(cvec, w_mod, b_mod):
    tn = 1536
    return pl.pallas_call(
        _mod_kernel,
        grid=(DEPTH, MOD_W // tn),
        in_specs=[pl.BlockSpec((N_MOD_ROWS, D_MODEL), lambda l, j: (0, 0)),
                  pl.BlockSpec((None, D_MODEL, tn), lambda l, j: (l, 0, j)),
                  pl.BlockSpec((None, 1, tn), lambda l, j: (l, 0, j))],
        out_specs=pl.BlockSpec((None, N_MOD_ROWS, tn), lambda l, j: (l, 0, j)),
        out_shape=jax.ShapeDtypeStruct((DEPTH, N_MOD_ROWS, MOD_W), jnp.float32),
        compiler_params=_cparams("arbitrary", "arbitrary"),
        name="modulation",
    )(cvec, w_mod, b_mod.reshape(DEPTH, 1, MOD_W))


def _modulated_norm(x, g, shift, scale):
    y = x * lax.rsqrt(jnp.mean(x * x, axis=-1, keepdims=True) + EPS) * g
    return y * (1.0 + scale) + shift


def _in_proj_kernel(x_ref, mod_ref, g_ref, w_ref, o_ref):
    h = _modulated_norm(x_ref[...], g_ref[...], mod_ref[:, 0:D_MODEL], mod_ref[:, D_MODEL:2 * D_MODEL])
    o_ref[...] = jnp.dot(h.astype(jnp.bfloat16), w_ref[...], preferred_element_type=jnp.float32)


def _in_proj(layer, x, mod, g, w):
    return pl.pallas_call(
        _in_proj_kernel,
        grid=(N_TOK // TM,),
        in_specs=[pl.BlockSpec((TM, D_MODEL), lambda i: (i, 0)),
                  pl.BlockSpec((None, 1, MOD_W), lambda i: (_mod_row(i), 0, 0)),
                  pl.BlockSpec((1, D_MODEL), lambda i: (0, 0)),
                  pl.BlockSpec((None, D_MODEL, IN_COLS_PAD), lambda i: (layer, 0, 0),
                               pipeline_mode=pl.Buffered(1))],
        out_specs=pl.BlockSpec((TM, IN_COLS_PAD), lambda i: (i, 0)),
        out_shape=jax.ShapeDtypeStruct((N_TOK, IN_COLS_PAD), jnp.float32),
        compiler_params=_cparams("arbitrary"),
        name="in_proj",
    )(x, mod, g, w)


N_CTX_TILES = N_PROMPT_TOK // TM


def _out_proj_kernel(a_c, b_c, d_c, a_l, b_l, c_l, d_l, x_ref, mod_ref, w_ref, o_ref):
    is_ctx = pl.program_id(0) < N_CTX_TILES
    ctx = jnp.concatenate([a_c[...], b_c[...], d_c[...]], axis=1)
    lat = jnp.concatenate([a_l[...], b_l[...], c_l[...], d_l[...]], axis=1)
    mix = jnp.where(is_ctx, ctx, lat)
    y = jnp.dot(mix.astype(jnp.bfloat16), w_ref[...], preferred_element_type=jnp.float32)
    o_ref[...] = x_ref[...] + mod_ref[:, 2 * D_MODEL:3 * D_MODEL] * y


def _out_proj(layer, ctx_outs, lat_outs, x, mod, w):
    ctx_map = lambda i: (jnp.minimum(i, N_CTX_TILES - 1), 0)
    lat_map = lambda i: (jnp.maximum(i - N_CTX_TILES, 0), 0)
    return pl.pallas_call(
        _out_proj_kernel,
        grid=(N_TOK // TM,),
        in_specs=[pl.BlockSpec((TM, a.shape[1]), ctx_map) for a in ctx_outs]
                 + [pl.BlockSpec((TM, a.shape[1]), lat_map) for a in lat_outs]
                 + [pl.BlockSpec((TM, D_MODEL), lambda i: (i, 0)),
                    pl.BlockSpec((None, 1, MOD_W), lambda i: (_mod_row(i), 0, 0)),
                    pl.BlockSpec((None, D_MODEL, D_MODEL), lambda i: (layer, 0, 0), pipeline_mode=pl.Buffered(1))],
        out_specs=pl.BlockSpec((TM, D_MODEL), lambda i: (i, 0)),
        out_shape=jax.ShapeDtypeStruct((N_TOK, D_MODEL), jnp.float32),
        compiler_params=_cparams("arbitrary"),
        name="out_proj",
    )(*ctx_outs, *lat_outs, x, mod, w)


FF_CHUNK = D_FF // 2


def _ffn_kernel(x_ref, mod_ref, g_ref, w1_ref, w3_ref, w2_ref, o_ref):
    x = x_ref[...]
    h = _modulated_norm(x, g_ref[...], mod_ref[:, 3 * D_MODEL:4 * D_MODEL], mod_ref[:, 4 * D_MODEL:5 * D_MODEL])
    hb = h.astype(jnp.bfloat16)
    y = jnp.zeros((TM, D_MODEL), jnp.float32)
    for c0 in range(0, D_FF, FF_CHUNK):
        a = jnp.dot(hb, w1_ref[:, c0:c0 + FF_CHUNK], preferred_element_type=jnp.float32)
        b = jnp.dot(hb, w3_ref[:, c0:c0 + FF_CHUNK], preferred_element_type=jnp.float32)
        s = (a * jax.nn.sigmoid(a) * b).astype(jnp.bfloat16)
        y = y + jnp.dot(s, w2_ref[c0:c0 + FF_CHUNK, :], preferred_element_type=jnp.float32)
    o_ref[...] = x + mod_ref[:, 5 * D_MODEL:6 * D_MODEL] * y


def _ffn(idx, x, mod, g, w1, w3, w2):
    resident = dict(pipeline_mode=pl.Buffered(1))
    return pl.pallas_call(
        _ffn_kernel,
        grid=(N_TOK // TM,),
        in_specs=[pl.BlockSpec((TM, D_MODEL), lambda i: (i, 0)),
                  pl.BlockSpec((None, 1, MOD_W), lambda i: (_mod_row(i), 0, 0)),
                  pl.BlockSpec((1, D_MODEL), lambda i: (0, 0)),
                  pl.BlockSpec((None, D_MODEL, D_FF), lambda i: (idx, 0, 0), **resident),
                  pl.BlockSpec((None, D_MODEL, D_FF), lambda i: (idx, 0, 0), **resident),
                  pl.BlockSpec((None, D_FF, D_MODEL), lambda i: (idx, 0, 0), **resident)],
        out_specs=pl.BlockSpec((TM, D_MODEL), lambda i: (i, 0)),
        out_shape=jax.ShapeDtypeStruct((N_TOK, D_MODEL), jnp.float32),
        compiler_params=_cparams("arbitrary"),
        name="ffn_dense",
    )(x, mod, g, w1, w3, w2)


TMOE = 512
MOE_ROWS = TOP_K * N_TOK + N_EXPERTS * TMOE
MOE_TILES = MOE_ROWS // TMOE
PACK_W = D_MODEL // 2
SC_WINDOW = 64


def _pack_bf16_pairs(a):
    w = a.shape[1] // 2
    bits = lax.bitcast_convert_type(a.astype(jnp.bfloat16).astype(jnp.float32), jnp.uint32)
    return (bits[:, w:] & jnp.uint32(0xFFFF0000)) | (bits[:, 0:w] >> 16)


def _unpack_bf16_pairs(p):
    lo = lax.bitcast_convert_type(p << 16, jnp.float32)
    hi = lax.bitcast_convert_type(p & jnp.uint32(0xFFFF0000), jnp.float32)
    return jnp.concatenate([lo, hi], axis=1)


def _route_kernel(x_ref, mod_ref, g_ref, r_ref, h_ref, gate_ref, sel_ref):
    h = _modulated_norm(x_ref[...], g_ref[...], mod_ref[:, 3 * D_MODEL:4 * D_MODEL],
                        mod_ref[:, 4 * D_MODEL:5 * D_MODEL])
    h_ref[...] = _pack_bf16_pairs(h)
    logits = _mm_x3(h, r_ref[...])
    lane = lax.broadcasted_iota(jnp.int32, logits.shape, 1)
    v1 = jnp.max(logits, axis=-1, keepdims=True)
    i1 = jnp.min(jnp.where(logits == v1, lane, N_EXPERTS), axis=-1, keepdims=True)
    rest = jnp.where(lane == i1, -jnp.inf, logits)
    v2 = jnp.max(rest, axis=-1, keepdims=True)
    i2 = jnp.min(jnp.where(rest == v2, lane, N_EXPERTS), axis=-1, keepdims=True)
    p2 = jnp.exp(v2 - v1)
    den = 1.0 + p2
    gate_ref[...] = jnp.where(lane == i1, 1.0 / den, 0.0) + jnp.where(lane == i2, p2 / den, 0.0)
    sel_ref[...] = jnp.where((lane == i1) | (lane == i2), 1, 0)


def _route(x, mod, g, router):
    return pl.pallas_call(
        _route_kernel,
        grid=(N_TOK // TM,),
        in_specs=[pl.BlockSpec((TM, D_MODEL), lambda i: (i, 0)),
                  pl.BlockSpec((None, 1, MOD_W), lambda i: (_mod_row(i), 0, 0)),
                  pl.BlockSpec((1, D_MODEL), lambda i: (0, 0)),
                  pl.BlockSpec((D_MODEL, N_EXPERTS), lambda i: (0, 0))],
        out_specs=[pl.BlockSpec((TM, PACK_W), lambda i: (i, 0)),
                   pl.BlockSpec((TM, N_EXPERTS), lambda i: (i, 0)),
                   pl.BlockSpec((TM, N_EXPERTS), lambda i: (i, 0))],
        out_shape=[jax.ShapeDtypeStruct((N_TOK, PACK_W), jnp.uint32),
                   jax.ShapeDtypeStruct((N_TOK, N_EXPERTS), jnp.float32),
                   jax.ShapeDtypeStruct((N_TOK, N_EXPERTS), jnp.int32)],
        compiler_params=_cparams("arbitrary"),
        name="moe_route",
    )(x, mod, g, router)


def _dispatch_plan(gates, sel):
    rank = jnp.cumsum(sel, axis=0) - sel
    count = jnp.sum(sel, axis=0)
    padded = -(-count // TMOE) * TMOE
    seg_end = jnp.cumsum(padded)
    pos = (seg_end - padded)[None, :] + rank
    chosen = sel > 0
    pos_lo = jnp.min(jnp.where(chosen, pos, MOE_ROWS), axis=1)
    pos_hi = jnp.max(jnp.where(chosen, pos, -1), axis=1)
    w_lo = jnp.sum(jnp.where(chosen & (pos == pos_lo[:, None]), gates, 0.0), axis=1)
    w_hi = jnp.sum(jnp.where(chosen & (pos == pos_hi[:, None]), gates, 0.0), axis=1)
    fill = jnp.arange(TMOE, dtype=jnp.int32)[None, :]
    pad_dest = jnp.where(fill < (padded - count)[:, None], (seg_end - padded + count)[:, None] + fill, MOE_ROWS)
    dest = jnp.concatenate([pos_lo, pos_hi, pad_dest.reshape(-1)]).astype(jnp.int32)
    tile_row = jnp.arange(MOE_TILES, dtype=jnp.int32) * TMOE
    tile_expert = jnp.minimum(jnp.sum(seg_end[None, :] <= tile_row[:, None], axis=1), N_EXPERTS - 1)
    return (dest, jnp.stack([pos_lo, pos_hi]).astype(jnp.int32), jnp.stack([w_lo, w_hi], axis=1),
            tile_expert.astype(jnp.int32), (seg_end[-1:] // TMOE).astype(jnp.int32))


def _scatter_rows(table, dest):
    k, m = dest.shape[0], table.shape[0]
    mesh = plsc.VectorSubcoreMesh(core_axis_name="core", subcore_axis_name="subcore")

    @functools.partial(pl.kernel, out_type=jax.ShapeDtypeStruct((MOE_ROWS + SC_WINDOW, PACK_W), table.dtype),
                       mesh=mesh, scratch_types=[])
    def scatter(x_hbm, i_hbm, o_hbm):
        def body(x_vmem, i_vmem):
            pltpu.sync_copy(x_vmem, o_hbm.at[i_vmem.at[0, pl.ds(0, SC_WINDOW)]])

        pltpu.emit_pipeline(
            body,
            grid=(k // SC_WINDOW,),
            in_specs=[pl.BlockSpec((SC_WINDOW, PACK_W), index_map=lambda i: (i % (m // SC_WINDOW), 0)),
                      pl.BlockSpec((1, 2 * SC_WINDOW), index_map=lambda i: (i, 0))],
            out_specs=[],
            core_axis_name=('core', 'subcore'),
            dimension_semantics=(pltpu.PARALLEL,),
        )(x_hbm, i_hbm)

    idx_rows = dest.reshape(k // SC_WINDOW, SC_WINDOW)
    return scatter(table, jnp.concatenate([idx_rows, idx_rows], axis=1))


def _gather_rows(table, idx):
    k = idx.shape[0]
    mesh = plsc.VectorSubcoreMesh(core_axis_name="core", subcore_axis_name="subcore")

    @functools.partial(pl.kernel, out_type=jax.ShapeDtypeStruct((k, PACK_W), table.dtype), mesh=mesh)
    def gather(x_hbm, i_hbm, o_hbm):
        def body(i_vmem, o_vmem):
            pltpu.sync_copy(x_hbm.at[i_vmem.at[0, pl.ds(0, SC_WINDOW)]], o_vmem)

        pltpu.emit_pipeline(
            body,
            grid=(k // SC_WINDOW,),
            in_specs=[pl.BlockSpec((1, 2 * SC_WINDOW), index_map=lambda i: (i, 0))],
            out_specs=[pl.BlockSpec((SC_WINDOW, PACK_W), index_map=lambda i: (i, 0))],
            core_axis_name=('core', 'subcore'),
            dimension_semantics=(pltpu.PARALLEL,),
        )(i_hbm, o_hbm)

    idx_rows = idx.reshape(k // SC_WINDOW, SC_WINDOW)
    return gather(table, jnp.concatenate([idx_rows, idx_rows], axis=1))


def _expert_kernel(te_ref, nu_ref, h_ref, w1_ref, w3_ref, w2_ref, y_ref, w1_s, w3_s, w2_s):
    j = pl.program_id(0)

    @pl.when((j == 0) | (te_ref[j] != te_ref[jnp.maximum(j - 1, 0)]))
    def _():
        w1_s[...] = w1_ref[...].astype(jnp.bfloat16)
        w3_s[...] = w3_ref[...].astype(jnp.bfloat16)
        w2_s[...] = w2_ref[...].astype(jnp.bfloat16)

    @pl.when(j < nu_ref[0])
    def _():
        hb = _unpack_bf16_pairs(h_ref[...]).astype(jnp.bfloat16)
        a = jnp.dot(hb, w1_s[...], preferred_element_type=jnp.float32)
        b = jnp.dot(hb, w3_s[...], preferred_element_type=jnp.float32)
        s = (a * jax.nn.sigmoid(a) * b).astype(jnp.bfloat16)
        y_ref[...] = _pack_bf16_pairs(jnp.dot(s, w2_s[...], preferred_element_type=jnp.float32))

    @pl.when(pl.program_id(0) >= nu_ref[0])
    def _():
        y_ref[...] = jnp.zeros_like(y_ref)


def _experts(idx, tile_expert, n_used, hs, w1, w3, w2):
    wspec = lambda shape: pl.BlockSpec((None, None) + shape, lambda j, te, nu: (idx, te[j], 0, 0))
    return pl.pallas_call(
        _expert_kernel,
        grid_spec=pltpu.PrefetchScalarGridSpec(
            num_scalar_prefetch=2,
            grid=(MOE_TILES,),
            in_specs=[pl.BlockSpec((TMOE, PACK_W), lambda j, te, nu: (j, 0)),
                      wspec((D_MODEL, D_FF_EXPERT)), wspec((D_MODEL, D_FF_EXPERT)), wspec((D_FF_EXPERT, D_MODEL))],
            out_specs=pl.BlockSpec((TMOE, PACK_W), lambda j, te, nu: (j, 0)),
            scratch_shapes=[pltpu.VMEM((D_MODEL, D_FF_EXPERT), jnp.bfloat16),
                            pltpu.VMEM((D_MODEL, D_FF_EXPERT), jnp.bfloat16),
                            pltpu.VMEM((D_FF_EXPERT, D_MODEL), jnp.bfloat16)]),
        out_shape=jax.ShapeDtypeStruct((MOE_ROWS, PACK_W), jnp.uint32),
        compiler_params=_cparams("arbitrary"),
        name="moe_experts",
    )(tile_expert, n_used, hs, w1, w3, w2)


def _combine_kernel(x_ref, mod_ref, ylo_ref, yhi_ref, w_ref, *o_refs):
    y = w_ref[:, 0:1] * _unpack_bf16_pairs(ylo_ref[...]) + w_ref[:, 1:2] * _unpack_bf16_pairs(yhi_ref[...])
    out = x_ref[...] + mod_ref[:, 5 * D_MODEL:6 * D_MODEL] * y
    if len(o_refs) == 1:
        o_refs[0][...] = out
    else:
        is_ctx = pl.program_id(0) < N_CTX_TILES

        @pl.when(is_ctx)
        def _():
            o_refs[0][...] = out

        @pl.when(jnp.logical_not(is_ctx))
        def _():
            o_refs[1][...] = out


def _combine(x, mod, y2, w, split_streams=False):
    n_tiles = N_TOK // TM
    if split_streams:
        return pl.pallas_call(
            _combine_kernel,
            grid=(n_tiles,),
            in_specs=[pl.BlockSpec((TM, D_MODEL), lambda i: (i, 0)),
                      pl.BlockSpec((None, 1, MOD_W), lambda i: (_mod_row(i), 0, 0)),
                      pl.BlockSpec((TM, PACK_W), lambda i: (i, 0)),
                      pl.BlockSpec((TM, PACK_W), lambda i: (n_tiles + i, 0)),
                      pl.BlockSpec((TM, TOP_K), lambda i: (i, 0))],
            out_specs=[pl.BlockSpec((TM, D_MODEL), lambda i: (jnp.minimum(i, N_CTX_TILES - 1), 0)),
                       pl.BlockSpec((TM, D_MODEL), lambda i: (jnp.maximum(i - N_CTX_TILES, 0), 0))],
            out_shape=[jax.ShapeDtypeStruct((N_PROMPT_TOK, D_MODEL), jnp.float32),
                       jax.ShapeDtypeStruct((N_SAMPLE_TOK, D_MODEL), jnp.float32)],
            compiler_params=_cparams("arbitrary"),
            name="moe_combine_split",
        )(x, mod, y2, y2, w)
    return pl.pallas_call(
        _combine_kernel,
        grid=(n_tiles,),
        in_specs=[pl.BlockSpec((TM, D_MODEL), lambda i: (i, 0)),
                  pl.BlockSpec((None, 1, MOD_W), lambda i: (_mod_row(i), 0, 0)),
                  pl.BlockSpec((TM, PACK_W), lambda i: (i, 0)),
                  pl.BlockSpec((TM, PACK_W), lambda i: (n_tiles + i, 0)),
                  pl.BlockSpec((TM, TOP_K), lambda i: (i, 0))],
        out_specs=pl.BlockSpec((TM, D_MODEL), lambda i: (i, 0)),
        out_shape=jax.ShapeDtypeStruct((N_TOK, D_MODEL), jnp.float32),
        compiler_params=_cparams("arbitrary"),
        name="moe_combine",
    )(x, mod, y2, y2, w)


def _moe_routed(idx, x, mod, g, router, w1, w3, w2, split_streams=False):
    hp, gates, sel = _route(x, mod, g, router)
    dest, pos, w, tile_expert, n_used = _dispatch_plan(gates, sel)
    ys = _experts(idx, tile_expert, n_used, _scatter_rows(hp, dest), w1, w3, w2)
    return _combine(x, mod, _gather_rows(ys, pos.reshape(-1)), w, split_streams)


CH = GLA_CHUNK
SLAB = GROUP_WIDTH
N_SLAB_HEADS = SLAB // HEAD_DIM
PREP_CHUNKS = 4
_HI = lax.Precision.HIGHEST


def _linear_consts():
    i = np.arange(CH)[:, None]
    j = (np.arange(SLAB) % CH)[None, :]
    slab = np.stack([i >= j, i > j, i <= j, i < j, i == j]).astype(np.float32)
    r = np.arange(SLAB) // CH
    bd = (r[:, None] == r[None, :]).astype(np.float32)
    t = np.arange(CH)
    tri = np.stack([t[:, None] >= t[None, :], t[:, None] <= t[None, :]]).astype(np.float32)
    return jnp.asarray(slab), jnp.asarray(bd), jnp.asarray(tri, dtype=jnp.bfloat16)


def _bf(x):
    return x.astype(jnp.bfloat16)


def _split2(x):
    hi = _bf(x)
    return hi, _bf(x - hi.astype(jnp.float32))


def _split3_lanes(x):
    hi = _bf(x)
    r1 = x - hi.astype(jnp.float32)
    mid = _bf(r1)
    lo = _bf(r1 - mid.astype(jnp.float32))
    return jnp.concatenate([hi, mid, lo], axis=1)


def _mm(a, b):
    return jnp.dot(a, b, preferred_element_type=jnp.float32)


def _mm_nt(a, b):
    return lax.dot_general(a, b, (((1,), (1,)), ((), ())), preferred_element_type=jnp.float32)


def _mm_tn(a, b):
    return lax.dot_general(a, b, (((0,), (0,)), ((), ())), preferred_element_type=jnp.float32)


def _blockdiag(y_b, bd_b):
    return jnp.concatenate([y_b] * N_SLAB_HEADS, axis=0) * bd_b


def _exact_rows_mm(lhs_b, x):
    c = _mm(lhs_b, _split3_lanes(x))
    return c[:, 0:SLAB] + c[:, SLAB:2 * SLAB] + c[:, 2 * SLAB:3 * SLAB]


def _head_mm3(lhs, y, bd_b):
    m = lhs.shape[0]
    lh, ll = _split2(lhs)
    yh, yl = _split2(y)
    a = _mm(jnp.concatenate([lh, ll], axis=0), _blockdiag(yh, bd_b))
    return a[0:m] + a[m:2 * m] + _mm(lh, _blockdiag(yl, bd_b))


def _unit_triangular_inverses(nmats, eye, bd_b):
    ts = [eye + n for n in nmats]
    ps = [_head_mm3(n, n, bd_b) for n in nmats]
    n_doublings = int(math.log2(CH)) - 1
    for it in range(n_doublings):
        if it < n_doublings - 1:
            res = [_head_mm3(jnp.concatenate([t, p], axis=0), p, bd_b) for t, p in zip(ts, ps)]
            ts = [t + r[0:CH] for t, r in zip(ts, res)]
            ps = [r[CH:2 * CH] for r in res]
        else:
            ts = [t + _head_mm3(t, p, bd_b) for t, p in zip(ts, ps)]
    return ts


def _group_sum(x, bd_f):
    m, w = x.shape
    hi, lo = _split2(x)
    s = _mm(jnp.concatenate([hi, lo], axis=0), _bf(bd_f[0:w, 0:w]))
    return s[0:m] + s[m:2 * m]


def _exact_cols_mm(x, rhs_b):
    m = x.shape[0]
    hi = _bf(x)
    r1 = x - hi.astype(jnp.float32)
    mid = _bf(r1)
    lo = _bf(r1 - mid.astype(jnp.float32))
    s = _mm(jnp.concatenate([hi, mid, lo], axis=0), rhs_b)
    return s[0:m] + s[m:2 * m] + s[2 * m:3 * m]


def _mm_x3(x, w):
    m = x.shape[0]
    xh, xl = _split2(x)
    wh, wl = _split2(w)
    s = _mm(jnp.concatenate([xh, xl], axis=0), wh)
    return s[0:m] + s[m:2 * m] + _mm(xh, wl)


def _softplus(z):
    return jnp.maximum(z, 0.0) + jnp.log1p(jnp.exp(-jnp.abs(z)))


def _silu(z):
    return z * jax.nn.sigmoid(z)


def _own_layer(ref, first_call):
    if not first_call:
        return ref
    for l in range(1, DEPTH):
        ref[l] = jnp.zeros(ref.shape[1:], ref.dtype)
    return ref.at[0]


def _load_head_states(ref, bd_f, transposed):
    out = []
    for d in range(2):
        rows = jnp.concatenate([ref[d, h] for h in range(N_SLAB_HEADS)], axis=0)
        wide = jnp.concatenate([rows.T] * N_SLAB_HEADS, axis=0) * bd_f
        out.append(wide if transposed else wide.T)
    return jnp.concatenate(out, axis=0)


def _store_head_states(ref, st, transposed):
    for d in range(2):
        s = st[d * SLAB:(d + 1) * SLAB]
        if transposed:
            s = s.T
        for h in range(N_SLAB_HEADS):
            ref[d, h] = s[h * HEAD_DIM:(h + 1) * HEAD_DIM, h * HEAD_DIM:(h + 1) * HEAD_DIM]


def _gdn_prepare(qns, kns, vs, betas, gs, slab_ref, bd_b, tri_ref):
    eye = slab_ref[4]
    chains = [(j, d) for j in range(len(qns)) for d in (0, 1)]
    kn_b = [_bf(kn) for kn in kns]
    prods = [_mm_nt(jnp.concatenate([kn_b[j], _bf(qns[j])], axis=0), _blockdiag(kn_b[j], bd_b))
             for j in range(len(qns))]
    ones = jnp.ones((8, CH), jnp.bfloat16)
    cs = [_exact_rows_mm(tri_ref[d], gs[i]) for i, (j, d) in enumerate(chains)]
    r_rows = [_exact_rows_mm(ones, c * eye)[0:1] for c in cs]
    decs = [jnp.exp(jnp.where(slab_ref[2 * d] > 0.5, cs[i] - r_rows[i], 0.0)) * slab_ref[2 * d]
            for i, (j, d) in enumerate(chains)]
    t_invs = _unit_triangular_inverses(
        [-(slab_ref[2 * d + 1] * betas[i] * decs[i] * prods[j][0:CH]) for i, (j, d) in enumerate(chains)], eye, bd_b)
    g_lasts = [cs[i][0:1] if d == 1 else cs[i][CH - 1:CH] for i, (j, d) in enumerate(chains)]
    e_cs = [jnp.exp(c) for c in cs]
    rhs = [jnp.concatenate([_blockdiag(_bf(betas[i] * vs[j]), bd_b),
                            _blockdiag(_bf(betas[i] * e_cs[i] * kns[j]), bd_b)], axis=1)
           for i, (j, d) in enumerate(chains)]
    ws = [_mm(_bf(t), r) for t, r in zip(t_invs, rhs)]
    return ([w[:, 0:SLAB] for w in ws],
            [_bf(w[:, SLAB:2 * SLAB]) for w in ws],
            [_bf(prods[j][CH:2 * CH] * decs[i]) for i, (j, d) in enumerate(chains)],
            [_bf(qns[j] * e_cs[i]) for i, (j, d) in enumerate(chains)],
            [_bf(kns[j] * jnp.exp(g_lasts[i] - cs[i])) for i, (j, d) in enumerate(chains)],
            [jnp.broadcast_to(jnp.exp(g), (8, SLAB)) for g in g_lasts])


def _gdn_kernel(n, has_s0, n_aliased, *refs):
    refs = list(refs)
    x_ref, small_ref = refs[0:2]
    k = 2
    if has_s0:
        s0_ref = refs[k]
        k += 1
    conv_ref, dec_ref, gain_ref, e_ref, slab_ref, bd_ref, tri_ref = refs[k:k + 7]
    k += 7 + n_aliased
    o_ref, sfin_ref, qn_s, kn_s, v_s, g_s, b_s, wv_s, wk_s, qk_s, qd_s, ke_s, eg_s, o_s, st_s = refs[k:]
    nc = n // CH
    bd_f = bd_ref[...]
    bd_b = _bf(bd_f)

    x = x_ref[:, 0:3 * SLAB]
    row = lax.broadcasted_iota(jnp.int32, (n, 1), 0)
    prev = jnp.where(row == 0, 0.0, pltpu.roll(x, 1, axis=0))
    nxt = jnp.where(row == n - 1, 0.0, pltpu.roll(x, n - 1, axis=0))
    y = _silu(prev * conv_ref[0:1, :] + x * conv_ref[1:2, :] + nxt * conv_ref[2:3, :])
    q = y[:, 0:SLAB]
    kk = y[:, SLAB:2 * SLAB]
    qn_s[...] = q * lax.rsqrt(_group_sum(q * q, bd_f) + EPS) * (HEAD_DIM ** -0.5)
    kn_s[...] = kk * lax.rsqrt(_group_sum(kk * kk, bd_f) + EPS)
    v_s[...] = y[:, 2 * SLAB:3 * SLAB]
    sm = small_ref[...]
    lane = lax.broadcasted_iota(jnp.int32, sm.shape, 1)
    decay = -jnp.exp(dec_ref[0:1, :]) * _softplus(sm + dec_ref[1:2, :])
    scal = jnp.where(lane < SMALL_GDN + 2 * GDN_HEADS, decay, jax.nn.sigmoid(sm))
    scal = jnp.where((lane >= SMALL_GDN) & (lane < SMALL_GDN + 4 * GDN_HEADS), scal, 0.0)
    bc = _exact_cols_mm(scal, _bf(e_ref[...]))
    for d in range(2):
        g_s[d] = bc[:, d * SLAB:(d + 1) * SLAB]
        b_s[d] = bc[:, (2 + d) * SLAB:(3 + d) * SLAB]
    if has_s0:
        st_s[...] = _load_head_states(s0_ref, bd_f, transposed=False)
    else:
        st_s[...] = jnp.zeros_like(st_s)

    def prepare(i, carry):
        c0 = i * PREP_CHUNKS
        rows = [pl.ds(pl.multiple_of((c0 + j) * CH, CH), CH) for j in range(PREP_CHUNKS)]
        vals = _gdn_prepare([qn_s[r, :] for r in rows], [kn_s[r, :] for r in rows], [v_s[r, :] for r in rows],
                            [b_s[d, r, :] for r in rows for d in range(2)],
                            [g_s[d, r, :] for r in rows for d in range(2)], slab_ref, bd_b, tri_ref)
        for ref, chain_vals in zip((wv_s, wk_s, qk_s, qd_s, ke_s, eg_s), vals):
            for j in range(PREP_CHUNKS):
                ref[c0 + j] = jnp.concatenate(chain_vals[2 * j:2 * j + 2], axis=0)
        return carry

    lax.fori_loop(0, nc // PREP_CHUNKS, prepare, 0)

    def step(t, carry):
        s_all = st_s[...]
        dirs = (0, 1)
        cc = (t, nc - 1 - t)
        half = (slice(0, CH), slice(CH, 2 * CH))
        s = [s_all[d * SLAB:(d + 1) * SLAB] for d in dirs]
        s_b = [_bf(x) for x in s]
        u_b = [_bf(wv_s[cc[d], half[d], :] - _mm(wk_s[cc[d], half[d], :], s_b[d])) for d in dirs]
        upd = [_mm_tn(ke_s[cc[d], half[d], :], u_b[d]) for d in dirs]
        outs = [_mm(qd_s[cc[d], half[d], :], s_b[d]) + _mm(qk_s[cc[d], half[d], :], _blockdiag(u_b[d], bd_b))
                for d in dirs]
        new_s = [s[d] * eg_s[cc[d], d * 8:d * 8 + 1, :] + bd_f * upd[d] for d in dirs]
        st_s[...] = jnp.concatenate(new_s, axis=0)
        o_s[t] = jnp.concatenate(outs, axis=0)
        return carry

    lax.fori_loop(0, nc, step, 0)
    _store_head_states(_own_layer(sfin_ref, not has_s0 and n_aliased == 0), st_s[...], transposed=False)
    o = jnp.concatenate([o_s[c, 0:CH, :] + o_s[nc - 1 - c, CH:2 * CH, :] for c in range(nc)], axis=0)
    o = o * lax.rsqrt(_group_sum(o * o, bd_f) * (1.0 / HEAD_DIM) + EPS) * gain_ref[...]
    o_ref[...] = o * _silu(x_ref[:, 3 * SLAB:4 * SLAB])


def _gla_prepare(qs, ks, vs, las, slab_ref, bd_f, tri_ref):
    bd_b = _bf(bd_f)
    chains = [(j, d) for j in range(len(qs)) for d in (0, 1)]
    cums = [_exact_rows_mm(tri_ref[d], las[i]) for i, (j, d) in enumerate(chains)]
    lasts = [cums[i][0:1] if d == 1 else cums[i][CH - 1:CH] for i, (j, d) in enumerate(chains)]
    q_dec = [_bf(qs[j] * (HEAD_DIM ** -0.5) * jnp.exp(cums[i])) for i, (j, d) in enumerate(chains)]
    k_inv = [_bf(ks[j] * jnp.exp(-cums[i])) for i, (j, d) in enumerate(chains)]
    k_end = [_bf(ks[j] * jnp.exp(lasts[i] - cums[i])) for i, (j, d) in enumerate(chains)]
    v_b = [_bf(v) for v in vs]
    v_bd = [_blockdiag(v, bd_b) for v in v_b]
    att = [_mm_nt(q_dec[i], _blockdiag(k_inv[i], bd_b)) * slab_ref[2 * d] for i, (j, d) in enumerate(chains)]
    upd = [bd_f * _mm_tn(v_b[j], k_end[i]) for i, (j, d) in enumerate(chains)]
    intra = [_mm(_bf(att[i]), v_bd[j]) for i, (j, d) in enumerate(chains)]
    return intra, q_dec, upd, [jnp.broadcast_to(jnp.exp(l), (8, SLAB)) for l in lasts]


def _gla_kernel(n, has_s0, n_aliased, *refs):
    refs = list(refs)
    x_ref, small_ref = refs[0:2]
    k = 2
    if has_s0:
        s0_ref = refs[k]
        k += 1
    wup_ref, bup_ref, gain_ref, slab_ref, bd_ref, tri_ref = refs[k:k + 6]
    k += 6 + n_aliased
    o_ref, sfin_ref, la_s, oi_s, qd_s, up_s, el_s, o_s, st_s = refs[k:]
    nc = n // CH
    bd_f = bd_ref[...]
    small = small_ref[...]
    for d in range(2):
        z = _mm_x3(small, wup_ref[d]) + bup_ref[d:d + 1, :]
        la_s[d] = (jnp.minimum(z, 0.0) - jnp.log1p(jnp.exp(-jnp.abs(z)))) * (1.0 / GLA_TAU)
    if has_s0:
        st_s[...] = _load_head_states(s0_ref, bd_f, transposed=True)
    else:
        st_s[...] = jnp.zeros_like(st_s)

    def prepare(i, carry):
        c0 = i * PREP_CHUNKS
        rows = [pl.ds(pl.multiple_of((c0 + j) * CH, CH), CH) for j in range(PREP_CHUNKS)]
        vals = _gla_prepare([x_ref[r, 0:SLAB] for r in rows], [x_ref[r, SLAB:2 * SLAB] for r in rows],
                            [x_ref[r, 2 * SLAB:3 * SLAB] for r in rows],
                            [la_s[d, r, :] for r in rows for d in range(2)], slab_ref, bd_f, tri_ref)
        for ref, chain_vals in zip((oi_s, qd_s, up_s, el_s), vals):
            for j in range(PREP_CHUNKS):
                ref[c0 + j] = jnp.concatenate(chain_vals[2 * j:2 * j + 2], axis=0)
        return carry

    lax.fori_loop(0, nc // PREP_CHUNKS, prepare, 0)

    def step(t, carry):
        s_all = st_s[...]
        cc = (t, nc - 1 - t)
        s = [s_all[d * SLAB:(d + 1) * SLAB] for d in range(2)]
        outs = [oi_s[cc[d], d * CH:(d + 1) * CH, :] + _mm_nt(qd_s[cc[d], d * CH:(d + 1) * CH, :], _bf(s[d]))
                for d in range(2)]
        new_s = [s[d] * el_s[cc[d], d * 8:d * 8 + 1, :] + up_s[cc[d], d * SLAB:(d + 1) * SLAB, :] for d in range(2)]
        st_s[...] = jnp.concatenate(new_s, axis=0)
        o_s[t] = jnp.concatenate(outs, axis=0)
        return carry

    lax.fori_loop(0, nc, step, 0)
    _store_head_states(_own_layer(sfin_ref, not has_s0 and n_aliased == 0), st_s[...], transposed=True)
    o = jnp.concatenate([o_s[c, 0:CH, :] + o_s[nc - 1 - c, CH:2 * CH, :] for c in range(nc)], axis=0)
    o = o * lax.rsqrt(_group_sum(o * o, bd_f) * (1.0 / HEAD_DIM) + EPS) * gain_ref[...]
    o_ref[...] = o * _silu(x_ref[:, 3 * SLAB:4 * SLAB])


def _linear_mixer(kind, layer, hp, col_block, s0, params, consts, ctx_states=None):
    latent = s0 is not None
    n = DEC_SEQ if latent else SEQ
    n_seq = DEC_BATCH if latent else BATCH
    row0 = N_PROMPT_TOK // n if latent else 0
    full = lambda a: pl.BlockSpec(a.shape, lambda b: (0,) * a.ndim)
    state_block = (2, N_SLAB_HEADS, HEAD_DIM, HEAD_DIM)
    in_specs = [pl.BlockSpec((n, 4 * SLAB), lambda b: (row0 + b, col_block)),
                pl.BlockSpec((n, LANES), lambda b: (row0 + b, SMALL_COL_BLOCK))]
    args = [hp, hp]
    if latent:
        in_specs.append(pl.BlockSpec((None, None) + state_block, lambda b: (b, layer, 0, 0, 0, 0)))
        args.append(s0)
    for a in tuple(params) + tuple(consts):
        in_specs.append(full(a))
        args.append(a)
    aliases = {}
    if ctx_states is not None:
        aliases = {len(args): 1}
        in_specs.append(pl.BlockSpec(memory_space=pl.ANY))
        args.append(ctx_states)
    if latent:
        state_spec = pl.BlockSpec((None,) + state_block, lambda b: (b, 0, 0, 0, 0))
        state_shape = (n_seq,) + state_block
    elif ctx_states is None:
        assert layer == 0
        state_spec = pl.BlockSpec((None, DEPTH) + state_block, lambda b: (b, 0, 0, 0, 0, 0))
        state_shape = (n_seq, DEPTH) + state_block
    else:
        state_spec = pl.BlockSpec((None, None) + state_block, lambda b: (b, layer, 0, 0, 0, 0))
        state_shape = (n_seq, DEPTH) + state_block
    nc = n // CH
    seq_buf = pltpu.VMEM((n, SLAB), jnp.float32)
    dir_buf = pltpu.VMEM((2, n, SLAB), jnp.float32)
    pair_f32 = pltpu.VMEM((nc, 2 * CH, SLAB), jnp.float32)
    pair_b16 = pltpu.VMEM((nc, 2 * CH, SLAB), jnp.bfloat16)
    state_buf = pltpu.VMEM((2 * SLAB, SLAB), jnp.float32)
    if kind == 'gdn':
        body = functools.partial(_gdn_kernel, n, latent, len(aliases))
        scratch = [seq_buf, seq_buf, seq_buf, dir_buf, dir_buf, pair_f32, pair_b16, pair_b16, pair_b16, pair_b16,
                   pltpu.VMEM((nc, 16, SLAB), jnp.float32), pair_f32, state_buf]
    else:
        body = functools.partial(_gla_kernel, n, latent, len(aliases))
        scratch = [dir_buf, pair_f32, pair_b16, pltpu.VMEM((nc, 2 * SLAB, SLAB), jnp.float32),
                   pltpu.VMEM((nc, 16, SLAB), jnp.float32), pair_f32, state_buf]
    return pl.pallas_call(
        body,
        grid=(n_seq,),
        in_specs=in_specs,
        out_specs=[pl.BlockSpec((n, SLAB), lambda b: (b, 0)), state_spec],
        out_shape=[jax.ShapeDtypeStruct((n_seq * n, SLAB), jnp.float32),
                   jax.ShapeDtypeStruct(state_shape, jnp.float32)],
        scratch_shapes=scratch,
        input_output_aliases=aliases,
        compiler_params=_cparams("arbitrary"),
        name=kind + ("_latent" if latent else "_ctx"),
    )(*args)


ATT_SCALE = HEAD_DIM ** -0.5
NA_ROWS = DEC_SEQ // GRID_W
NA_KROWS = min(NA_KH, NA_ROWS)
NA_WIN = NA_KROWS * GRID_W
N_SWA_BLOCKS = DEC_SEQ // SWA_BLOCK
assert SWA_WINDOW == SWA_BLOCK
NA_GROUP = 4
assert NA_KROWS % 2 == 0 and NA_ROWS % NA_GROUP == 0


def _head_rmsnorm(x, bd_f, gain):
    return x * lax.rsqrt(_group_sum(x * x, bd_f) * (1.0 / HEAD_DIM) + EPS) * gain


def _lane_group(shape):
    return lax.broadcasted_iota(jnp.int32, shape, 1) // HEAD_DIM


def _stack_groups(x):
    grp = _lane_group(x.shape)
    return jnp.concatenate([jnp.where(grp == g, x, jnp.zeros_like(x)) for g in range(x.shape[1] // HEAD_DIM)], axis=0)


def _stack_swa_queries(q_b):
    return jnp.concatenate([_stack_groups(q_b[:, 0:LANES]), _stack_groups(q_b[:, LANES:2 * LANES])], axis=0)


def _unstack_swa(o, m):
    low = _lane_group((m, LANES)) == 0
    return jnp.concatenate([jnp.where(low, o[0:m], o[m:2 * m]), jnp.where(low, o[2 * m:3 * m], o[3 * m:4 * m])],
                           axis=1)


def _unstack_groups(o, m):
    grp = _lane_group((m, o.shape[1]))
    out = jnp.where(grp == 0, o[0:m], 0.0)
    for g in range(1, o.shape[1] // HEAD_DIM):
        out = out + jnp.where(grp == g, o[g * m:(g + 1) * m], 0.0)
    return out


def _softmax_pv_chains(chains):
    ms = []
    for logits, _, extra, _ in chains:
        m = jnp.max(logits[0], axis=-1, keepdims=True)
        for l in logits[1:]:
            m = jnp.maximum(m, jnp.max(l, axis=-1, keepdims=True))
        ms.append(m if extra is None else jnp.maximum(m, extra))
    es = [[jnp.exp(l - m) for l in c[0]] for c, m in zip(chains, ms)]
    dens = []
    for c, m, e in zip(chains, ms, es):
        den = jnp.exp(c[2] - m) if c[2] is not None else 0.0
        for piece in e:
            den = den + jnp.sum(piece, axis=-1, keepdims=True)
        dens.append(den)
    pvs = [0.0] * len(chains)
    for p in range(max(len(c[0]) for c in chains)):
        for ci, c in enumerate(chains):
            if p < len(c[0]):
                transposed = c[3] and p == len(c[0]) - 1
                e_b = _bf(es[ci][p])
                pvs[ci] = pvs[ci] + (_mm_nt(e_b, c[1][p]) if transposed else _mm(e_b, c[1][p]))
    return [pv / den for pv, den in zip(pvs, dens)]


def _sink_column(sink_ref, m):
    return jnp.concatenate([jnp.full((m, 1), sink_ref[g], jnp.float32) for g in range(SWA_HEADS)], axis=0)


def _rope(x, rope_ref):
    reps = x.shape[1] // LANES
    wide = lambda i: jnp.concatenate([rope_ref[i]] * reps, axis=1)
    shift = HEAD_DIM // 4
    return (x * wide(0) + pltpu.roll(x, x.shape[1] - shift, axis=1) * wide(1)
            + pltpu.roll(x, shift, axis=1) * wide(2))


def _store_cache(ref, slab):
    t = slab.T
    for h in range(ref.shape[0]):
        ref[h] = t[h * HEAD_DIM:(h + 1) * HEAD_DIM, :]


def _ctx_attn_kernel(n_aliased, bq_ref, bk_ref, bv_ref, cq_ref, ck_ref, cv_ref, gain_ref, sink_ref, bd_ref, *rest):
    o_ref = rest[n_aliased]
    swak_ref, swav_ref, nak_ref, nav_ref = (_own_layer(r, n_aliased == 0) for r in rest[n_aliased + 1:])
    n = SEQ
    bd_f = bd_ref[...]
    q = _head_rmsnorm(bq_ref[...], bd_f, gain_ref[0:1, :])
    k = _head_rmsnorm(bk_ref[...], bd_f[0:LANES, 0:LANES], gain_ref[1:2, 0:LANES])
    v = bv_ref[...]
    _store_cache(swak_ref, k)
    _store_cache(swav_ref, v)
    q2 = _head_rmsnorm(cq_ref[...], bd_f, gain_ref[2:3, :])
    k2 = _head_rmsnorm(ck_ref[...], bd_f, gain_ref[3:4, :])
    v2 = cv_ref[...]
    _store_cache(nak_ref, k2)
    _store_cache(nav_ref, v2)
    logits_b = _mm_nt(_stack_swa_queries(_bf(q)), _bf(k)) * ATT_SCALE
    logits_c = _mm_nt(_stack_groups(_bf(q2)), _bf(k2)) * ATT_SCALE
    out_b, out_c = _softmax_pv_chains([([logits_b], [_bf(v)], _sink_column(sink_ref, n), False),
                                       ([logits_c], [_bf(v2)], None, False)])
    o_ref[...] = jnp.concatenate([_unstack_swa(out_b, n), _unstack_groups(out_c, n)], axis=1)


def _load_cache_slabs(ck_ref, cv_ref, ck_s, cv_s):
    for src, dst in ((ck_ref, ck_s), (cv_ref, cv_s)):
        dst[...] = _bf(jnp.concatenate([src[h] for h in range(src.shape[0])], axis=0))


def _swa_latent_kernel(q_ref, k_ref, v_ref, ck_ref, cv_ref, rope_ref, gain_ref, sink_ref, bd_ref,
                       o_ref, q_s, k_s, v_s, ck_s, cv_s):
    _load_cache_slabs(ck_ref, cv_ref, ck_s, cv_s)
    bd_f = bd_ref[...]
    q_s[...] = _bf(_rope(_head_rmsnorm(q_ref[...], bd_f, gain_ref[0:1, :]), rope_ref))
    k_s[...] = _bf(_rope(_head_rmsnorm(k_ref[...], bd_f[0:LANES, 0:LANES], gain_ref[1:2, 0:LANES]), rope_ref))
    v_s[...] = _bf(v_ref[...])
    m = SWA_BLOCK
    iq = lax.broadcasted_iota(jnp.int32, (SWA_HEADS * m, m), 0) % m
    jk = lax.broadcasted_iota(jnp.int32, (SWA_HEADS * m, m), 1)
    ok_prev = jk >= iq
    ok_next = jk <= iq
    ok_same = jk >= 0
    sink = _sink_column(sink_ref, m)
    for i in range(N_SWA_BLOCKS):
        lo, hi = max(i - 1, 0), min(i + 1, N_SWA_BLOCKS - 1)
        qs = _stack_swa_queries(q_s[i * m:(i + 1) * m, :])
        allowed = jnp.concatenate([ok_prev if j < i else ok_next if j > i else ok_same for j in range(lo, hi + 1)],
                                  axis=1)
        l_loc = jnp.where(allowed, _mm_nt(qs, k_s[lo * m:(hi + 1) * m, :]) * ATT_SCALE, NEG_INF)
        l_ctx = _mm(qs, ck_s[...]) * ATT_SCALE
        o, = _softmax_pv_chains([([l_loc, l_ctx], [v_s[lo * m:(hi + 1) * m, :], cv_s[...]], sink, True)])
        o_ref[i * m:(i + 1) * m, :] = _unstack_swa(o, m)


def _na_latent_kernel(q_ref, k_ref, v_ref, ck_ref, cv_ref, bias_ref, gain_ref, bd_ref,
                      o_ref, q_s, k_s, v_s, ck_s, cv_s):
    _load_cache_slabs(ck_ref, cv_ref, ck_s, cv_s)
    bd_f = bd_ref[...]
    q_s[...] = _bf(_head_rmsnorm(q_ref[...], bd_f, gain_ref[2:3, :]))
    k_s[...] = _bf(_head_rmsnorm(k_ref[...], bd_f, gain_ref[3:4, :]))
    v_s[...] = _bf(v_ref[...])

    grp = range(NA_GROUP)

    def row_group(i, carry):
        rows = [NA_GROUP * i + a for a in grp]
        rs = [jnp.clip(r - NA_KROWS // 2, 0, NA_ROWS - NA_KROWS) for r in rows]
        qrows = [pl.ds(pl.multiple_of(r * GRID_W, GRID_W), GRID_W) for r in rows]
        wins = [pl.ds(pl.multiple_of(s * GRID_W, GRID_W), NA_WIN) for s in rs]
        qs = [_stack_groups(q_s[qr, :]) for qr in qrows]
        bias = [jnp.concatenate([bias_ref[s - r + NA_KH - 1 + 2 * p] for p in range(NA_KROWS // 2)], axis=1)
                for r, s in zip(rows, rs)]
        l_loc = [_mm_nt(qs[a], k_s[wins[a], :]) * ATT_SCALE + bias[a] for a in grp]
        l_ctx = [_mm(qs[a], ck_s[...]) * ATT_SCALE for a in grp]
        m = [jnp.maximum(jnp.max(l_loc[a], axis=-1, keepdims=True), jnp.max(l_ctx[a], axis=-1, keepdims=True))
             for a in grp]
        e_loc = [jnp.exp(l_loc[a] - m[a]) for a in grp]
        e_ctx = [jnp.exp(l_ctx[a] - m[a]) for a in grp]
        den = [jnp.sum(e_loc[a], axis=-1, keepdims=True) + jnp.sum(e_ctx[a], axis=-1, keepdims=True) for a in grp]
        pv_loc = [_mm(_bf(e_loc[a]), v_s[wins[a], :]) for a in grp]
        pv_ctx = [_mm_nt(_bf(e_ctx[a]), cv_s[...]) for a in grp]
        for a in grp:
            o_ref[qrows[a], :] = _unstack_groups((pv_loc[a] + pv_ctx[a]) / den[a], GRID_W)
        return carry

    lax.fori_loop(0, NA_ROWS // NA_GROUP, row_group, 0)


def _rope_tables():
    t = jnp.arange(DEC_SEQ)
    rows = (t // GRID_W).astype(jnp.float32)
    cols = (t % GRID_W).astype(jnp.float32)
    half = HEAD_DIM // 2
    nf = half // 2
    inv = 1.0 / (ROPE_BASE ** (jnp.arange(nf, dtype=jnp.float32) / nf))
    d = np.arange(LANES) % HEAD_DIM
    pos = jnp.where(jnp.asarray(d < half)[None, :], rows[:, None], cols[:, None])
    ang = pos * inv[jnp.asarray(d % nf)][None, :]
    first = jnp.asarray((d % half) < nf)[None, :]
    cos, sin = jnp.cos(ang), jnp.sin(ang)
    return jnp.stack([cos, jnp.where(first, -sin, 0.0), jnp.where(first, 0.0, sin)])


def _na_bias_table(rpb):
    col = np.arange(GRID_W)
    cs = np.clip(col - NA_KW // 2, 0, GRID_W - NA_KW)
    col_mask = (col[None, :] >= cs[:, None]) & (col[None, :] < cs[:, None] + NA_KW)
    dc = np.clip(col[None, :] - col[:, None], -(NA_KW - 1), NA_KW - 1) + NA_KW - 1
    pick = (np.arange(2 * NA_KW - 1)[:, None, None] == dc[None]).astype(np.float32)
    b = jnp.einsum('hrd,dqk->hrqk', rpb, jnp.asarray(pick), precision=_HI)
    b = jnp.where(jnp.asarray(col_mask)[None, None], b, NEG_INF)
    pair = jnp.concatenate([b[:, 0:2 * NA_KH - 2], b[:, 1:2 * NA_KH - 1]], axis=-1)
    return pair.transpose(1, 0, 2, 3).reshape(2 * NA_KH - 2, NA_HEADS * GRID_W, 2 * GRID_W)


def _attention_ctx(layer, hp, gains, sink, bd_c, caches):
    cb = lambda w, off: (lambda b: (b, off // w))
    q_w, kv_w = GROUP_WIDTH, SWA_KV_WIDTH
    full = lambda a: pl.BlockSpec(a.shape, lambda b: (0,) * a.ndim)
    if caches is None:
        assert layer == 0
        cache_out = lambda h: pl.BlockSpec((None, DEPTH, h, HEAD_DIM, SEQ), lambda b: (b, 0, 0, 0, 0))
    else:
        cache_out = lambda h: pl.BlockSpec((None, None, h, HEAD_DIM, SEQ), lambda b: (b, layer, 0, 0, 0))
    cache_shape = lambda h: jax.ShapeDtypeStruct((BATCH, DEPTH, h, HEAD_DIM, SEQ), jnp.float32)
    n_in = 9
    prev = [] if caches is None else list(caches)
    return pl.pallas_call(
        functools.partial(_ctx_attn_kernel, len(prev)),
        grid=(BATCH,),
        in_specs=[pl.BlockSpec((SEQ, q_w), cb(q_w, COL_B)),
                  pl.BlockSpec((SEQ, kv_w), cb(kv_w, COL_B + q_w)),
                  pl.BlockSpec((SEQ, kv_w), cb(kv_w, COL_B + q_w + kv_w)),
                  pl.BlockSpec((SEQ, q_w), cb(q_w, COL_C)),
                  pl.BlockSpec((SEQ, q_w), cb(q_w, COL_C + q_w)),
                  pl.BlockSpec((SEQ, q_w), cb(q_w, COL_C + 2 * q_w)),
                  full(gains), pl.BlockSpec(memory_space=pltpu.SMEM), full(bd_c)]
                 + [pl.BlockSpec(memory_space=pl.ANY)] * len(prev),
        out_specs=[pl.BlockSpec((SEQ, 2 * q_w), lambda b: (b, 0)),
                   cache_out(SWA_KV_HEADS), cache_out(SWA_KV_HEADS), cache_out(NA_HEADS), cache_out(NA_HEADS)],
        out_shape=[jax.ShapeDtypeStruct((N_PROMPT_TOK, 2 * q_w), jnp.float32),
                   cache_shape(SWA_KV_HEADS), cache_shape(SWA_KV_HEADS), cache_shape(NA_HEADS), cache_shape(NA_HEADS)],
        input_output_aliases={n_in + i: 1 + i for i in range(len(prev))},
        compiler_params=_cparams("arbitrary"),
        name="attn_ctx",
    )(hp, hp, hp, hp, hp, hp, gains, sink, bd_c, *prev)


def _attention_latent(kind, layer, hp, ck, cv, table, gains, sink, bd_c):
    n = DEC_SEQ
    row0 = N_PROMPT_TOK // n
    q_w = GROUP_WIDTH
    kv_w = SWA_KV_WIDTH if kind == 'swa' else q_w
    col = COL_B if kind == 'swa' else COL_C
    cb = lambda w, off: (lambda b: (row0 + b, off // w))
    full = lambda a: pl.BlockSpec(a.shape, lambda b: (0,) * a.ndim)
    cache_spec = pl.BlockSpec((None, None, kv_w // HEAD_DIM, HEAD_DIM, PAST_LEN), lambda b: (b, layer, 0, 0, 0))
    in_specs = [pl.BlockSpec((n, q_w), cb(q_w, col)),
                pl.BlockSpec((n, kv_w), cb(kv_w, col + q_w)),
                pl.BlockSpec((n, kv_w), cb(kv_w, col + q_w + kv_w)),
                cache_spec, cache_spec,
                full(table), full(gains)]
    args = [hp, hp, hp, ck, cv, table, gains]
    if kind == 'swa':
        in_specs.append(pl.BlockSpec(memory_space=pltpu.SMEM))
        args.append(sink)
    in_specs.append(full(bd_c))
    args.append(bd_c)
    return pl.pallas_call(
        _swa_latent_kernel if kind == 'swa' else _na_latent_kernel,
        grid=(DEC_BATCH,),
        in_specs=in_specs,
        out_specs=pl.BlockSpec((n, q_w), lambda b: (b, 0)),
        out_shape=jax.ShapeDtypeStruct((N_SAMPLE_TOK, q_w), jnp.float32),
        scratch_shapes=[pltpu.VMEM((n, q_w), jnp.bfloat16), pltpu.VMEM((n, kv_w), jnp.bfloat16),
                        pltpu.VMEM((n, kv_w), jnp.bfloat16), pltpu.VMEM((kv_w, PAST_LEN), jnp.bfloat16),
                        pltpu.VMEM((kv_w, PAST_LEN), jnp.bfloat16)],
        compiler_params=_cparams("arbitrary"),
        name=kind + "_latent",
    )(*args)


def kernel(x_prompt, x_sample, cache_swa_k, cache_swa_v, cache_na_k, cache_na_v, state_gla, state_gdn, c, c_ctx, w_mod, b_mod, norm1_g, norm2_g, w_in, w_out, gla_wup, gla_bup, gla_onorm, swa_qnorm, swa_knorm, swa_sink, na_qnorm, na_knorm, na_rpb, gdn_conv, gdn_alog, gdn_dtbias, gdn_onorm, ffn_w1, ffn_w3, ffn_w2, moe_router, moe_w1, moe_w3, moe_w2):
    bf16 = jnp.bfloat16
    x = jnp.concatenate([x_prompt.reshape(N_PROMPT_TOK, D_MODEL), x_sample.reshape(N_SAMPLE_TOK, D_MODEL)], axis=0)
    cvec = jnp.concatenate([c_ctx[None, :], c, jnp.zeros((N_MOD_ROWS - 1 - DEC_BATCH, D_MODEL), jnp.float32)], axis=0)
    mod_all = _modulation_all(cvec, w_mod, b_mod).reshape(DEPTH, N_MOD_ROWS, 1, MOD_W)
    w_in_b = jnp.concatenate([_take_segments(w_in.astype(bf16), _in_col_segments(), 2),
                              jnp.zeros((DEPTH, D_MODEL, IN_COLS_PAD - IN_COLS), bf16)], axis=2)
    w_out_b = _take_segments(w_out.astype(bf16), _OUT_ROW_SEGMENTS, 1)
    rope_tables = _rope_tables()
    consts = _linear_consts()
    tile_heads = lambda g: jnp.tile(g, N_SLAB_HEADS)[None, :]
    small_expand = jnp.asarray(_small_expand_matrix())
    ffn_w = [w.astype(bf16) for w in (ffn_w1, ffn_w3, ffn_w2)]
    moe_w = (moe_w1, moe_w3, moe_w2)
    ctx_kv = [jnp.swapaxes(c_, -1, -2) for c_ in (cache_swa_k, cache_swa_v, cache_na_k, cache_na_v)]

    new_kv = st_a = st_d = None
    for l in range(DEPTH):
        mod = mod_all[l]
        hp = _in_proj(l, x, mod, norm1_g[l][None, :], w_in_b)

        wup = jnp.zeros((2, LANES, SLAB), jnp.float32)
        wup = wup.at[0, 0:GLA_LOWRANK].set(gla_wup[l, 0]).at[1, GLA_LOWRANK:2 * GLA_LOWRANK].set(gla_wup[l, 1])
        gla_params = (wup, gla_bup[l], tile_heads(gla_onorm[l]))
        a_ctx, st_a = _linear_mixer('gla', l, hp, COL_A // (4 * SLAB), None, gla_params, consts, st_a)
        a_lat, _ = _linear_mixer('gla', l, hp, COL_A // (4 * SLAB), state_gla, gla_params, consts)
        decay_cols = slice(SMALL_GDN, SMALL_GDN + 2 * GDN_HEADS)
        decay_params = jnp.zeros((2, LANES), jnp.float32).at[:, decay_cols].set(
            jnp.stack([gdn_alog[l].reshape(-1), gdn_dtbias[l].reshape(-1)]))
        gdn_params = (gdn_conv[l], decay_params, tile_heads(gdn_onorm[l]), small_expand)
        d_ctx, st_d = _linear_mixer('gdn', l, hp, COL_D // (4 * SLAB), None, gdn_params, consts, st_d)
        d_lat, _ = _linear_mixer('gdn', l, hp, COL_D // (4 * SLAB), state_gdn, gdn_params, consts)

        gains = jnp.stack([jnp.tile(g, N_SLAB_HEADS) for g in (swa_qnorm[l], swa_knorm[l], na_qnorm[l], na_knorm[l])])
        sink = swa_sink[l][jnp.asarray(SWA_Q_HEAD_ORDER)]
        bc_ctx, *new_kv = _attention_ctx(l, hp, gains, sink, consts[1], new_kv)
        b_lat = _attention_latent('swa', l, hp, ctx_kv[0], ctx_kv[1], rope_tables, gains, sink, consts[1])
        c_lat = _attention_latent('na', l, hp, ctx_kv[2], ctx_kv[3], _na_bias_table(na_rpb[l]), gains, None,
                                  consts[1])
        x = _out_proj(l, (a_ctx, bc_ctx, d_ctx), (a_lat, b_lat, c_lat, d_lat), x, mod, w_out_b)
        if l % 2 == 0:
            x = _ffn(l // 2, x, mod, norm2_g[l][None, :], *ffn_w)
        else:
            x = _moe_routed(l // 2, x, mod, norm2_g[l][None, :], moe_router[l // 2], *moe_w,
                            split_streams=(l == DEPTH - 1))

    outs = [jnp.swapaxes(o, -1, -2) for o in new_kv] + [st_a, st_d]
    assert DEPTH % 2 == 0
    y_prompt = x[0].reshape(BATCH, SEQ, D_MODEL)
    y_sample = x[1].reshape(DEC_BATCH, DEC_SEQ, D_MODEL)
    return (y_prompt, y_sample, *outs)
```
